```python
import math
import jax, jax.numpy as jnp
from jax import lax
import numpy as np

D_MODEL = 2048
BATCH = 8
SEQ = 2048
DEPTH = 2

CTX_LEN = 256
GRID_W = 64
F32 = jnp.float32
EPS = 1e-6
N_MOD = 9
FFN_RESIDUAL = 0.5
D_FF = 5632
N_BRANCH = 4
BRANCH_W = 512
HEAD_DIM = 128
SG_GROUPS = 4
SG_CHUNK = 128
ML_HEADS = 4
ML_CHUNK = 128
GD_HEADS = 4
GD_CHUNK = 64
GD_CONV = 3
AT_Q_HEADS = 4
AT_KV_HEADS = 2
AT_BLOCK = 128
ROPE_THETA = 10000.0

COLS = (
    ("ml_kv", 2 * BRANCH_W),
    ("ml_if", 4 * ML_HEADS),
    ("gd_kv", 2 * BRANCH_W),
    ("gd_ba", 4 * GD_HEADS),
    ("at_kv", 2 * AT_KV_HEADS * HEAD_DIM),
    ("sg_uv", 2 * BRANCH_W),
    ("ml_q", BRANCH_W),
    ("ml_o", BRANCH_W),
    ("gd_q", BRANCH_W),
    ("gd_z", BRANCH_W),
    ("at_q", AT_Q_HEADS * HEAD_DIM),
    ("gate", N_BRANCH * D_MODEL),
)
KV_COLS = 2 * BRANCH_W + 4 * ML_HEADS + 2 * BRANCH_W + 4 * GD_HEADS + 2 * AT_KV_HEADS * HEAD_DIM
IN_COLS = KV_COLS + 2 * BRANCH_W + 5 * BRANCH_W + N_BRANCH * D_MODEL

kernel_name = "hybrid_gated_branch_dit_block"


def _col(z, name):
    off = 0
    for n, w in COLS:
        if n == name:
            return z[..., off:off + w]
        off += w
    raise KeyError(name)


def rms_norm(x, g):
    xf = x.astype(F32)
    y = xf * lax.rsqrt(jnp.mean(xf * xf, axis=-1, keepdims=True) + EPS)
    return (y * g.astype(F32)).astype(x.dtype)


def _l2norm(x):
    return x * lax.rsqrt(jnp.sum(x * x, axis=-1, keepdims=True) + EPS)


def modulate(h, shift, scale):
    return h * (1.0 + scale) + shift


def swiglu(h, w_in, w_out):
    a, b = jnp.split(h @ w_in, 2, axis=-1)
    return (jax.nn.silu(a) * b) @ w_out


def _heads(t, n_heads):
    Bn, T, W = t.shape
    return t.reshape(Bn, T, n_heads, W // n_heads).transpose(0, 2, 1, 3)


def _dwconv(x, w):
    K, C = w.shape
    return lax.conv_general_dilated(
        x, w[:, None, :].astype(x.dtype), window_strides=(1,), padding=[(K // 2, K // 2)],
        dimension_numbers=("NWC", "WIO", "NWC"), feature_group_count=C)


def axial_rope_tables(rows):
    n = HEAD_DIM // 4
    inv = ROPE_THETA ** (-jnp.arange(n, dtype=F32) / n)
    r = jnp.repeat(jnp.arange(rows, dtype=F32), GRID_W)
    cl = jnp.tile(jnp.arange(GRID_W, dtype=F32), rows)
    ar = r[:, None] * inv
    ac = cl[:, None] * inv
    return (jnp.cos(ar), jnp.sin(ar), jnp.cos(ac), jnp.sin(ac))


def _rope_1d(x, cos, sin):
    x1, x2 = jnp.split(x, 2, axis=-1)
    return jnp.concatenate([x1 * cos - x2 * sin, x2 * cos + x1 * sin], axis=-1)


def axial_rope(x, tabs):
    cos_r, sin_r, cos_c, sin_c = tabs
    e = lambda t: t[None, :, None, :]
    xr, xc = jnp.split(x.astype(F32), 2, axis=-1)
    y = jnp.concatenate([_rope_1d(xr, e(cos_r), e(sin_r)), _rope_1d(xc, e(cos_c), e(sin_c))], axis=-1)
    return y.astype(x.dtype)


def spatial_gating(z_uv, norm_g, w_s, b_s):
    Bn, T, _ = z_uv.shape
    u, v = jnp.split(jax.nn.gelu(z_uv), 2, axis=-1)
    v = rms_norm(v, norm_g)
    v = v.reshape(Bn, T // SG_CHUNK, SG_CHUNK, SG_GROUPS, BRANCH_W // SG_GROUPS)
    s = jnp.einsum("gts,bnsgc->bntgc", w_s.astype(v.dtype), v) + b_s.T[None, None, :, :, None]
    return u * s.reshape(Bn, T, BRANCH_W)


def _bidir(scan_fn, q, k, v, a_fw, b_fw, a_bw, b_bw, st_fw, st_bw, with_out):
    out_f, fin_f = scan_fn(q, k, v, a_fw, b_fw, st_fw, with_out)
    fl = lambda t: None if t is None else jnp.flip(t, axis=2)
    out_b, fin_b = scan_fn(fl(q), fl(k), fl(v), jnp.flip(a_bw, -1), jnp.flip(b_bw, -1), st_bw, with_out)
    out = (out_f + jnp.flip(out_b, axis=2)) if with_out else None
    return out, fin_f, fin_b


def mlstm_scan(q, k, v, logi, logf, state, with_out=True):
    Bn, H, T, d = k.shape
    L = ML_CHUNK
    N = T // L
    ch = lambda t: t.reshape(Bn, H, N, L, *t.shape[3:])
    k = ch(k) * (d ** -0.5)
    v = ch(v)
    li = ch(logi)
    lf = ch(logf)
    b = jnp.cumsum(lf, axis=-1)
    b_end = b[..., -1]
    a = b_end[..., None] - b + li
    m_loc = jnp.max(a, axis=-1)
    wgt = jnp.exp(a - m_loc[..., None])
    C_loc = jnp.einsum("bhnl,bhnle,bhnld->bhned", wgt, v, k)
    n_loc = jnp.einsum("bhnl,bhnld->bhnd", wgt, k)

    def step(carry, xs):
        C, n, m = carry
        Cl, nl, ml, be = xs
        m_new = jnp.maximum(be + m, ml)
        sp = jnp.exp(be + m - m_new)
        sl = jnp.exp(ml - m_new)
        new = (sp[..., None, None] * C + sl[..., None, None] * Cl, sp[..., None] * n + sl[..., None] * nl, m_new)
        return new, (carry if with_out else None)

    mv = lambda t: jnp.moveaxis(t, 2, 0)
    final, prev = lax.scan(step, state, (mv(C_loc), mv(n_loc), mv(m_loc), mv(b_end)))
    if not with_out:
        return None, final
    C_prev, n_prev, m_prev = (jnp.moveaxis(t, 0, 2) for t in prev)
    q = ch(q)
    incl = jnp.tril(jnp.ones((L, L), dtype=bool))
    Dm = jnp.where(incl, b[..., :, None] - b[..., None, :] + li[..., None, :], -jnp.inf)
    g = b + m_prev[..., None]
    m_t = jnp.maximum(jnp.max(Dm, axis=-1), g)
    P = jnp.exp(Dm - m_t[..., None]) * jnp.einsum("bhntd,bhnsd->bhnts", q, k)
    inter = jnp.exp(g - m_t)
    num = jnp.einsum("bhnts,bhnse->bhnte", P, v) + inter[..., None] * jnp.einsum("bhned,bhntd->bhnte", C_prev, q)
    den = jnp.sum(P, axis=-1) + inter * jnp.einsum("bhnd,bhntd->bhnt", n_prev, q)
    h = num / jnp.maximum(jnp.abs(den), jnp.exp(-m_t))[..., None]
    return h.reshape(Bn, H, T, d), final


def mlstm_branch(zc, zl, p, ctx_out):
    def prep(z, with_q):
        k, v = jnp.split(_col(z, "ml_kv"), 2, axis=-1)
        k = _heads(k, ML_HEADS).astype(F32)
        v = _heads(v, ML_HEADS).astype(F32)
        pre = (_col(z, "ml_if") + p["ml_if_bias"]).astype(F32)
        Bn, T, _ = pre.shape
        pre = pre.reshape(Bn, T, 4, ML_HEADS).transpose(2, 0, 3, 1)
        gates = (pre[0], jax.nn.log_sigmoid(pre[2]), pre[1], jax.nn.log_sigmoid(pre[3]))
        q = _heads(_col(z, "ml_q"), ML_HEADS).astype(F32) if with_q else None
        return q, k, v, gates

    def out(h, z):
        Bn, H, T, d = h.shape
        h = rms_norm(h.transpose(0, 2, 1, 3), p["ml_norm"].reshape(ML_HEADS, HEAD_DIM))
        return (jax.nn.sigmoid(_col(z, "ml_o").astype(F32)) * h.reshape(Bn, T, H * d)).astype(z.dtype)

    qc, kc, vc, gc = prep(zc, ctx_out)
    Bn = kc.shape[0]
    zero = (jnp.zeros((Bn, ML_HEADS, HEAD_DIM, HEAD_DIM), F32), jnp.zeros((Bn, ML_HEADS, HEAD_DIM), F32),
            jnp.zeros((Bn, ML_HEADS), F32))
    hc, st_fw, st_bw = _bidir(mlstm_scan, qc, kc, vc, *gc, zero, zero, ctx_out)
    ql, kl, vl, gl = prep(zl, True)
    hl, _, _ = _bidir(mlstm_scan, ql, kl, vl, *gl, st_fw, st_bw, True)
    return (out(hc, zc) if ctx_out else None), out(hl, zl)


def gdn_scan(q, k, v, g, beta, S0, with_out=True):
    Bn, H, T, d = k.shape
    L = GD_CHUNK
    N = T // L
    ch = lambda t: t.reshape(Bn, H, N, L, *t.shape[3:])
    k, v, g, beta = ch(k), ch(v), ch(g), ch(beta)
    gc = jnp.cumsum(g, axis=-1)
    incl = jnp.tril(jnp.ones((L, L), dtype=bool))
    decay = jnp.exp(jnp.where(incl, gc[..., :, None] - gc[..., None, :], -jnp.inf))
    kb = k * beta[..., None]
    M = jnp.tril(jnp.einsum("bhnid,bhnjd->bhnij", kb, k) * decay, -1)
    eye = jnp.eye(L, dtype=M.dtype)
    Tinv = lax.linalg.triangular_solve(M + eye, jnp.broadcast_to(eye, M.shape), left_side=True,
                                       lower=True, unit_diagonal=True)
    u = jnp.einsum("bhnij,bhnjd->bhnid", Tinv, v * beta[..., None])
    w = jnp.einsum("bhnij,bhnjd->bhnid", Tinv, kb * jnp.exp(gc)[..., None])
    k_end = k * jnp.exp(gc[..., -1:] - gc)[..., None]
    g_end = jnp.exp(gc[..., -1])
    mv = lambda t: jnp.moveaxis(t, 2, 0)
    if with_out:
        q = ch(q) * (d ** -0.5)
        qk = jnp.einsum("bhnid,bhnjd->bhnij", q, k) * decay
        q_dec = q * jnp.exp(gc)[..., None]

        def step(S, xs):
            u_n, w_n, ke_n, ge_n, qk_n, qd_n = xs
            v_new = u_n - jnp.einsum("bhld,bhde->bhle", w_n, S)
            o = jnp.einsum("bhld,bhde->bhle", qd_n, S) + jnp.einsum("bhij,bhje->bhie", qk_n, v_new)
            S = S * ge_n[..., None, None] + jnp.einsum("bhld,bhle->bhde", ke_n, v_new)
            return S, o

        S_fin, o = lax.scan(step, S0, (mv(u), mv(w), mv(k_end), mv(g_end), mv(qk), mv(q_dec)))
        return jnp.moveaxis(o, 0, 2).reshape(Bn, H, T, d), S_fin

    def step_state(S, xs):
        u_n, w_n, ke_n, ge_n = xs
        v_new = u_n - jnp.einsum("bhld,bhde->bhle", w_n, S)
        return S * ge_n[..., None, None] + jnp.einsum("bhld,bhle->bhde", ke_n, v_new), None

    S_fin, _ = lax.scan(step_state, S0, (mv(u), mv(w), mv(k_end), mv(g_end)))
    return None, S_fin


def gdn_branch(zc, zl, p, ctx_out):
    def prep(z, with_q):
        kv = jax.nn.silu(_dwconv(_col(z, "gd_kv"), p["gd_conv"][:, BRANCH_W:]))
        k, v = jnp.split(kv, 2, axis=-1)
        k = _l2norm(_heads(k, GD_HEADS).astype(F32))
        v = _heads(v, GD_HEADS).astype(F32)
        ba = _col(z, "gd_ba").astype(F32)
        Bn, T, _ = ba.shape
        ba = ba.reshape(Bn, T, 4, GD_HEADS).transpose(2, 0, 3, 1)
        A = jnp.exp(p["gd_a_log"].astype(F32))[:, None, :, None]
        dtb = p["gd_dt_bias"].astype(F32)[:, None, :, None]
        g = -A * jax.nn.softplus(ba[2:] + dtb)
        beta = jax.nn.sigmoid(ba[:2])
        q = None
        if with_q:
            q = jax.nn.silu(_dwconv(_col(z, "gd_q"), p["gd_conv"][:, :BRANCH_W]))
            q = _l2norm(_heads(q, GD_HEADS).astype(F32))
        return q, k, v, (g[0], beta[0], g[1], beta[1])

    def out(o, z):
        Bn, H, T, d = o.shape
        o = rms_norm(o.transpose(0, 2, 1, 3), p["gd_norm"])
        return (o.reshape(Bn, T, H * d) * jax.nn.silu(_col(z, "gd_z").astype(F32))).astype(z.dtype)

    qc, kc, vc, gc = prep(zc, ctx_out)
    Bn = kc.shape[0]
    zero = jnp.zeros((Bn, GD_HEADS, HEAD_DIM, HEAD_DIM), F32)
    oc, st_fw, st_bw = _bidir(gdn_scan, qc, kc, vc, *gc, zero, zero, ctx_out)
    ql, kl, vl, gl = prep(zl, True)
    ol, _, _ = _bidir(gdn_scan, ql, kl, vl, *gl, st_fw, st_bw, True)
    return (out(oc, zc) if ctx_out else None), out(ol, zl)


def block_attention(q, k, v):
    Bn, Tq, Hq, d = q.shape
    Hkv = k.shape[2]
    G = Hq // Hkv
    nb = Tq // AT_BLOCK
    qb = q.reshape(Bn, nb, AT_BLOCK, Hkv, G, d).transpose(1, 0, 3, 4, 2, 5)
    kt = k.transpose(0, 2, 1, 3)
    vt = v.transpose(0, 2, 1, 3)
    scale = d ** -0.5

    def one(qblk):
        s = jnp.einsum("bhgqd,bhkd->bhgqk", qblk, kt).astype(F32) * scale
        pr = jax.nn.softmax(s, axis=-1).astype(vt.dtype)
        return jnp.einsum("bhgqk,bhkd->bhgqd", pr, vt)

    o = lax.map(one, qb)
    return o.transpose(1, 0, 4, 2, 3, 5).reshape(Bn, Tq, Hq * d)


def attn_branch(zc, zl, p, rope, ctx_out):
    def kv(z):
        k, v = jnp.split(_col(z, "at_kv"), 2, axis=-1)
        Bn, T, _ = k.shape
        k = rms_norm(k.reshape(Bn, T, AT_KV_HEADS, HEAD_DIM), p["at_k_norm"])
        return k, v.reshape(Bn, T, AT_KV_HEADS, HEAD_DIM)

    def qry(z):
        q = _col(z, "at_q")
        Bn, T, _ = q.shape
        return rms_norm(q.reshape(Bn, T, AT_Q_HEADS, HEAD_DIM), p["at_q_norm"])

    kc, vc = kv(zc)
    kl, vl = kv(zl)
    kl = axial_rope(kl, rope)
    ql = axial_rope(qry(zl), rope)
    yl = block_attention(ql, jnp.concatenate([kc, kl], axis=1), jnp.concatenate([vc, vl], axis=1))
    yc = block_attention(qry(zc), kc, vc) if ctx_out else None
    return yc, yl


def merge_branches(ys, z, w_branch, w_out):
    gates = _col(z, "gate")
    Bn, T, _ = gates.shape
    gates = gates.reshape(Bn, T, N_BRANCH, D_MODEL)
    merged = sum(jax.nn.sigmoid(gates[:, :, n]) * (ys[n].astype(z.dtype) @ w_branch[n]) for n in range(N_BRANCH))
    return merged @ w_out


def trunk_layer(x, xc, c, c_ctx, p, rope, last):
    mod_l = (jax.nn.silu(c) @ p["mod_w"] + p["mod_b"])[:, None, :]
    mod_c = (jax.nn.silu(c_ctx) @ p["mod_w"] + p["mod_b"])[None, None, :]
    ml = jnp.split(mod_l, N_MOD, axis=-1)
    mc = jnp.split(mod_c, N_MOD, axis=-1)
    ctx_out = not last

    x = x + FFN_RESIDUAL * ml[2] * swiglu(modulate(rms_norm(x, p["ffn1_norm"]), ml[0], ml[1]),
                                          p["ffn1_w_in"], p["ffn1_w_out"])
    xc = xc + FFN_RESIDUAL * mc[2] * swiglu(modulate(rms_norm(xc, p["ffn1_norm"]), mc[0], mc[1]),
                                            p["ffn1_w_in"], p["ffn1_w_out"])

    hl = modulate(rms_norm(x, p["mix_norm"]), ml[3], ml[4])
    hc = modulate(rms_norm(xc, p["mix_norm"]), mc[3], mc[4])
    zl = hl @ p["w_in"]
    zc = hc @ (p["w_in"] if ctx_out else p["w_in"][:, :KV_COLS])
    y_sg_l = spatial_gating(_col(zl, "sg_uv"), p["sg_norm"], p["sg_w"], p["sg_b"])
    y_ml_c, y_ml_l = mlstm_branch(zc, zl, p, ctx_out)
    y_gd_c, y_gd_l = gdn_branch(zc, zl, p, ctx_out)
    y_at_c, y_at_l = attn_branch(zc, zl, p, rope, ctx_out)
    x = x + ml[5] * merge_branches([y_sg_l, y_ml_l, y_gd_l, y_at_l], zl, p["w_branch"], p["w_out"])
    if ctx_out:
        y_sg_c = spatial_gating(_col(zc, "sg_uv"), p["sg_norm"], p["sg_w"], p["sg_b"])
        xc = xc + mc[5] * merge_branches([y_sg_c, y_ml_c, y_gd_c, y_at_c], zc, p["w_branch"], p["w_out"])

    x = x + FFN_RESIDUAL * ml[8] * swiglu(modulate(rms_norm(x, p["ffn2_norm"]), ml[6], ml[7]),
                                          p["ffn2_w_in"], p["ffn2_w_out"])
    if ctx_out:
        xc = xc + FFN_RESIDUAL * mc[8] * swiglu(modulate(rms_norm(xc, p["ffn2_norm"]), mc[6], mc[7]),
                                                p["ffn2_w_in"], p["ffn2_w_out"])
    return x, xc


def setup_inputs(seed: int = 0) -> dict:
    key = jax.random.key(seed)
    ks = iter(jax.random.split(key, 40))
    L = DEPTH

    def normal(shape, scale):
        return scale * jax.random.normal(next(ks), shape, F32)

    def gain(shape):
        return 1.0 + 0.02 * jax.random.normal(next(ks), shape, F32)

    x = normal((BATCH, SEQ, D_MODEL), 1.0)
    c = normal((BATCH, D_MODEL), 1.0)
    ctx = normal((BATCH, CTX_LEN, D_MODEL), 1.0)
    c_ctx = normal((D_MODEL,), 1.0)
    mod_w = normal((L, D_MODEL, N_MOD * D_MODEL), 0.5 * D_MODEL ** -0.5)
    mod_b = normal((L, N_MOD * D_MODEL), 0.01)
    ffn1_norm = gain((L, D_MODEL))
    ffn1_w_in = normal((L, D_MODEL, 2 * D_FF), D_MODEL ** -0.5)
    ffn1_w_out = normal((L, D_FF, D_MODEL), D_FF ** -0.5)
    mix_norm = gain((L, D_MODEL))
    w_in = normal((L, D_MODEL, IN_COLS), D_MODEL ** -0.5)
    sg_norm = gain((L, BRANCH_W))
    sg_w = normal((L, SG_GROUPS, SG_CHUNK, SG_CHUNK), SG_CHUNK ** -0.5)
    sg_b = normal((L, SG_GROUPS, SG_CHUNK), 0.02)
    ml_if_bias = jnp.concatenate([normal((L, 2 * ML_HEADS), 0.1),
                                  3.0 + normal((L, 2 * ML_HEADS), 0.1)], axis=-1)
    ml_norm = gain((L, BRANCH_W))
    gd_conv = normal((L, GD_CONV, 3 * BRANCH_W), GD_CONV ** -0.5)
    gd_a_log = jnp.log(jax.random.uniform(next(ks), (L, 2, GD_HEADS), F32, 1.0, 16.0))
    dt = jnp.exp(jax.random.uniform(next(ks), (L, 2, GD_HEADS), F32, math.log(1e-3), math.log(1e-1)))
    gd_dt_bias = dt + jnp.log(-jnp.expm1(-dt))
    gd_norm = gain((L, HEAD_DIM))
    at_q_norm = gain((L, HEAD_DIM))
    at_k_norm = gain((L, HEAD_DIM))
    w_branch = normal((L, N_BRANCH, BRANCH_W, D_MODEL), BRANCH_W ** -0.5)
    w_out = normal((L, D_MODEL, D_MODEL), D_MODEL ** -0.5)
    ffn2_norm = gain((L, D_MODEL))
    ffn2_w_in = normal((L, D_MODEL, 2 * D_FF), D_MODEL ** -0.5)
    ffn2_w_out = normal((L, D_FF, D_MODEL), D_FF ** -0.5)
    return {"x": x, "c": c, "ctx": ctx, "c_ctx": c_ctx, "mod_w": mod_w, "mod_b": mod_b,
            "ffn1_norm": ffn1_norm, "ffn1_w_in": ffn1_w_in, "ffn1_w_out": ffn1_w_out,
            "mix_norm": mix_norm, "w_in": w_in, "sg_norm": sg_norm, "sg_w": sg_w, "sg_b": sg_b,
            "ml_if_bias": ml_if_bias, "ml_norm": ml_norm, "gd_conv": gd_conv, "gd_a_log": gd_a_log,
            "gd_dt_bias": gd_dt_bias, "gd_norm": gd_norm, "at_q_norm": at_q_norm, "at_k_norm": at_k_norm,
            "w_branch": w_branch, "w_out": w_out, "ffn2_norm": ffn2_norm, "ffn2_w_in": ffn2_w_in,
            "ffn2_w_out": ffn2_w_out}


def reference(x, c, ctx, c_ctx, mod_w, mod_b, ffn1_norm, ffn1_w_in, ffn1_w_out, mix_norm, w_in,
              sg_norm, sg_w, sg_b, ml_if_bias, ml_norm, gd_conv, gd_a_log, gd_dt_bias, gd_norm,
              at_q_norm, at_k_norm, w_branch, w_out, ffn2_norm, ffn2_w_in, ffn2_w_out):
    ROWS = x.shape[1] // GRID_W
    rope = axial_rope_tables(ROWS)
    xc = ctx
    for l in range(DEPTH):
        p = dict(mod_w=mod_w[l], mod_b=mod_b[l], ffn1_norm=ffn1_norm[l], ffn1_w_in=ffn1_w_in[l],
                 ffn1_w_out=ffn1_w_out[l], mix_norm=mix_norm[l], w_in=w_in[l], sg_norm=sg_norm[l],
                 sg_w=sg_w[l], sg_b=sg_b[l], ml_if_bias=ml_if_bias[l], ml_norm=ml_norm[l],
                 gd_conv=gd_conv[l], gd_a_log=gd_a_log[l], gd_dt_bias=gd_dt_bias[l], gd_norm=gd_norm[l],
                 at_q_norm=at_q_norm[l], at_k_norm=at_k_norm[l], w_branch=w_branch[l], w_out=w_out[l],
                 ffn2_norm=ffn2_norm[l], ffn2_w_in=ffn2_w_in[l], ffn2_w_out=ffn2_w_out[l])
        x, xc = trunk_layer(x, xc, c, c_ctx, p, rope, l == DEPTH - 1)
    return x
```

```python
import functools
import math

import jax
import jax.numpy as jnp
from jax import lax
from jax.experimental import pallas as pl
from jax.experimental.pallas import tpu as pltpu

F32 = jnp.float32
BF16 = jnp.bfloat16
EPS = 1e-6

D_MODEL = 2048
D_FF = 5632
N_MOD = 9
N_BRANCH = 4
BRANCH_W = 512
HEAD_DIM = 128
N_HEADS = 4
SG_GROUPS = 4
ML_CHUNK = 128
GD_CHUNK = 64
AT_Q_HEADS = 4
AT_KV_HEADS = 2
GRID_W = 64
ROPE_THETA = 10000.0
FFN_RESIDUAL = 0.5

LANES = 128
N_GATE = 8
MOD_ROWS = 16

C_ML_K, C_ML_V = 0, 512
C_GD_K, C_GD_V = 1024, 1536
C_AT_K, C_AT_V = 2048, 2304
KV_MAIN = 2560
C_SG_U, C_SG_V = 2560, 3072
C_ML_Q, C_ML_O = 3584, 4096
C_GD_Q, C_GD_Z = 4608, 5120
C_AT_Q = 5632
N_MAIN = 6144
R_ML_IF = 1024
R_GD_KV = 1040
R_GD_BA = 2064
R_AT_KV = 2080
R_REST = 2592
R_GATE = 6176

VMEM_LIMIT = 56 * 1024 * 1024

TM_FFN = 512
TF_FFN = 512
TM_PROJ = 512
TN_PROJ = 512
TM_MERGE = 512
TN_MERGE = 256
TN_OUT = 512
TQ_ATTN = 256
TR_SG = 256
TN_MOD = 512


def _params(sem):
    return pltpu.CompilerParams(dimension_semantics=sem, vmem_limit_bytes=VMEM_LIMIT)


def _mm(a, b):
    return jnp.dot(a.astype(BF16), b.astype(BF16), preferred_element_type=F32)


def _mm_nt(a, b):
    return lax.dot_general(a.astype(BF16), b.astype(BF16), (((1,), (1,)), ((), ())),
                           preferred_element_type=F32)


def _mm_tn(a, b):
    return lax.dot_general(a.astype(BF16), b.astype(BF16), (((0,), (0,)), ((), ())),
                           preferred_element_type=F32)


def _sigmoid(x):
    return 1.0 / (1.0 + jnp.exp(-x))


def _silu(x):
    return x * _sigmoid(x)


def _softplus(x):
    return jnp.maximum(x, 0.0) + jnp.log1p(jnp.exp(-jnp.abs(x)))


def _gelu_tanh(x):
    c = math.sqrt(2.0 / math.pi)
    return x * (0.5 * (1.0 + jnp.tanh(c * (x + 0.044715 * (x * x * x)))))


def _rms(x):
    return x * lax.rsqrt(jnp.mean(x * x, axis=-1, keepdims=True) + EPS)


def _norm_mod(x, gain, shift, scale):
    return (_rms(x) * gain) * (1.0 + scale) + shift


def _mod_kernel(cc_ref, w_ref, b_ref, o_ref):
    o_ref[...] = _mm(_silu(cc_ref[...]), w_ref[...]) + b_ref[...]


def _modulation(cc, mod_w, mod_b):
    depth = mod_w.shape[0]
    n = mod_w.shape[2]
    return pl.pallas_call(
        _mod_kernel,
        grid=(depth, n // TN_MOD),
        in_specs=[pl.BlockSpec((MOD_ROWS, D_MODEL), lambda l, j: (0, 0)),
                  pl.BlockSpec((None, D_MODEL, TN_MOD), lambda l, j: (l, 0, j)),
                  pl.BlockSpec((None, 1, TN_MOD), lambda l, j: (l, 0, j))],
        out_specs=pl.BlockSpec((None, MOD_ROWS, TN_MOD), lambda l, j: (l, 0, j)),
        out_shape=jax.ShapeDtypeStruct((depth, MOD_ROWS, n), F32),
        compiler_params=_params(("parallel", "arbitrary")),
        name="modulation",
    )(cc, mod_w, mod_b.reshape(depth, 1, n))


def _ffn_kernel(x_ref, mod_ref, g_ref, wa_ref, wb_ref, wo_ref, o_ref, h_scr, acc_scr, *, base):
    j = pl.program_id(2)

    @pl.when(j == 0)
    def _():
        h = _norm_mod(x_ref[...], g_ref[...], mod_ref[base:base + 1, :], mod_ref[base + 1:base + 2, :])
        h_scr[...] = h.astype(BF16)
        acc_scr[...] = jnp.zeros_like(acc_scr)

    h = h_scr[...]
    a = jnp.dot(h, wa_ref[...], preferred_element_type=F32)
    b = jnp.dot(h, wb_ref[...], preferred_element_type=F32)
    acc_scr[...] += _mm(_silu(a) * b, wo_ref[...])

    @pl.when(j == pl.num_programs(2) - 1)
    def _():
        o_ref[...] = x_ref[...] + (FFN_RESIDUAL * mod_ref[base + 2:base + 3, :]) * acc_scr[...]


def _ffn(x, mod, gain, w_in, w_out, base, mod_row):
    groups, rows, _ = x.shape
    nj = D_FF // TF_FFN
    mod_idx = (lambda b, i, j: (b, 0, 0)) if mod_row is None else (lambda b, i, j: (mod_row, 0, 0))
    return pl.pallas_call(
        functools.partial(_ffn_kernel, base=base),
        grid=(groups, rows // TM_FFN, nj),
        in_specs=[pl.BlockSpec((None, TM_FFN, D_MODEL), lambda b, i, j: (b, i, 0)),
                  pl.BlockSpec((None, N_MOD, D_MODEL), mod_idx),
                  pl.BlockSpec((1, D_MODEL), lambda b, i, j: (0, 0)),
                  pl.BlockSpec((D_MODEL, TF_FFN), lambda b, i, j: (0, j)),
                  pl.BlockSpec((D_MODEL, TF_FFN), lambda b, i, j: (0, j + nj)),
                  pl.BlockSpec((TF_FFN, D_MODEL), lambda b, i, j: (j, 0))],
        out_specs=pl.BlockSpec((None, TM_FFN, D_MODEL), lambda b, i, j: (b, i, 0)),
        out_shape=jax.ShapeDtypeStruct(x.shape, F32),
        scratch_shapes=[pltpu.VMEM((TM_FFN, D_MODEL), BF16), pltpu.VMEM((TM_FFN, D_MODEL), F32)],
        compiler_params=_params(("parallel", "parallel", "arbitrary")),
        name="ffn",
    )(x, mod, gain.reshape(1, D_MODEL), w_in, w_in, w_out)


def _inproj_kernel(x_ref, mod_ref, g_ref, w_ref, wgc_ref, wgr_ref, z_ref, gc_ref, gr_ref, h_scr):
    n = pl.program_id(2)

    @pl.when(n == 0)
    def _():
        h = _norm_mod(x_ref[...], g_ref[...], mod_ref[3:4, :], mod_ref[4:5, :]).astype(BF16)
        h_scr[...] = h
        gc_ref[...] = jnp.dot(h, wgc_ref[...], preferred_element_type=F32)
        gr_ref[...] = lax.dot_general(wgr_ref[...], h, (((1,), (1,)), ((), ())), preferred_element_type=F32)

    z_ref[...] = jnp.dot(h_scr[...], w_ref[...], preferred_element_type=F32)


def _inproj(x, mod, gain, w_main, w_gc, w_gr, n_cols, mod_row):
    groups, rows, _ = x.shape
    mod_idx = (lambda b, i, n: (b, 0, 0)) if mod_row is None else (lambda b, i, n: (mod_row, 0, 0))
    return pl.pallas_call(
        _inproj_kernel,
        grid=(groups, rows // TM_PROJ, n_cols // TN_PROJ),
        in_specs=[pl.BlockSpec((None, TM_PROJ, D_MODEL), lambda b, i, n: (b, i, 0)),
                  pl.BlockSpec((None, N_MOD, D_MODEL), mod_idx),
                  pl.BlockSpec((1, D_MODEL), lambda b, i, n: (0, 0)),
                  pl.BlockSpec((D_MODEL, TN_PROJ), lambda b, i, n: (0, n)),
                  pl.BlockSpec((D_MODEL, N_HEADS * LANES), lambda b, i, n: (0, 0)),
                  pl.BlockSpec((N_HEADS * N_GATE, D_MODEL), lambda b, i, n: (0, 0))],
        out_specs=[pl.BlockSpec((None, TM_PROJ, TN_PROJ), lambda b, i, n: (b, i, n)),
                   pl.BlockSpec((None, TM_PROJ, N_HEADS * LANES), lambda b, i, n: (b, i, 0)),
                   pl.BlockSpec((None, N_HEADS * N_GATE, TM_PROJ), lambda b, i, n: (b, 0, i))],
        out_shape=[jax.ShapeDtypeStruct((groups, rows, n_cols), F32),
                   jax.ShapeDtypeStruct((groups, rows, N_HEADS * LANES), F32),
                   jax.ShapeDtypeStruct((groups, N_HEADS * N_GATE, rows), F32)],
        scratch_shapes=[pltpu.VMEM((TM_PROJ, D_MODEL), BF16)],
        compiler_params=_params(("parallel", "parallel", "arbitrary")),
        name="inproj",
    )(x, mod, gain.reshape(1, D_MODEL), w_main, w_gc, w_gr)


def _split3(e):
    e1 = e.astype(BF16)
    r1 = e - e1.astype(F32)
    e2 = r1.astype(BF16)
    e3 = (r1 - e2.astype(F32)).astype(BF16)
    return e1, e2, e3


def _gate_values(pre, a_log, cls):
    neg_sp = -_softplus(-pre)
    return jnp.where(cls < 2, pre,
                     jnp.where(cls < 4, neg_sp,
                               jnp.where(cls < 6, _sigmoid(pre), -jnp.exp(a_log) * _softplus(pre))))


def _gateprep_kernel(gc_ref, gr_ref, bc_ref, ac_ref, br_ref, ar_ref, oc_ref, or_ref):
    L = ML_CHUNK
    half = GD_CHUNK
    pre = gc_ref[...] + bc_ref[...]
    cls = lax.broadcasted_iota(jnp.int32, pre.shape, 1) & (LANES - 1)
    t = lax.broadcasted_iota(jnp.int32, pre.shape, 0)
    e = _gate_values(pre, ac_ref[...], cls)
    tri = (lax.broadcasted_iota(jnp.int32, (L, L), 1) <= lax.broadcasted_iota(jnp.int32, (L, L), 0)).astype(BF16)
    e1, e2, e3 = _split3(e)
    p = (jnp.dot(tri, e1, preferred_element_type=F32) + jnp.dot(tri, e2, preferred_element_type=F32)
         + jnp.dot(tri, e3, preferred_element_type=F32))
    tot = p[L - 1:L, :]
    mid = p[half - 1:half, :]
    second = t >= half
    pre64 = p - jnp.where(second, mid, 0.0)
    suf64 = jnp.where(second, tot, mid) - p + e
    cum = jnp.where(cls == 2, p, jnp.where(cls == 3, tot - p + e, jnp.where(cls == 6, pre64, suf64)))
    oc_ref[...] = jnp.where((cls == 2) | (cls == 3) | (cls == 6) | (cls == 7), cum, e)

    pre_r = gr_ref[...] + br_ref[...]
    cls_r = lax.broadcasted_iota(jnp.int32, pre_r.shape, 0) & (N_GATE - 1)
    t_r = lax.broadcasted_iota(jnp.int32, pre_r.shape, 1)
    e_r = _gate_values(pre_r, ar_ref[...], cls_r)
    tri_r = (lax.broadcasted_iota(jnp.int32, (L, L), 0) <= lax.broadcasted_iota(jnp.int32, (L, L), 1)).astype(BF16)
    f1, f2, f3 = _split3(e_r)
    pr = (jnp.dot(f1, tri_r, preferred_element_type=F32) + jnp.dot(f2, tri_r, preferred_element_type=F32)
          + jnp.dot(f3, tri_r, preferred_element_type=F32))
    tot_r = pr[:, L - 1:L]
    mid_r = pr[:, half - 1:half]
    second_r = t_r >= half
    pre64_r = pr - jnp.where(second_r, mid_r, 0.0)
    suf64_r = jnp.where(second_r, tot_r, mid_r) - pr + e_r
    cum_r = jnp.where(cls_r == 2, pr, jnp.where(cls_r == 3, tot_r - pr + e_r, jnp.where(cls_r == 6, pre64_r, suf64_r)))
    or_ref[...] = jnp.where((cls_r == 2) | (cls_r == 3) | (cls_r == 6) | (cls_r == 7), cum_r, e_r)


def _gateprep(gc, gr, bias_c, alog_c, bias_r, alog_r):
    groups, rows, _ = gc.shape
    nt = rows // ML_CHUNK
    hw = N_HEADS * LANES
    hg = N_HEADS * N_GATE
    return pl.pallas_call(
        _gateprep_kernel,
        grid=(groups, nt),
        in_specs=[pl.BlockSpec((None, ML_CHUNK, hw), lambda b, c: (b, c, 0)),
                  pl.BlockSpec((None, hg, ML_CHUNK), lambda b, c: (b, 0, c)),
                  pl.BlockSpec((1, hw), lambda b, c: (0, 0)),
                  pl.BlockSpec((1, hw), lambda b, c: (0, 0)),
                  pl.BlockSpec((hg, 1), lambda b, c: (0, 0)),
                  pl.BlockSpec((hg, 1), lambda b, c: (0, 0))],
        out_specs=[pl.BlockSpec((None, ML_CHUNK, hw), lambda b, c: (b, c, 0)),
                   pl.BlockSpec((None, None, hg, ML_CHUNK), lambda b, c: (b, c, 0, 0))],
        out_shape=[jax.ShapeDtypeStruct((groups, rows, hw), F32),
                   jax.ShapeDtypeStruct((groups, nt, hg, ML_CHUNK), F32)],
        compiler_params=_params(("parallel", "parallel")),
        name="gateprep",
    )(gc, gr, bias_c, alog_c, bias_r, alog_r)


def _sg_kernel(u_ref, v_ref, gn_ref, w_ref, bias_ref, o_ref):
    gw = BRANCH_W // SG_GROUPS
    for ch in range(TR_SG // ML_CHUNK):
        rows = slice(ch * ML_CHUNK, (ch + 1) * ML_CHUNK)
        v = _rms(_gelu_tanh(v_ref[rows, :])) * gn_ref[...]
        u = _gelu_tanh(u_ref[rows, :])
        for g in range(SG_GROUPS):
            cols = slice(g * gw, (g + 1) * gw)
            s = _mm(w_ref[g], v[:, cols]) + bias_ref[:, cols]
            o_ref[rows, cols] = u[:, cols] * s


def _spatial_gating(z, gn, w_s, bias_full):
    groups, rows, _ = z.shape
    ub, vb = C_SG_U // BRANCH_W, C_SG_V // BRANCH_W
    return pl.pallas_call(
        _sg_kernel,
        grid=(groups, rows // TR_SG),
        in_specs=[pl.BlockSpec((None, TR_SG, BRANCH_W), lambda b, i: (b, i, ub)),
                  pl.BlockSpec((None, TR_SG, BRANCH_W), lambda b, i: (b, i, vb)),
                  pl.BlockSpec((1, BRANCH_W), lambda b, i: (0, 0)),
                  pl.BlockSpec((SG_GROUPS, ML_CHUNK, ML_CHUNK), lambda b, i: (0, 0, 0)),
                  pl.BlockSpec((ML_CHUNK, BRANCH_W), lambda b, i: (0, 0))],
        out_specs=pl.BlockSpec((None, TR_SG, BRANCH_W), lambda b, i: (b, i, 0)),
        out_shape=jax.ShapeDtypeStruct((groups, rows, BRANCH_W), F32),
        compiler_params=_params(("parallel", "parallel")),
        name="spatial_gating",
    )(z, z, gn, w_s, bias_full)


def _mlstm_chunk(q, k, v, gp, gpt, state, rev, with_out):
    L = ML_CHUNK
    j_i, j_b = (1, 3) if rev else (0, 2)
    li_c = gp[:, j_i:j_i + 1]
    b_c = gp[:, j_b:j_b + 1]
    b_end = b_c[0:1, :] if rev else b_c[L - 1:L, :]
    ks = k * (HEAD_DIM ** -0.5)
    a = b_end - b_c + li_c
    m_loc = jnp.max(a, axis=0, keepdims=True)
    wgt = jnp.exp(a - m_loc)
    ct_loc = _mm_tn(ks, v * wgt)
    n_loc = jnp.sum(ks * wgt, axis=0, keepdims=True)
    ct, n, m = state
    m_new = jnp.maximum(b_end + m, m_loc)
    sp = jnp.exp(b_end + m - m_new)
    sl = jnp.exp(m_loc - m_new)
    new_state = (sp * ct + sl * ct_loc, sp * n + sl * n_loc, m_new)
    if not with_out:
        return new_state, None
    li_r = gpt[j_i:j_i + 1, :]
    b_r = gpt[j_b:j_b + 1, :]
    t = lax.broadcasted_iota(jnp.int32, (L, L), 0)
    s = lax.broadcasted_iota(jnp.int32, (L, L), 1)
    incl = (s >= t) if rev else (s <= t)
    dm = jnp.where(incl, b_c - b_r + li_r, -jnp.inf)
    g = b_c + m
    m_t = jnp.maximum(jnp.max(dm, axis=1, keepdims=True), g)
    p = jnp.exp(dm - m_t) * _mm_nt(q, ks)
    inter = jnp.exp(g - m_t)
    num = _mm(p, v) + inter * _mm(q, ct)
    den = jnp.sum(p, axis=1, keepdims=True) + inter * jnp.sum(q * n, axis=1, keepdims=True)
    return new_state, num / jnp.maximum(jnp.abs(den), jnp.exp(-m_t))


def _mlstm_kernel(*refs, ctx_out, n_ctx, n_lat):
    if ctx_out:
        (kc_ref, vc_ref, gpc_ref, gptc_ref, qc_ref, oc_ref, kl_ref, vl_ref, gpl_ref, gptl_ref, ql_ref, ol_ref,
         norm_ref, yc_ref, yl_ref, hfc, hbc, hfl, hbl) = refs
    else:
        (kc_ref, vc_ref, gpc_ref, gptc_ref, kl_ref, vl_ref, gpl_ref, gptl_ref, ql_ref, ol_ref,
         norm_ref, yl_ref, hfl, hbl) = refs
        qc_ref = oc_ref = yc_ref = hfc = hbc = None
    L = ML_CHUNK

    def run(k_ref, v_ref, gp_ref, gpt_ref, q_ref, hf, hb, n_chunks, st_f, st_b, with_out):
        def body(j, carry):
            st_f, st_b = carry
            rf = pl.ds(pl.multiple_of(j * L, L), L)
            jb = n_chunks - 1 - j
            rb = pl.ds(pl.multiple_of(jb * L, L), L)
            st_f, h_f = _mlstm_chunk(q_ref[rf, :] if with_out else None, k_ref[rf, :], v_ref[rf, :],
                                     gp_ref[rf, :], gpt_ref[j], st_f, False, with_out)
            st_b, h_b = _mlstm_chunk(q_ref[rb, :] if with_out else None, k_ref[rb, :], v_ref[rb, :],
                                     gp_ref[rb, :], gpt_ref[jb], st_b, True, with_out)
            if with_out:
                hf[rf, :] = h_f
                hb[rb, :] = h_b
            return st_f, st_b
        return lax.fori_loop(0, n_chunks, body, (st_f, st_b))

    zero = (jnp.zeros((HEAD_DIM, HEAD_DIM), F32), jnp.zeros((1, HEAD_DIM), F32), jnp.zeros((1, 1), F32))
    st_f, st_b = run(kc_ref, vc_ref, gpc_ref, gptc_ref, qc_ref, hfc, hbc, n_ctx, zero, zero, ctx_out)
    run(kl_ref, vl_ref, gpl_ref, gptl_ref, ql_ref, hfl, hbl, n_lat, st_f, st_b, True)

    def finish(hf, hb, o_ref, y_ref):
        y_ref[...] = _sigmoid(o_ref[...]) * (_rms(hf[...] + hb[...]) * norm_ref[...])

    finish(hfl, hbl, ol_ref, yl_ref)
    if ctx_out:
        finish(hfc, hbc, oc_ref, yc_ref)


def _seq_spec(rows, col):
    return pl.BlockSpec((None, rows, HEAD_DIM), lambda b, h: (b, 0, col // HEAD_DIM + h))


def _gp_specs(rows):
    return [pl.BlockSpec((None, rows, LANES), lambda b, h: (b, 0, h)),
            pl.BlockSpec((None, rows // ML_CHUNK, N_GATE, ML_CHUNK), lambda b, h: (b, 0, h, 0))]


def _mlstm(zc, gpc, gptc, zl, gpl, gptl, norm, ctx_out):
    nb, tc, _ = zc.shape
    tl = zl.shape[1]
    ctx_in = [zc, zc, gpc, gptc] + ([zc, zc] if ctx_out else [])
    ctx_specs = [_seq_spec(tc, C_ML_K), _seq_spec(tc, C_ML_V)] + _gp_specs(tc) + (
        [_seq_spec(tc, C_ML_Q), _seq_spec(tc, C_ML_O)] if ctx_out else [])
    lat_in = [zl, zl, gpl, gptl, zl, zl]
    lat_specs = [_seq_spec(tl, C_ML_K), _seq_spec(tl, C_ML_V)] + _gp_specs(tl) + [
        _seq_spec(tl, C_ML_Q), _seq_spec(tl, C_ML_O)]
    out_spec = lambda rows: pl.BlockSpec((None, rows, HEAD_DIM), lambda b, h: (b, 0, h))
    out_shape = lambda rows: jax.ShapeDtypeStruct((nb, rows, BRANCH_W), F32)
    scr = lambda rows: [pltpu.VMEM((rows, HEAD_DIM), F32), pltpu.VMEM((rows, HEAD_DIM), F32)]
    res = pl.pallas_call(
        functools.partial(_mlstm_kernel, ctx_out=ctx_out, n_ctx=tc // ML_CHUNK, n_lat=tl // ML_CHUNK),
        grid=(nb, N_HEADS),
        in_specs=ctx_specs + lat_specs + [pl.BlockSpec((1, HEAD_DIM), lambda b, h: (0, h))],
        out_specs=([out_spec(tc)] if ctx_out else []) + [out_spec(tl)],
        out_shape=([out_shape(tc)] if ctx_out else []) + [out_shape(tl)],
        scratch_shapes=(scr(tc) if ctx_out else []) + scr(tl),
        compiler_params=_params(("parallel", "parallel")),
        name="mlstm",
    )(*ctx_in, *lat_in, norm)
    return (res[0], res[1]) if ctx_out else (None, res[0])


def _conv_silu(x, w3):
    rows = x.shape[0]
    t = lax.broadcasted_iota(jnp.int32, x.shape, 0)
    prev = jnp.where(t == 0, 0.0, pltpu.roll(x, 1, 0))
    nxt = jnp.where(t == rows - 1, 0.0, pltpu.roll(x, rows - 1, 0))
    return _silu(w3[0:1, :] * prev + w3[1:2, :] * x + w3[2:3, :] * nxt)


def _l2norm(x):
    return x * lax.rsqrt(jnp.sum(x * x, axis=-1, keepdims=True) + EPS)


def _gdn_chunk(q, k, v, gp, gpt, S, rev, with_out):
    L = GD_CHUNK
    j_b, j_g = (5, 7) if rev else (4, 6)
    beta = gp[:, j_b:j_b + 1]
    gc_c = gp[:, j_g:j_g + 1]
    gc_r = gpt[j_g:j_g + 1, :]
    g_tot = gc_c[0:1, :] if rev else gc_c[L - 1:L, :]
    t = lax.broadcasted_iota(jnp.int32, (L, L), 0)
    s = lax.broadcasted_iota(jnp.int32, (L, L), 1)
    incl = (s >= t) if rev else (s <= t)
    strict = (s > t) if rev else (s < t)
    decay = jnp.exp(jnp.where(incl, gc_c - gc_r, -jnp.inf))
    kb = k * beta
    m = jnp.where(strict, _mm_nt(kb, k) * decay, 0.0)
    eg = jnp.exp(gc_c)
    x = t ^ s
    tinv = jnp.where(x == 0, 1.0, 0.0) - jnp.where(x == 1, m, 0.0)
    for lvl in range(1, 6):
        c = jnp.where((x >> lvl) == 1, m, 0.0)
        tinv = tinv - _mm(_mm(tinv, c), tinv)
    y = _mm(tinv, jnp.concatenate([v * beta, kb * eg], axis=1))
    u = y[:, :HEAD_DIM]
    w = y[:, HEAD_DIM:]
    v_new = u - _mm(w, S)
    o = None
    if with_out:
        qs = q * (HEAD_DIM ** -0.5)
        o = _mm(qs * eg, S) + _mm(_mm_nt(qs, k) * decay, v_new)
    S = S * jnp.exp(g_tot) + _mm_tn(k * jnp.exp(g_tot - gc_c), v_new)
    return S, o


def _gdn_kernel(*refs, ctx_out, n_ctx, n_lat):
    if ctx_out:
        (kc_ref, vc_ref, gpc_ref, gptc_ref, qc_ref, zc_ref, kl_ref, vl_ref, gpl_ref, gptl_ref, ql_ref, zl_ref,
         wq_ref, wk_ref, wv_ref, norm_ref, yc_ref, yl_ref, kcs, vcs, kls, vls, qls, ofl, obl, qcs, ofc, obc) = refs
    else:
        (kc_ref, vc_ref, gpc_ref, gptc_ref, kl_ref, vl_ref, gpl_ref, gptl_ref, ql_ref, zl_ref,
         wq_ref, wk_ref, wv_ref, norm_ref, yl_ref, kcs, vcs, kls, vls, qls, ofl, obl) = refs
        qc_ref = zc_ref = yc_ref = qcs = ofc = obc = None
    L = GD_CHUNK
    P = 2 * L

    kcs[...] = _l2norm(_conv_silu(kc_ref[...], wk_ref[...]))
    vcs[...] = _conv_silu(vc_ref[...], wv_ref[...])
    kls[...] = _l2norm(_conv_silu(kl_ref[...], wk_ref[...]))
    vls[...] = _conv_silu(vl_ref[...], wv_ref[...])
    qls[...] = _l2norm(_conv_silu(ql_ref[...], wq_ref[...]))
    if ctx_out:
        qcs[...] = _l2norm(_conv_silu(qc_ref[...], wq_ref[...]))

    def run(ks, vs, gp_ref, gpt_ref, qs, of, ob, n_pairs, S_f, S_b, with_out):
        def body(j, carry):
            S_f, S_b = carry
            jb = n_pairs - 1 - j
            gpt_f = gpt_ref[j]
            gpt_b = gpt_ref[jb]
            for c in range(2):
                rf = pl.ds(pl.multiple_of(j * P + c * L, L), L)
                S_f, o_f = _gdn_chunk(qs[rf, :] if with_out else None, ks[rf, :], vs[rf, :], gp_ref[rf, :],
                                      gpt_f[:, c * L:(c + 1) * L], S_f, False, with_out)
                cb = 1 - c
                rb = pl.ds(pl.multiple_of(jb * P + cb * L, L), L)
                S_b, o_b = _gdn_chunk(qs[rb, :] if with_out else None, ks[rb, :], vs[rb, :], gp_ref[rb, :],
                                      gpt_b[:, cb * L:(cb + 1) * L], S_b, True, with_out)
                if with_out:
                    of[rf, :] = o_f
                    ob[rb, :] = o_b
            return S_f, S_b
        return lax.fori_loop(0, n_pairs, body, (S_f, S_b))

    zero = jnp.zeros((HEAD_DIM, HEAD_DIM), F32)
    S_f, S_b = run(kcs, vcs, gpc_ref, gptc_ref, qcs, ofc, obc, n_ctx, zero, zero, ctx_out)
    run(kls, vls, gpl_ref, gptl_ref, qls, ofl, obl, n_lat, S_f, S_b, True)

    def finish(of, ob, z_ref, y_ref):
        y_ref[...] = (_rms(of[...] + ob[...]) * norm_ref[...]) * _silu(z_ref[...])

    finish(ofl, obl, zl_ref, yl_ref)
    if ctx_out:
        finish(ofc, obc, zc_ref, yc_ref)


def _gdn(zc, gpc, gptc, zl, gpl, gptl, conv, norm, ctx_out):
    nb, tc, _ = zc.shape
    tl = zl.shape[1]
    ctx_in = [zc, zc, gpc, gptc] + ([zc, zc] if ctx_out else [])
    ctx_specs = [_seq_spec(tc, C_GD_K), _seq_spec(tc, C_GD_V)] + _gp_specs(tc) + (
        [_seq_spec(tc, C_GD_Q), _seq_spec(tc, C_GD_Z)] if ctx_out else [])
    lat_in = [zl, zl, gpl, gptl, zl, zl]
    lat_specs = [_seq_spec(tl, C_GD_K), _seq_spec(tl, C_GD_V)] + _gp_specs(tl) + [
        _seq_spec(tl, C_GD_Q), _seq_spec(tl, C_GD_Z)]
    conv_spec = lambda off: pl.BlockSpec((3, HEAD_DIM), lambda b, h: (0, off // HEAD_DIM + h))
    out_spec = lambda rows: pl.BlockSpec((None, rows, HEAD_DIM), lambda b, h: (b, 0, h))
    out_shape = lambda rows: jax.ShapeDtypeStruct((nb, rows, BRANCH_W), F32)
    seq = lambda rows: pltpu.VMEM((rows, HEAD_DIM), F32)
    scratch = [seq(tc), seq(tc), seq(tl), seq(tl), seq(tl), seq(tl), seq(tl)] + (
        [seq(tc), seq(tc), seq(tc)] if ctx_out else [])
    res = pl.pallas_call(
        functools.partial(_gdn_kernel, ctx_out=ctx_out, n_ctx=tc // (2 * GD_CHUNK), n_lat=tl // (2 * GD_CHUNK)),
        grid=(nb, N_HEADS),
        in_specs=ctx_specs + lat_specs + [conv_spec(0), conv_spec(BRANCH_W), conv_spec(2 * BRANCH_W),
                                          pl.BlockSpec((1, HEAD_DIM), lambda b, h: (0, 0))],
        out_specs=([out_spec(tc)] if ctx_out else []) + [out_spec(tl)],
        out_shape=([out_shape(tc)] if ctx_out else []) + [out_shape(tl)],
        scratch_shapes=scratch,
        compiler_params=_params(("parallel", "parallel")),
        name="gdn",
    )(*ctx_in, *lat_in, conv, conv, conv, norm)
    return (res[0], res[1]) if ctx_out else (None, res[0])


def _rope(x, cos, sin):
    lane = lax.broadcasted_iota(jnp.int32, x.shape, 1)
    quarter = HEAD_DIM // 4
    partner = jnp.where((lane & (2 * quarter - 1)) < quarter,
                        pltpu.roll(x, HEAD_DIM - quarter, 1), pltpu.roll(x, quarter, 1))
    return x * cos + partner * sin


def _attn_kernel(*refs, latent, tc):
    if latent:
        (q_ref, kc_ref, vc_ref, kl_ref, vl_ref, cos_ref, sin_ref, cosq_ref, sinq_ref, qn_ref, kn_ref,
         o_ref, k_scr, v_scr) = refs
    else:
        q_ref, kc_ref, vc_ref, qn_ref, kn_ref, o_ref, k_scr, v_scr = refs

    @pl.when(pl.program_id(2) == 0)
    def _():
        k_scr[0:tc, :] = (_rms(kc_ref[...]) * kn_ref[...]).astype(BF16)
        v_scr[0:tc, :] = vc_ref[...].astype(BF16)
        if latent:
            kl = _rope(_rms(kl_ref[...]) * kn_ref[...], cos_ref[...], sin_ref[...])
            k_scr[tc:, :] = kl.astype(BF16)
            v_scr[tc:, :] = vl_ref[...].astype(BF16)

    scale = HEAD_DIM ** -0.5
    for g in range(AT_Q_HEADS // AT_KV_HEADS):
        cols = slice(g * HEAD_DIM, (g + 1) * HEAD_DIM)
        q = _rms(q_ref[:, cols]) * qn_ref[...]
        if latent:
            q = _rope(q, cosq_ref[...], sinq_ref[...])
        s = lax.dot_general(q.astype(BF16), k_scr[...], (((1,), (1,)), ((), ())),
                            preferred_element_type=F32) * scale
        p = jnp.exp(s - jnp.max(s, axis=-1, keepdims=True))
        l = jnp.sum(p, axis=-1, keepdims=True)
        o_ref[:, cols] = jnp.dot(p.astype(BF16), v_scr[...], preferred_element_type=F32) / l


def _attention(zq, zc, zl, cos, sin, qn, kn):
    latent = zl is not None
    nb, tq_total, _ = zq.shape
    tc = zc.shape[1]
    tk = tc + (zl.shape[1] if latent else 0)
    gq = (AT_Q_HEADS // AT_KV_HEADS) * HEAD_DIM
    head_spec = lambda rows, col: pl.BlockSpec((None, rows, HEAD_DIM), lambda b, h, i: (b, 0, col // HEAD_DIM + h))
    vec_spec = pl.BlockSpec((1, HEAD_DIM), lambda b, h, i: (0, 0))
    in_specs = [pl.BlockSpec((None, TQ_ATTN, gq), lambda b, h, i: (b, i, C_AT_Q // gq + h)),
                head_spec(tc, C_AT_K), head_spec(tc, C_AT_V)]
    args = [zq, zc, zc]
    if latent:
        tl = zl.shape[1]
        in_specs += [head_spec(tl, C_AT_K), head_spec(tl, C_AT_V),
                     pl.BlockSpec((tl, HEAD_DIM), lambda b, h, i: (0, 0)),
                     pl.BlockSpec((tl, HEAD_DIM), lambda b, h, i: (0, 0)),
                     pl.BlockSpec((TQ_ATTN, HEAD_DIM), lambda b, h, i: (i, 0)),
                     pl.BlockSpec((TQ_ATTN, HEAD_DIM), lambda b, h, i: (i, 0))]
        args += [zl, zl, cos, sin, cos, sin]
    return pl.pallas_call(
        functools.partial(_attn_kernel, latent=latent, tc=tc),
        grid=(nb, AT_KV_HEADS, tq_total // TQ_ATTN),
        in_specs=in_specs + [vec_spec, vec_spec],
        out_specs=pl.BlockSpec((None, TQ_ATTN, gq), lambda b, h, i: (b, i, h)),
        out_shape=jax.ShapeDtypeStruct((nb, tq_total, BRANCH_W), F32),
        scratch_shapes=[pltpu.VMEM((tk, HEAD_DIM), BF16), pltpu.VMEM((tk, HEAD_DIM), BF16)],
        compiler_params=_params(("parallel", "parallel", "arbitrary")),
        name="attention",
    )(*args, qn, kn)


def _merge_kernel(x_ref, mod_ref, g_ref, xcol_ref, modcol_ref, y0_ref, y1_ref, y2_ref, y3_ref,
                  wg0_ref, wg1_ref, wg2_ref, wg3_ref, wb_ref, wo_ref, o_ref, h_scr, m_scr):
    s = pl.program_id(2)
    n_gate = D_MODEL // TN_MERGE
    ys = (y0_ref, y1_ref, y2_ref, y3_ref)
    wgs = (wg0_ref, wg1_ref, wg2_ref, wg3_ref)

    @pl.when(s == 0)
    def _():
        h_scr[...] = _norm_mod(x_ref[...], g_ref[...], mod_ref[3:4, :], mod_ref[4:5, :]).astype(BF16)

    @pl.when(s < n_gate)
    def _():
        h = h_scr[...]
        acc = jnp.zeros((TM_MERGE, TN_MERGE), F32)
        for n in range(N_BRANCH):
            gate = jnp.dot(h, wgs[n][...], preferred_element_type=F32)
            acc = acc + _sigmoid(gate) * _mm(ys[n][...], wb_ref[n])
        m_scr[s] = acc.astype(BF16)

    @pl.when(s >= n_gate)
    def _():
        acc = jnp.zeros((TM_MERGE, TN_OUT), F32)
        for kk in range(n_gate):
            acc = acc + jnp.dot(m_scr[kk], wo_ref[kk * TN_MERGE:(kk + 1) * TN_MERGE, :],
                                preferred_element_type=F32)
        o_ref[...] = xcol_ref[...] + modcol_ref[5:6, :] * acc


def _merge(x, mod, gain, ys, w_gate, w_branch, w_out, mod_row):
    groups, rows, _ = x.shape
    n_gate = D_MODEL // TN_MERGE
    n_out = D_MODEL // TN_OUT
    mrow = (lambda b: b) if mod_row is None else (lambda b: mod_row)
    gate_col = lambda s: jnp.minimum(s, n_gate - 1)
    out_col = lambda s: jnp.maximum(s - n_gate, 0)
    y_spec = pl.BlockSpec((None, TM_MERGE, BRANCH_W), lambda b, i, s: (b, i, 0))
    wg_spec = lambda n: pl.BlockSpec((D_MODEL, TN_MERGE), lambda b, i, s: (0, n * n_gate + gate_col(s)))
    return pl.pallas_call(
        _merge_kernel,
        grid=(groups, rows // TM_MERGE, n_gate + n_out),
        in_specs=[pl.BlockSpec((None, TM_MERGE, D_MODEL), lambda b, i, s: (b, i, 0)),
                  pl.BlockSpec((None, N_MOD, D_MODEL), lambda b, i, s: (mrow(b), 0, 0)),
                  pl.BlockSpec((1, D_MODEL), lambda b, i, s: (0, 0)),
                  pl.BlockSpec((None, TM_MERGE, TN_OUT), lambda b, i, s: (b, i, out_col(s))),
                  pl.BlockSpec((None, N_MOD, TN_OUT), lambda b, i, s: (mrow(b), 0, out_col(s))),
                  y_spec, y_spec, y_spec, y_spec,
                  wg_spec(0), wg_spec(1), wg_spec(2), wg_spec(3),
                  pl.BlockSpec((N_BRANCH, BRANCH_W, TN_MERGE), lambda b, i, s: (0, 0, gate_col(s))),
                  pl.BlockSpec((D_MODEL, TN_OUT), lambda b, i, s: (0, out_col(s)))],
        out_specs=pl.BlockSpec((None, TM_MERGE, TN_OUT), lambda b, i, s: (b, i, out_col(s))),
        out_shape=jax.ShapeDtypeStruct(x.shape, F32),
        scratch_shapes=[pltpu.VMEM((TM_MERGE, D_MODEL), BF16), pltpu.VMEM((n_gate, TM_MERGE, TN_MERGE), BF16)],
        compiler_params=_params(("parallel", "parallel", "arbitrary")),
        name="merge",
    )(x, mod, gain.reshape(1, D_MODEL), x, mod, *ys, w_gate, w_gate, w_gate, w_gate, w_branch, w_out)


def _rope_tables(seq):
    n = HEAD_DIM // 4
    inv = ROPE_THETA ** (-jnp.arange(n, dtype=F32) / n)
    pos = jnp.arange(seq)
    ar = (pos // GRID_W).astype(F32)[:, None] * inv
    ac = (pos % GRID_W).astype(F32)[:, None] * inv
    cos = jnp.concatenate([jnp.cos(ar), jnp.cos(ar), jnp.cos(ac), jnp.cos(ac)], axis=1)
    sin = jnp.concatenate([-jnp.sin(ar), jnp.sin(ar), -jnp.sin(ac), jnp.sin(ac)], axis=1)
    return cos, sin


def _gate_params(ml_if_bias, gd_a_log, gd_dt_bias):
    zeros = jnp.zeros((2, N_HEADS), F32)
    bias = jnp.concatenate([ml_if_bias.reshape(4, N_HEADS), zeros, gd_dt_bias], axis=0).T
    alog = jnp.concatenate([jnp.zeros((6, N_HEADS), F32), gd_a_log], axis=0).T
    pad = lambda a: jnp.pad(a, ((0, 0), (0, LANES - N_GATE))).reshape(1, N_HEADS * LANES)
    col = lambda a: a.reshape(N_HEADS * N_GATE, 1)
    return pad(bias), pad(alog), col(bias), col(alog)


def _gate_weights(w_in):
    g = jnp.concatenate([w_in[:, R_ML_IF:R_ML_IF + 16], w_in[:, R_GD_BA:R_GD_BA + 16]], axis=1)
    g = g.reshape(D_MODEL, N_GATE, N_HEADS).transpose(0, 2, 1)
    w_gc = jnp.pad(g, ((0, 0), (0, 0), (0, LANES - N_GATE))).reshape(D_MODEL, N_HEADS * LANES)
    w_gr = g.reshape(D_MODEL, N_HEADS * N_GATE).T
    return w_gc.astype(BF16), w_gr.astype(BF16)


def kernel(x, c, ctx, c_ctx, mod_w, mod_b, ffn1_norm, ffn1_w_in, ffn1_w_out, mix_norm, w_in, sg_norm, sg_w, sg_b, ml_if_bias, ml_norm, gd_conv, gd_a_log, gd_dt_bias, gd_norm, at_q_norm, at_k_norm, w_branch, w_out, ffn2_norm, ffn2_w_in, ffn2_w_out):
    nb, seq, _ = x.shape
    tc = ctx.shape[1]
    depth = mod_w.shape[0]
    ctx_row = nb
    cos, sin = _rope_tables(seq)

    cc = jnp.zeros((MOD_ROWS, D_MODEL), F32).at[:nb].set(c).at[ctx_row].set(c_ctx)
    mods = _modulation(cc, mod_w, mod_b).reshape(depth, MOD_ROWS, N_MOD, D_MODEL)

    flat = lambda a: a.reshape(1, nb * tc, a.shape[-1])
    unflat = lambda a: a.reshape(nb, tc, a.shape[-1])
    xc = ctx
    for l in range(depth):
        last = l == depth - 1
        ctx_out = not last
        mod = mods[l]
        wl = w_in[l]
        w_main = jnp.concatenate([wl[:, :R_ML_IF], wl[:, R_GD_KV:R_GD_BA], wl[:, R_AT_KV:R_GATE]], axis=1).astype(BF16)
        w_gate = wl[:, R_GATE:].astype(BF16)
        w_gc, w_gr = _gate_weights(wl)
        bias_c, alog_c, bias_r, alog_r = _gate_params(ml_if_bias[l], gd_a_log[l], gd_dt_bias[l])
        f1_in, f1_out = ffn1_w_in[l].astype(BF16), ffn1_w_out[l].astype(BF16)
        f2_in, f2_out = ffn2_w_in[l].astype(BF16), ffn2_w_out[l].astype(BF16)
        wb, wo = w_branch[l].astype(BF16), w_out[l].astype(BF16)
        sgw = sg_w[l].astype(BF16)
        sg_bias = jnp.repeat(sg_b[l].T, BRANCH_W // SG_GROUPS, axis=1)
        sgn = sg_norm[l].reshape(1, BRANCH_W)
        mln = ml_norm[l].reshape(1, BRANCH_W)
        gdn_g = gd_norm[l].reshape(1, HEAD_DIM)
        qn, kn = at_q_norm[l].reshape(1, HEAD_DIM), at_k_norm[l].reshape(1, HEAD_DIM)

        x = _ffn(x, mod, ffn1_norm[l], f1_in, f1_out, 0, None)
        xc = unflat(_ffn(flat(xc), mod, ffn1_norm[l], f1_in, f1_out, 0, ctx_row))

        zl, gcl, grl = _inproj(x, mod, mix_norm[l], w_main, w_gc, w_gr, N_MAIN, None)
        zc, gcc, grc = _inproj(flat(xc), mod, mix_norm[l], w_main, w_gc, w_gr, N_MAIN if ctx_out else KV_MAIN, ctx_row)
        zc = unflat(zc)
        gcc = unflat(gcc)
        grc = grc.reshape(N_HEADS * N_GATE, nb, tc).transpose(1, 0, 2)
        gpl, gptl = _gateprep(gcl, grl, bias_c, alog_c, bias_r, alog_r)
        gpc, gptc = _gateprep(gcc, grc, bias_c, alog_c, bias_r, alog_r)

        y_sg_l = _spatial_gating(zl, sgn, sgw, sg_bias)
        y_ml_c, y_ml_l = _mlstm(zc, gpc, gptc, zl, gpl, gptl, mln, ctx_out)
        y_gd_c, y_gd_l = _gdn(zc, gpc, gptc, zl, gpl, gptl, gd_conv[l], gdn_g, ctx_out)
        y_at_l = _attention(zl, zc, zl, cos, sin, qn, kn)
        x = _merge(x, mod, mix_norm[l], [y_sg_l, y_ml_l, y_gd_l, y_at_l], w_gate, wb, wo, None)
        if ctx_out:
            y_sg_c = _spatial_gating(zc, sgn, sgw, sg_bias)
            y_at_c = _attention(zc, zc, None, None, None, qn, kn)
            xc = unflat(_merge(flat(xc), mod, mix_norm[l], [flat(y) for y in (y_sg_c, y_ml_c, y_gd_c, y_at_c)],
                               w_gate, wb, wo, ctx_row))

        x = _ffn(x, mod, ffn2_norm[l], f2_in, f2_out, 6, None)
        if ctx_out:
            xc = unflat(_ffn(flat(xc), mod, ffn2_norm[l], f2_in, f2_out, 6, ctx_row))
    return x
```

```python
import functools
import math

import jax
import jax.numpy as jnp
from jax import lax
from jax.experimental import pallas as pl
from jax.experimental.pallas import tpu as pltpu

F32 = jnp.float32
BF16 = jnp.bfloat16
EPS = 1e-6

D_MODEL = 2048
D_FF = 5632
N_MOD = 9
N_BRANCH = 4
BRANCH_W = 512
HEAD_DIM = 128
N_HEADS = 4
SG_GROUPS = 4
ML_CHUNK = 128
GD_CHUNK = 64
GD_SUPER = 256
AT_Q_HEADS = 4
AT_KV_HEADS = 2
GRID_W = 64
ROPE_THETA = 10000.0
FFN_RESIDUAL = 0.5

LANES = 128
N_GATE = 8
MOD_ROWS = 16

C_ML_K, C_ML_V = 0, 512
C_GD_K, C_GD_V = 1024, 1536
C_AT_K, C_AT_V = 2048, 2304
KV_MAIN = 2560
C_SG_U, C_SG_V = 2560, 3072
C_ML_Q, C_ML_O = 3584, 4096
C_GD_Q, C_GD_Z = 4608, 5120
C_AT_Q = 5632
N_MAIN = 6144
R_ML_IF = 1024
R_GD_KV = 1040
R_GD_BA = 2064
R_AT_KV = 2080
R_REST = 2592
R_GATE = 6176

VMEM_LIMIT = 56 * 1024 * 1024

TM_FFN = 512
TF_FFN = 512
TM_PROJ = 1024
TN_PROJ = 512
TM_MERGE = 512
TN_MERGE = 256
TN_OUT = 512
TQ_ATTN = 256
TR_SG = 256
TN_MOD = 2048


def _params(sem):
    return pltpu.CompilerParams(dimension_semantics=sem, vmem_limit_bytes=VMEM_LIMIT)


def _mm(a, b):
    return jnp.dot(a.astype(BF16), b.astype(BF16), preferred_element_type=F32)


def _mm_nt(a, b):
    return lax.dot_general(a.astype(BF16), b.astype(BF16), (((1,), (1,)), ((), ())),
                           preferred_element_type=F32)


def _mm_tn(a, b):
    return lax.dot_general(a.astype(BF16), b.astype(BF16), (((0,), (0,)), ((), ())),
                           preferred_element_type=F32)


def _sigmoid(x):
    return 1.0 / (1.0 + jnp.exp(-x))


def _silu(x):
    return x * _sigmoid(x)


def _softplus(x):
    return jnp.maximum(x, 0.0) + jnp.log1p(jnp.exp(-jnp.abs(x)))


def _gelu_tanh(x):
    c = math.sqrt(2.0 / math.pi)
    return x * (0.5 * (1.0 + jnp.tanh(c * (x + 0.044715 * (x * x * x)))))


def _rms(x):
    return x * lax.rsqrt(jnp.mean(x * x, axis=-1, keepdims=True) + EPS)


def _norm_mod(x, gain, shift, scale):
    return (_rms(x) * gain) * (1.0 + scale) + shift


def _mod_kernel(cc_ref, w_ref, b_ref, o_ref):
    o_ref[...] = _mm(_silu(cc_ref[...]), w_ref[...]) + b_ref[...]


def _modulation(cc, mod_w, mod_b):
    depth = mod_w.shape[0]
    n = mod_w.shape[2]
    return pl.pallas_call(
        _mod_kernel,
        grid=(depth, n // TN_MOD),
        in_specs=[pl.BlockSpec((MOD_ROWS, D_MODEL), lambda l, j: (0, 0)),
                  pl.BlockSpec((None, D_MODEL, TN_MOD), lambda l, j: (l, 0, j)),
                  pl.BlockSpec((None, 1, TN_MOD), lambda l, j: (l, 0, j))],
        out_specs=pl.BlockSpec((None, MOD_ROWS, TN_MOD), lambda l, j: (l, 0, j)),
        out_shape=jax.ShapeDtypeStruct((depth, MOD_ROWS, n), F32),
        compiler_params=_params(("parallel", "arbitrary")),
        name="modulation",
    )(cc, mod_w, mod_b.reshape(depth, 1, n))


def _ffn_kernel(x_ref, mod_ref, g_ref, wa_ref, wb_ref, wo_ref, o_ref, h_scr, acc_scr, *, base):
    j = pl.program_id(2)

    @pl.when(j == 0)
    def _():
        h = _norm_mod(x_ref[...], g_ref[...], mod_ref[base:base + 1, :], mod_ref[base + 1:base + 2, :])
        h_scr[...] = h.astype(BF16)
        acc_scr[...] = jnp.zeros_like(acc_scr)

    h = h_scr[...]
    a = jnp.dot(h, wa_ref[...], preferred_element_type=F32)
    b = jnp.dot(h, wb_ref[...], preferred_element_type=F32)
    acc_scr[...] += _mm(_silu(a) * b, wo_ref[...])

    @pl.when(j == pl.num_programs(2) - 1)
    def _():
        o_ref[...] = x_ref[...] + (FFN_RESIDUAL * mod_ref[base + 2:base + 3, :]) * acc_scr[...]


def _ffn(x, mod, gain, w_in, w_out, base, mod_row):
    groups, rows, _ = x.shape
    nj = D_FF // TF_FFN
    mod_idx = (lambda b, i, j: (b, 0, 0)) if mod_row is None else (lambda b, i, j: (mod_row, 0, 0))
    return pl.pallas_call(
        functools.partial(_ffn_kernel, base=base),
        grid=(groups, rows // TM_FFN, nj),
        in_specs=[pl.BlockSpec((None, TM_FFN, D_MODEL), lambda b, i, j: (b, i, 0)),
                  pl.BlockSpec((None, N_MOD, D_MODEL), mod_idx),
                  pl.BlockSpec((1, D_MODEL), lambda b, i, j: (0, 0)),
                  pl.BlockSpec((D_MODEL, TF_FFN), lambda b, i, j: (0, j)),
                  pl.BlockSpec((D_MODEL, TF_FFN), lambda b, i, j: (0, j + nj)),
                  pl.BlockSpec((TF_FFN, D_MODEL), lambda b, i, j: (j, 0))],
        out_specs=pl.BlockSpec((None, TM_FFN, D_MODEL), lambda b, i, j: (b, i, 0)),
        out_shape=jax.ShapeDtypeStruct(x.shape, F32),
        scratch_shapes=[pltpu.VMEM((TM_FFN, D_MODEL), BF16), pltpu.VMEM((TM_FFN, D_MODEL), F32)],
        compiler_params=_params(("parallel", "parallel", "arbitrary")),
        name="ffn",
    )(x, mod, gain.reshape(1, D_MODEL), w_in, w_in, w_out)


def _inproj_kernel(x_ref, mod_ref, g_ref, w_ref, wgc_ref, wgr_ref, z_ref, gc_ref, gr_ref, h_scr):
    n = pl.program_id(2)

    @pl.when(n == 0)
    def _():
        h = _norm_mod(x_ref[...], g_ref[...], mod_ref[3:4, :], mod_ref[4:5, :]).astype(BF16)
        h_scr[...] = h
        gc_ref[...] = jnp.dot(h, wgc_ref[...], preferred_element_type=F32)
        gr_ref[...] = lax.dot_general(wgr_ref[...], h, (((1,), (1,)), ((), ())), preferred_element_type=F32)

    z_ref[...] = jnp.dot(h_scr[...], w_ref[...], preferred_element_type=F32)


def _inproj(x, mod, gain, w_main, w_gc, w_gr, n_cols, mod_row):
    groups, rows, _ = x.shape
    mod_idx = (lambda b, i, n: (b, 0, 0)) if mod_row is None else (lambda b, i, n: (mod_row, 0, 0))
    return pl.pallas_call(
        _inproj_kernel,
        grid=(groups, rows // TM_PROJ, n_cols // TN_PROJ),
        in_specs=[pl.BlockSpec((None, TM_PROJ, D_MODEL), lambda b, i, n: (b, i, 0)),
                  pl.BlockSpec((None, N_MOD, D_MODEL), mod_idx),
                  pl.BlockSpec((1, D_MODEL), lambda b, i, n: (0, 0)),
                  pl.BlockSpec((D_MODEL, TN_PROJ), lambda b, i, n: (0, n)),
                  pl.BlockSpec((D_MODEL, N_HEADS * LANES), lambda b, i, n: (0, 0)),
                  pl.BlockSpec((N_HEADS * N_GATE, D_MODEL), lambda b, i, n: (0, 0))],
        out_specs=[pl.BlockSpec((None, TM_PROJ, TN_PROJ), lambda b, i, n: (b, i, n)),
                   pl.BlockSpec((None, TM_PROJ, N_HEADS * LANES), lambda b, i, n: (b, i, 0)),
                   pl.BlockSpec((None, N_HEADS * N_GATE, TM_PROJ), lambda b, i, n: (b, 0, i))],
        out_shape=[jax.ShapeDtypeStruct((groups, rows, n_cols), F32),
                   jax.ShapeDtypeStruct((groups, rows, N_HEADS * LANES), F32),
                   jax.ShapeDtypeStruct((groups, N_HEADS * N_GATE, rows), F32)],
        scratch_shapes=[pltpu.VMEM((TM_PROJ, D_MODEL), BF16)],
        compiler_params=_params(("parallel", "parallel", "arbitrary")),
        name="inproj",
    )(x, mod, gain.reshape(1, D_MODEL), w_main, w_gc, w_gr)


def _split3(e):
    e1 = e.astype(BF16)
    r1 = e - e1.astype(F32)
    e2 = r1.astype(BF16)
    e3 = (r1 - e2.astype(F32)).astype(BF16)
    return e1, e2, e3


def _gate_values(pre, a_log, cls):
    neg_sp = -_softplus(-pre)
    return jnp.where(cls < 2, pre,
                     jnp.where(cls < 4, neg_sp,
                               jnp.where(cls < 6, _sigmoid(pre), -jnp.exp(a_log) * _softplus(pre))))


def _gateprep_kernel(gc_ref, gr_ref, bc_ref, ac_ref, br_ref, ar_ref, oc_ref, or_ref):
    L = ML_CHUNK
    half = GD_CHUNK
    pre = gc_ref[...] + bc_ref[...]
    cls = lax.broadcasted_iota(jnp.int32, pre.shape, 1) & (LANES - 1)
    t = lax.broadcasted_iota(jnp.int32, pre.shape, 0)
    e = _gate_values(pre, ac_ref[...], cls)
    tri = (lax.broadcasted_iota(jnp.int32, (L, L), 1) <= lax.broadcasted_iota(jnp.int32, (L, L), 0)).astype(BF16)
    e1, e2, e3 = _split3(e)
    p = (jnp.dot(tri, e1, preferred_element_type=F32) + jnp.dot(tri, e2, preferred_element_type=F32)
         + jnp.dot(tri, e3, preferred_element_type=F32))
    tot = p[L - 1:L, :]
    mid = p[half - 1:half, :]
    second = t >= half
    pre64 = p - jnp.where(second, mid, 0.0)
    suf64 = jnp.where(second, tot, mid) - p + e
    cum = jnp.where(cls == 2, p, jnp.where(cls == 3, tot - p + e, jnp.where(cls == 6, pre64, suf64)))
    oc_ref[...] = jnp.where((cls == 2) | (cls == 3) | (cls == 6) | (cls == 7), cum, e)

    pre_r = gr_ref[...] + br_ref[...]
    cls_r = lax.broadcasted_iota(jnp.int32, pre_r.shape, 0) & (N_GATE - 1)
    t_r = lax.broadcasted_iota(jnp.int32, pre_r.shape, 1)
    e_r = _gate_values(pre_r, ar_ref[...], cls_r)
    tri_r = (lax.broadcasted_iota(jnp.int32, (L, L), 0) <= lax.broadcasted_iota(jnp.int32, (L, L), 1)).astype(BF16)
    f1, f2, f3 = _split3(e_r)
    pr = (jnp.dot(f1, tri_r, preferred_element_type=F32) + jnp.dot(f2, tri_r, preferred_element_type=F32)
          + jnp.dot(f3, tri_r, preferred_element_type=F32))
    tot_r = pr[:, L - 1:L]
    mid_r = pr[:, half - 1:half]
    second_r = t_r >= half
    pre64_r = pr - jnp.where(second_r, mid_r, 0.0)
    suf64_r = jnp.where(second_r, tot_r, mid_r) - pr + e_r
    cum_r = jnp.where(cls_r == 2, pr, jnp.where(cls_r == 3, tot_r - pr + e_r, jnp.where(cls_r == 6, pre64_r, suf64_r)))
    or_ref[...] = jnp.where((cls_r == 2) | (cls_r == 3) | (cls_r == 6) | (cls_r == 7), cum_r, e_r)


def _gateprep(gc, gr, bias_c, alog_c, bias_r, alog_r):
    groups, rows, _ = gc.shape
    nt = rows // ML_CHUNK
    hw = N_HEADS * LANES
    hg = N_HEADS * N_GATE
    return pl.pallas_call(
        _gateprep_kernel,
        grid=(groups, nt),
        in_specs=[pl.BlockSpec((None, ML_CHUNK, hw), lambda b, c: (b, c, 0)),
                  pl.BlockSpec((None, hg, ML_CHUNK), lambda b, c: (b, 0, c)),
                  pl.BlockSpec((1, hw), lambda b, c: (0, 0)),
                  pl.BlockSpec((1, hw), lambda b, c: (0, 0)),
                  pl.BlockSpec((hg, 1), lambda b, c: (0, 0)),
                  pl.BlockSpec((hg, 1), lambda b, c: (0, 0))],
        out_specs=[pl.BlockSpec((None, ML_CHUNK, hw), lambda b, c: (b, c, 0)),
                   pl.BlockSpec((None, None, hg, ML_CHUNK), lambda b, c: (b, c, 0, 0))],
        out_shape=[jax.ShapeDtypeStruct((groups, rows, hw), F32),
                   jax.ShapeDtypeStruct((groups, nt, hg, ML_CHUNK), F32)],
        compiler_params=_params(("parallel", "parallel")),
        name="gateprep",
    )(gc, gr, bias_c, alog_c, bias_r, alog_r)


def _sg_kernel(u_ref, v_ref, gn_ref, w_ref, bias_ref, o_ref):
    gw = BRANCH_W // SG_GROUPS
    for ch in range(TR_SG // ML_CHUNK):
        rows = slice(ch * ML_CHUNK, (ch + 1) * ML_CHUNK)
        v = _rms(_gelu_tanh(v_ref[rows, :])) * gn_ref[...]
        u = _gelu_tanh(u_ref[rows, :])
        for g in range(SG_GROUPS):
            cols = slice(g * gw, (g + 1) * gw)
            s = _mm(w_ref[g], v[:, cols]) + bias_ref[:, cols]
            o_ref[rows, cols] = u[:, cols] * s


def _spatial_gating(z, gn, w_s, bias_full):
    groups, rows, _ = z.shape
    ub, vb = C_SG_U // BRANCH_W, C_SG_V // BRANCH_W
    return pl.pallas_call(
        _sg_kernel,
        grid=(groups, rows // TR_SG),
        in_specs=[pl.BlockSpec((None, TR_SG, BRANCH_W), lambda b, i: (b, i, ub)),
                  pl.BlockSpec((None, TR_SG, BRANCH_W), lambda b, i: (b, i, vb)),
                  pl.BlockSpec((1, BRANCH_W), lambda b, i: (0, 0)),
                  pl.BlockSpec((SG_GROUPS, ML_CHUNK, ML_CHUNK), lambda b, i: (0, 0, 0)),
                  pl.BlockSpec((ML_CHUNK, BRANCH_W), lambda b, i: (0, 0))],
        out_specs=pl.BlockSpec((None, TR_SG, BRANCH_W), lambda b, i: (b, i, 0)),
        out_shape=jax.ShapeDtypeStruct((groups, rows, BRANCH_W), F32),
        compiler_params=_params(("parallel", "parallel")),
        name="spatial_gating",
    )(z, z, gn, w_s, bias_full)


def _mlstm_chunk(q, k, v, gp, gpt, state, rev, with_out):
    L = ML_CHUNK
    j_i, j_b = (1, 3) if rev else (0, 2)
    li_c = gp[:, j_i:j_i + 1]
    b_c = gp[:, j_b:j_b + 1]
    b_end = b_c[0:1, :] if rev else b_c[L - 1:L, :]
    ks = k * (HEAD_DIM ** -0.5)
    a = b_end - b_c + li_c
    m_loc = jnp.max(a, axis=0, keepdims=True)
    wgt = jnp.exp(a - m_loc)
    ct_loc = _mm_tn(ks, v * wgt)
    n_loc = jnp.sum(ks * wgt, axis=0, keepdims=True)
    ct, n, m = state
    m_new = jnp.maximum(b_end + m, m_loc)
    sp = jnp.exp(b_end + m - m_new)
    sl = jnp.exp(m_loc - m_new)
    new_state = (sp * ct + sl * ct_loc, sp * n + sl * n_loc, m_new)
    if not with_out:
        return new_state, None
    li_r = gpt[j_i:j_i + 1, :]
    b_r = gpt[j_b:j_b + 1, :]
    t = lax.broadcasted_iota(jnp.int32, (L, L), 0)
    s = lax.broadcasted_iota(jnp.int32, (L, L), 1)
    incl = (s >= t) if rev else (s <= t)
    dm = jnp.where(incl, b_c - b_r + li_r, -jnp.inf)
    g = b_c + m
    m_t = jnp.maximum(jnp.max(dm, axis=1, keepdims=True), g)
    p = jnp.exp(dm - m_t) * _mm_nt(q, ks)
    inter = jnp.exp(g - m_t)
    num = _mm(p, v) + inter * _mm(q, ct)
    den = jnp.sum(p, axis=1, keepdims=True) + inter * jnp.sum(q * n, axis=1, keepdims=True)
    return new_state, num / jnp.maximum(jnp.abs(den), jnp.exp(-m_t))


def _mlstm_kernel(*refs, ctx_out, n_ctx, n_lat):
    if ctx_out:
        (kc_ref, vc_ref, gpc_ref, gptc_ref, qc_ref, oc_ref, kl_ref, vl_ref, gpl_ref, gptl_ref, ql_ref, ol_ref,
         norm_ref, yc_ref, yl_ref, hfc, hbc, hfl, hbl) = refs
    else:
        (kc_ref, vc_ref, gpc_ref, gptc_ref, kl_ref, vl_ref, gpl_ref, gptl_ref, ql_ref, ol_ref,
         norm_ref, yl_ref, hfl, hbl) = refs
        qc_ref = oc_ref = yc_ref = hfc = hbc = None
    L = ML_CHUNK

    def run(k_ref, v_ref, gp_ref, gpt_ref, q_ref, hf, hb, n_chunks, st_f, st_b, with_out):
        def body(j, carry):
            st_f, st_b = carry
            rf = pl.ds(pl.multiple_of(j * L, L), L)
            jb = n_chunks - 1 - j
            rb = pl.ds(pl.multiple_of(jb * L, L), L)
            st_f, h_f = _mlstm_chunk(q_ref[rf, :] if with_out else None, k_ref[rf, :], v_ref[rf, :],
                                     gp_ref[rf, :], gpt_ref[j], st_f, False, with_out)
            st_b, h_b = _mlstm_chunk(q_ref[rb, :] if with_out else None, k_ref[rb, :], v_ref[rb, :],
                                     gp_ref[rb, :], gpt_ref[jb], st_b, True, with_out)
            if with_out:
                hf[rf, :] = h_f
                hb[rb, :] = h_b
            return st_f, st_b
        return lax.fori_loop(0, n_chunks, body, (st_f, st_b))

    zero = (jnp.zeros((HEAD_DIM, HEAD_DIM), F32), jnp.zeros((1, HEAD_DIM), F32), jnp.zeros((1, 1), F32))
    st_f, st_b = run(kc_ref, vc_ref, gpc_ref, gptc_ref, qc_ref, hfc, hbc, n_ctx, zero, zero, ctx_out)
    run(kl_ref, vl_ref, gpl_ref, gptl_ref, ql_ref, hfl, hbl, n_lat, st_f, st_b, True)

    def finish(hf, hb, o_ref, y_ref):
        y_ref[...] = _sigmoid(o_ref[...]) * (_rms(hf[...] + hb[...]) * norm_ref[...])

    finish(hfl, hbl, ol_ref, yl_ref)
    if ctx_out:
        finish(hfc, hbc, oc_ref, yc_ref)


def _seq_spec(rows, col):
    return pl.BlockSpec((None, rows, HEAD_DIM), lambda b, h: (b, 0, col // HEAD_DIM + h))


def _gp_specs(rows):
    return [pl.BlockSpec((None, rows, LANES), lambda b, h: (b, 0, h)),
            pl.BlockSpec((None, rows // ML_CHUNK, N_GATE, ML_CHUNK), lambda b, h: (b, 0, h, 0))]


def _mlstm(zc, gpc, gptc, zl, gpl, gptl, norm, ctx_out):
    nb, tc, _ = zc.shape
    tl = zl.shape[1]
    ctx_in = [zc, zc, gpc, gptc] + ([zc, zc] if ctx_out else [])
    ctx_specs = [_seq_spec(tc, C_ML_K), _seq_spec(tc, C_ML_V)] + _gp_specs(tc) + (
        [_seq_spec(tc, C_ML_Q), _seq_spec(tc, C_ML_O)] if ctx_out else [])
    lat_in = [zl, zl, gpl, gptl, zl, zl]
    lat_specs = [_seq_spec(tl, C_ML_K), _seq_spec(tl, C_ML_V)] + _gp_specs(tl) + [
        _seq_spec(tl, C_ML_Q), _seq_spec(tl, C_ML_O)]
    out_spec = lambda rows: pl.BlockSpec((None, rows, HEAD_DIM), lambda b, h: (b, 0, h))
    out_shape = lambda rows: jax.ShapeDtypeStruct((nb, rows, BRANCH_W), F32)
    scr = lambda rows: [pltpu.VMEM((rows, HEAD_DIM), F32), pltpu.VMEM((rows, HEAD_DIM), F32)]
    res = pl.pallas_call(
        functools.partial(_mlstm_kernel, ctx_out=ctx_out, n_ctx=tc // ML_CHUNK, n_lat=tl // ML_CHUNK),
        grid=(nb, N_HEADS),
        in_specs=ctx_specs + lat_specs + [pl.BlockSpec((1, HEAD_DIM), lambda b, h: (0, h))],
        out_specs=([out_spec(tc)] if ctx_out else []) + [out_spec(tl)],
        out_shape=([out_shape(tc)] if ctx_out else []) + [out_shape(tl)],
        scratch_shapes=(scr(tc) if ctx_out else []) + scr(tl),
        compiler_params=_params(("parallel", "parallel")),
        name="mlstm",
    )(*ctx_in, *lat_in, norm)
    return (res[0], res[1]) if ctx_out else (None, res[0])


def _conv_silu(x, w3):
    rows = x.shape[0]
    t = lax.broadcasted_iota(jnp.int32, x.shape, 0)
    prev = jnp.where(t == 0, 0.0, pltpu.roll(x, 1, 0))
    nxt = jnp.where(t == rows - 1, 0.0, pltpu.roll(x, rows - 1, 0))
    return _silu(w3[0:1, :] * prev + w3[1:2, :] * x + w3[2:3, :] * nxt)


def _l2norm(x):
    return x * lax.rsqrt(jnp.sum(x * x, axis=-1, keepdims=True) + EPS)


def _chunk_totals(gc_r, rev):
    L = GD_CHUNK
    pos = lax.broadcasted_iota(jnp.int32, gc_r.shape, 1)
    tot = jnp.zeros_like(gc_r)
    for c in range(GD_SUPER // L):
        i = c * L if rev else c * L + L - 1
        tot = jnp.where((pos >= c * L) & (pos < (c + 1) * L), gc_r[:, i:i + 1], tot)
    return tot


def _gdn_prepare(k, v, qs, gp, gpt):
    dirs = (False, True)
    kk0 = _mm_nt(k, k)
    kT = k.T
    qk0 = None if qs is None else _mm_nt(qs, k)
    t = lax.broadcasted_iota(jnp.int32, (GD_SUPER, GD_SUPER), 0)
    s = lax.broadcasted_iota(jnp.int32, (GD_SUPER, GD_SUPER), 1)
    x = t ^ s
    same = x < GD_CHUNK
    beta, gc_c, gc_r, decay, m, eg = [], [], [], [], [], []
    for rev in dirs:
        j_b, j_g = (5, 7) if rev else (4, 6)
        beta.append(gp[:, j_b:j_b + 1])
        gc_c.append(gp[:, j_g:j_g + 1])
        gc_r.append(gpt[j_g:j_g + 1, :])
        incl = same & ((s >= t) if rev else (s <= t))
        strict = same & ((s > t) if rev else (s < t))
        decay.append(jnp.exp(jnp.where(incl, gc_c[-1] - gc_r[-1], -jnp.inf)))
        m.append(jnp.where(strict, (beta[-1] * kk0) * decay[-1], 0.0))
        eg.append(jnp.exp(gc_c[-1]))
    tinv = [jnp.where(x == 0, 1.0, 0.0) - jnp.where(x == 1, m[d], 0.0) for d in range(2)]
    for lvl in range(1, 6):
        tc = [_mm(tinv[d], jnp.where((x >> lvl) == 1, m[d], 0.0)) for d in range(2)]
        tinv = [tinv[d] - _mm(tc[d], tinv[d]) for d in range(2)]
    out = []
    for d, rev in enumerate(dirs):
        y = _mm(tinv[d], jnp.concatenate([v * beta[d], (k * beta[d]) * eg[d]], axis=1))
        kendT = (kT * jnp.exp(_chunk_totals(gc_r[d], rev) - gc_r[d])).astype(BF16)
        qd = None if qs is None else (qs * eg[d]).astype(BF16)
        qk = None if qs is None else (qk0 * decay[d]).astype(BF16)
        out.append((y[:, :HEAD_DIM], y[:, HEAD_DIM:].astype(BF16), kendT, qd, qk))
    return out


def _gdn_kernel(*refs, ctx_out, n_ctx, n_lat):
    n_in = 16 if ctx_out else 14
    n_out = 2 if ctx_out else 1
    if ctx_out:
        (kc_ref, vc_ref, gpc_ref, gptc_ref, qc_ref, zc_ref, kl_ref, vl_ref, gpl_ref, gptl_ref, ql_ref, zl_ref,
         wq_ref, wk_ref, wv_ref, norm_ref, yc_ref, yl_ref) = refs[:n_in + n_out]
    else:
        (kc_ref, vc_ref, gpc_ref, gptc_ref, kl_ref, vl_ref, gpl_ref, gptl_ref, ql_ref, zl_ref,
         wq_ref, wk_ref, wv_ref, norm_ref, yl_ref) = refs[:n_in + n_out]
        qc_ref = zc_ref = yc_ref = None
    ks, vs, qs, gpa, gpta, o_f, o_b = refs[n_in + n_out:n_in + n_out + 7]
    per_dir = refs[n_in + n_out + 7:]
    u_s, w_s, kT_s, qd_s, qk_s = (per_dir[0:2], per_dir[2:4], per_dir[4:6], per_dir[6:8], per_dir[8:10])
    L = GD_CHUNK
    SB = GD_SUPER
    tc = n_ctx * SB
    tiles = SB // ML_CHUNK

    ks[0:tc, :] = _l2norm(_conv_silu(kc_ref[...], wk_ref[...]))
    vs[0:tc, :] = _conv_silu(vc_ref[...], wv_ref[...])
    ks[tc:, :] = _l2norm(_conv_silu(kl_ref[...], wk_ref[...]))
    vs[tc:, :] = _conv_silu(vl_ref[...], wv_ref[...])
    qs[tc:, :] = _l2norm(_conv_silu(ql_ref[...], wq_ref[...])) * (HEAD_DIM ** -0.5)
    if ctx_out:
        qs[0:tc, :] = _l2norm(_conv_silu(qc_ref[...], wq_ref[...])) * (HEAD_DIM ** -0.5)
    gpa[0:tc, :] = gpc_ref[...]
    gpa[tc:, :] = gpl_ref[...]
    gpta[0:n_ctx * tiles] = gptc_ref[...]
    gpta[n_ctx * tiles:] = gptl_ref[...]

    def prepare(j, with_out):
        rows = pl.ds(pl.multiple_of(j * SB, SB), SB)
        gpt = jnp.concatenate([gpta[j * tiles + i] for i in range(tiles)], axis=1)
        res = _gdn_prepare(ks[rows, :], vs[rows, :], qs[rows, :] if with_out else None, gpa[rows, :], gpt)
        for d, (u, w, kT, qd, qk) in enumerate(res):
            u_s[d][rows, :] = u
            w_s[d][rows, :] = w
            kT_s[d][j] = kT
            if with_out:
                qd_s[d][rows, :] = qd
                qk_s[d][rows, :] = qk

    def prepare_loop(lo, n, with_out):
        def body(j, carry):
            prepare(j, with_out)
            return carry
        lax.fori_loop(lo, lo + n, body, 0)

    prepare_loop(0, n_ctx, ctx_out)
    prepare_loop(n_ctx, n_lat, True)

    def step(d, j, c, S, with_out):
        row0 = pl.multiple_of(j * SB + c * L, L)
        rows = pl.ds(row0, L)
        cols = slice(c * L, (c + 1) * L)
        edge = gpa[pl.ds(row0 + (0 if d else L - 1), 1), :]
        g_end = jnp.exp(edge[:, 7:8] if d else edge[:, 6:7])
        Sb = S.astype(BF16)
        v_new = (u_s[d][rows, :] - jnp.dot(w_s[d][rows, :], Sb, preferred_element_type=F32)).astype(BF16)
        if with_out:
            out = (jnp.dot(qd_s[d][rows, :], Sb, preferred_element_type=F32)
                   + jnp.dot(qk_s[d][rows, cols], v_new, preferred_element_type=F32))
            (o_b if d else o_f)[rows, :] = out
        return S * g_end + jnp.dot(kT_s[d][j, :, cols], v_new, preferred_element_type=F32)

    def scan(lo, n, S_f, S_b, with_out):
        def body(i, carry):
            S_f, S_b = carry
            for c in range(SB // L):
                S_f = step(0, lo + i, c, S_f, with_out)
                S_b = step(1, lo + n - 1 - i, SB // L - 1 - c, S_b, with_out)
            return S_f, S_b
        return lax.fori_loop(0, n, body, (S_f, S_b))

    zero = jnp.zeros((HEAD_DIM, HEAD_DIM), F32)
    S_f, S_b = scan(0, n_ctx, zero, zero, ctx_out)
    scan(n_ctx, n_lat, S_f, S_b, True)

    def finish(rows, z_ref, y_ref):
        y_ref[...] = (_rms(o_f[rows, :] + o_b[rows, :]) * norm_ref[...]) * _silu(z_ref[...])

    finish(slice(tc, None), zl_ref, yl_ref)
    if ctx_out:
        finish(slice(0, tc), zc_ref, yc_ref)


def _gdn(zc, gpc, gptc, zl, gpl, gptl, conv, norm, ctx_out):
    nb, tc, _ = zc.shape
    tl = zl.shape[1]
    ctx_in = [zc, zc, gpc, gptc] + ([zc, zc] if ctx_out else [])
    ctx_specs = [_seq_spec(tc, C_GD_K), _seq_spec(tc, C_GD_V)] + _gp_specs(tc) + (
        [_seq_spec(tc, C_GD_Q), _seq_spec(tc, C_GD_Z)] if ctx_out else [])
    lat_in = [zl, zl, gpl, gptl, zl, zl]
    lat_specs = [_seq_spec(tl, C_GD_K), _seq_spec(tl, C_GD_V)] + _gp_specs(tl) + [
        _seq_spec(tl, C_GD_Q), _seq_spec(tl, C_GD_Z)]
    conv_spec = lambda off: pl.BlockSpec((3, HEAD_DIM), lambda b, h: (0, off // HEAD_DIM + h))
    out_spec = lambda rows: pl.BlockSpec((None, rows, HEAD_DIM), lambda b, h: (b, 0, h))
    out_shape = lambda rows: jax.ShapeDtypeStruct((nb, rows, BRANCH_W), F32)
    rows = tc + tl
    seq = lambda dt: pltpu.VMEM((rows, HEAD_DIM), dt)
    two = lambda spec: [spec, spec]
    scratch = ([seq(F32), seq(F32), seq(F32), seq(F32), pltpu.VMEM((rows // ML_CHUNK, N_GATE, ML_CHUNK), F32),
                seq(F32), seq(F32)]
               + two(seq(F32)) + two(seq(BF16)) + two(pltpu.VMEM((rows // GD_SUPER, HEAD_DIM, GD_SUPER), BF16))
               + two(seq(BF16)) + two(pltpu.VMEM((rows, GD_SUPER), BF16)))
    res = pl.pallas_call(
        functools.partial(_gdn_kernel, ctx_out=ctx_out, n_ctx=tc // GD_SUPER, n_lat=tl // GD_SUPER),
        grid=(nb, N_HEADS),
        in_specs=ctx_specs + lat_specs + [conv_spec(0), conv_spec(BRANCH_W), conv_spec(2 * BRANCH_W),
                                          pl.BlockSpec((1, HEAD_DIM), lambda b, h: (0, 0))],
        out_specs=([out_spec(tc)] if ctx_out else []) + [out_spec(tl)],
        out_shape=([out_shape(tc)] if ctx_out else []) + [out_shape(tl)],
        scratch_shapes=scratch,
        compiler_params=_params(("parallel", "parallel")),
        name="gdn",
    )(*ctx_in, *lat_in, conv, conv, conv, norm)
    return (res[0], res[1]) if ctx_out else (None, res[0])


def _rope(x, cos, sin):
    lane = lax.broadcasted_iota(jnp.int32, x.shape, 1)
    quarter = HEAD_DIM // 4
    partner = jnp.where((lane & (2 * quarter - 1)) < quarter,
                        pltpu.roll(x, HEAD_DIM - quarter, 1), pltpu.roll(x, quarter, 1))
    return x * cos + partner * sin


def _attn_kernel(*refs, latent, tc):
    if latent:
        (q_ref, kc_ref, vc_ref, kl_ref, vl_ref, cos_ref, sin_ref, cosq_ref, sinq_ref, qn_ref, kn_ref,
         o_ref, k_scr, v_scr) = refs
    else:
        q_ref, kc_ref, vc_ref, qn_ref, kn_ref, o_ref, k_scr, v_scr = refs

    @pl.when(pl.program_id(2) == 0)
    def _():
        k_scr[0:tc, :] = (_rms(kc_ref[...]) * kn_ref[...]).astype(BF16)
        v_scr[0:tc, :] = vc_ref[...].astype(BF16)
        if latent:
            kl = _rope(_rms(kl_ref[...]) * kn_ref[...], cos_ref[...], sin_ref[...])
            k_scr[tc:, :] = kl.astype(BF16)
            v_scr[tc:, :] = vl_ref[...].astype(BF16)

    scale = HEAD_DIM ** -0.5
    for g in range(AT_Q_HEADS // AT_KV_HEADS):
        cols = slice(g * HEAD_DIM, (g + 1) * HEAD_DIM)
        q = _rms(q_ref[:, cols]) * qn_ref[...]
        if latent:
            q = _rope(q, cosq_ref[...], sinq_ref[...])
        s = lax.dot_general(q.astype(BF16), k_scr[...], (((1,), (1,)), ((), ())),
                            preferred_element_type=F32) * scale
        p = jnp.exp(s - jnp.max(s, axis=-1, keepdims=True))
        l = jnp.sum(p, axis=-1, keepdims=True)
        o_ref[:, cols] = jnp.dot(p.astype(BF16), v_scr[...], preferred_element_type=F32) / l


def _attention(zq, zc, zl, cos, sin, qn, kn):
    latent = zl is not None
    nb, tq_total, _ = zq.shape
    tc = zc.shape[1]
    tk = tc + (zl.shape[1] if latent else 0)
    gq = (AT_Q_HEADS // AT_KV_HEADS) * HEAD_DIM
    head_spec = lambda rows, col: pl.BlockSpec((None, rows, HEAD_DIM), lambda b, h, i: (b, 0, col // HEAD_DIM + h))
    vec_spec = pl.BlockSpec((1, HEAD_DIM), lambda b, h, i: (0, 0))
    in_specs = [pl.BlockSpec((None, TQ_ATTN, gq), lambda b, h, i: (b, i, C_AT_Q // gq + h)),
                head_spec(tc, C_AT_K), head_spec(tc, C_AT_V)]
    args = [zq, zc, zc]
    if latent:
        tl = zl.shape[1]
        in_specs += [head_spec(tl, C_AT_K), head_spec(tl, C_AT_V),
                     pl.BlockSpec((tl, HEAD_DIM), lambda b, h, i: (0, 0)),
                     pl.BlockSpec((tl, HEAD_DIM), lambda b, h, i: (0, 0)),
                     pl.BlockSpec((TQ_ATTN, HEAD_DIM), lambda b, h, i: (i, 0)),
                     pl.BlockSpec((TQ_ATTN, HEAD_DIM), lambda b, h, i: (i, 0))]
        args += [zl, zl, cos, sin, cos, sin]
    return pl.pallas_call(
        functools.partial(_attn_kernel, latent=latent, tc=tc),
        grid=(nb, AT_KV_HEADS, tq_total // TQ_ATTN),
        in_specs=in_specs + [vec_spec, vec_spec],
        out_specs=pl.BlockSpec((None, TQ_ATTN, gq), lambda b, h, i: (b, i, h)),
        out_shape=jax.ShapeDtypeStruct((nb, tq_total, BRANCH_W), F32),
        scratch_shapes=[pltpu.VMEM((tk, HEAD_DIM), BF16), pltpu.VMEM((tk, HEAD_DIM), BF16)],
        compiler_params=_params(("parallel", "parallel", "arbitrary")),
        name="attention",
    )(*args, qn, kn)


def _merge_kernel(x_ref, mod_ref, g_ref, xcol_ref, modcol_ref, y0_ref, y1_ref, y2_ref, y3_ref,
                  wg0_ref, wg1_ref, wg2_ref, wg3_ref, wb_ref, wo_ref, o_ref, h_scr, m_scr):
    s = pl.program_id(2)
    n_gate = D_MODEL // TN_MERGE
    ys = (y0_ref, y1_ref, y2_ref, y3_ref)
    wgs = (wg0_ref, wg1_ref, wg2_ref, wg3_ref)

    @pl.when(s == 0)
    def _():
        h_scr[...] = _norm_mod(x_ref[...], g_ref[...], mod_ref[3:4, :], mod_ref[4:5, :]).astype(BF16)

    @pl.when(s < n_gate)
    def _():
        h = h_scr[...]
        acc = jnp.zeros((TM_MERGE, TN_MERGE), F32)
        for n in range(N_BRANCH):
            gate = jnp.dot(h, wgs[n][...], preferred_element_type=F32)
            acc = acc + _sigmoid(gate) * _mm(ys[n][...], wb_ref[n])
        m_scr[s] = acc.astype(BF16)

    @pl.when(s >= n_gate)
    def _():
        acc = jnp.zeros((TM_MERGE, TN_OUT), F32)
        for kk in range(n_gate):
            acc = acc + jnp.dot(m_scr[kk], wo_ref[kk * TN_MERGE:(kk + 1) * TN_MERGE, :],
                                preferred_element_type=F32)
        o_ref[...] = xcol_ref[...] + modcol_ref[5:6, :] * acc


def _merge(x, mod, gain, ys, w_gate, w_branch, w_out, mod_row):
    groups, rows, _ = x.shape
    n_gate = D_MODEL // TN_MERGE
    n_out = D_MODEL // TN_OUT
    mrow = (lambda b: b) if mod_row is None else (lambda b: mod_row)
    gate_col = lambda s: jnp.minimum(s, n_gate - 1)
    out_col = lambda s: jnp.maximum(s - n_gate, 0)
    y_spec = pl.BlockSpec((None, TM_MERGE, BRANCH_W), lambda b, i, s: (b, i, 0))
    wg_spec = lambda n: pl.BlockSpec((D_MODEL, TN_MERGE), lambda b, i, s: (0, n * n_gate + gate_col(s)))
    return pl.pallas_call(
        _merge_kernel,
        grid=(groups, rows // TM_MERGE, n_gate + n_out),
        in_specs=[pl.BlockSpec((None, TM_MERGE, D_MODEL), lambda b, i, s: (b, i, 0)),
                  pl.BlockSpec((None, N_MOD, D_MODEL), lambda b, i, s: (mrow(b), 0, 0)),
                  pl.BlockSpec((1, D_MODEL), lambda b, i, s: (0, 0)),
                  pl.BlockSpec((None, TM_MERGE, TN_OUT), lambda b, i, s: (b, i, out_col(s))),
                  pl.BlockSpec((None, N_MOD, TN_OUT), lambda b, i, s: (mrow(b), 0, out_col(s))),
                  y_spec, y_spec, y_spec, y_spec,
                  wg_spec(0), wg_spec(1), wg_spec(2), wg_spec(3),
                  pl.BlockSpec((N_BRANCH, BRANCH_W, TN_MERGE), lambda b, i, s: (0, 0, gate_col(s))),
                  pl.BlockSpec((D_MODEL, TN_OUT), lambda b, i, s: (0, out_col(s)))],
        out_specs=pl.BlockSpec((None, TM_MERGE, TN_OUT), lambda b, i, s: (b, i, out_col(s))),
        out_shape=jax.ShapeDtypeStruct(x.shape, F32),
        scratch_shapes=[pltpu.VMEM((TM_MERGE, D_MODEL), BF16), pltpu.VMEM((n_gate, TM_MERGE, TN_MERGE), BF16)],
        compiler_params=_params(("parallel", "parallel", "arbitrary")),
        name="merge",
    )(x, mod, gain.reshape(1, D_MODEL), x, mod, *ys, w_gate, w_gate, w_gate, w_gate, w_branch, w_out)


def _rope_tables(seq):
    n = HEAD_DIM // 4
    inv = ROPE_THETA ** (-jnp.arange(n, dtype=F32) / n)
    pos = jnp.arange(seq)
    ar = (pos // GRID_W).astype(F32)[:, None] * inv
    ac = (pos % GRID_W).astype(F32)[:, None] * inv
    cos = jnp.concatenate([jnp.cos(ar), jnp.cos(ar), jnp.cos(ac), jnp.cos(ac)], axis=1)
    sin = jnp.concatenate([-jnp.sin(ar), jnp.sin(ar), -jnp.sin(ac), jnp.sin(ac)], axis=1)
    return cos, sin


def _gate_params(ml_if_bias, gd_a_log, gd_dt_bias):
    zeros = jnp.zeros((2, N_HEADS), F32)
    bias = jnp.concatenate([ml_if_bias.reshape(4, N_HEADS), zeros, gd_dt_bias], axis=0).T
    alog = jnp.concatenate([jnp.zeros((6, N_HEADS), F32), gd_a_log], axis=0).T
    pad = lambda a: jnp.pad(a, ((0, 0), (0, LANES - N_GATE))).reshape(1, N_HEADS * LANES)
    col = lambda a: a.reshape(N_HEADS * N_GATE, 1)
    return pad(bias), pad(alog), col(bias), col(alog)


def _gate_weights(w_in):
    g = jnp.concatenate([w_in[:, R_ML_IF:R_ML_IF + 16], w_in[:, R_GD_BA:R_GD_BA + 16]], axis=1)
    g = g.reshape(D_MODEL, N_GATE, N_HEADS).transpose(0, 2, 1)
    w_gc = jnp.pad(g, ((0, 0), (0, 0), (0, LANES - N_GATE))).reshape(D_MODEL, N_HEADS * LANES)
    w_gr = g.reshape(D_MODEL, N_HEADS * N_GATE).T
    return w_gc.astype(BF16), w_gr.astype(BF16)


def kernel(x, c, ctx, c_ctx, mod_w, mod_b, ffn1_norm, ffn1_w_in, ffn1_w_out, mix_norm, w_in, sg_norm, sg_w, sg_b, ml_if_bias, ml_norm, gd_conv, gd_a_log, gd_dt_bias, gd_norm, at_q_norm, at_k_norm, w_branch, w_out, ffn2_norm, ffn2_w_in, ffn2_w_out):
    nb, seq, _ = x.shape
    tc = ctx.shape[1]
    depth = mod_w.shape[0]
    ctx_row = nb
    cos, sin = _rope_tables(seq)

    cc = jnp.zeros((MOD_ROWS, D_MODEL), F32).at[:nb].set(c).at[ctx_row].set(c_ctx)
    mods = _modulation(cc, mod_w, mod_b).reshape(depth, MOD_ROWS, N_MOD, D_MODEL)

    flat = lambda a: a.reshape(1, nb * tc, a.shape[-1])
    unflat = lambda a: a.reshape(nb, tc, a.shape[-1])
    xc = ctx
    for l in range(depth):
        last = l == depth - 1
        ctx_out = not last
        mod = mods[l]
        wl = w_in[l]
        w_main = jnp.concatenate([wl[:, :R_ML_IF], wl[:, R_GD_KV:R_GD_BA], wl[:, R_AT_KV:R_GATE]], axis=1).astype(BF16)
        w_gate = wl[:, R_GATE:].astype(BF16)
        w_gc, w_gr = _gate_weights(wl)
        bias_c, alog_c, bias_r, alog_r = _gate_params(ml_if_bias[l], gd_a_log[l], gd_dt_bias[l])
        f1_in, f1_out = ffn1_w_in[l].astype(BF16), ffn1_w_out[l].astype(BF16)
        f2_in, f2_out = ffn2_w_in[l].astype(BF16), ffn2_w_out[l].astype(BF16)
        wb, wo = w_branch[l].astype(BF16), w_out[l].astype(BF16)
        sgw = sg_w[l].astype(BF16)
        sg_bias = jnp.repeat(sg_b[l].T, BRANCH_W // SG_GROUPS, axis=1)
        sgn = sg_norm[l].reshape(1, BRANCH_W)
        mln = ml_norm[l].reshape(1, BRANCH_W)
        gdn_g = gd_norm[l].reshape(1, HEAD_DIM)
        qn, kn = at_q_norm[l].reshape(1, HEAD_DIM), at_k_norm[l].reshape(1, HEAD_DIM)

        x = _ffn(x, mod, ffn1_norm[l], f1_in, f1_out, 0, None)
        xc = unflat(_ffn(flat(xc), mod, ffn1_norm[l], f1_in, f1_out, 0, ctx_row))

        zl, gcl, grl = _inproj(x, mod, mix_norm[l], w_main, w_gc, w_gr, N_MAIN, None)
        zc, gcc, grc = _inproj(flat(xc), mod, mix_norm[l], w_main, w_gc, w_gr, N_MAIN if ctx_out else KV_MAIN, ctx_row)
        zc = unflat(zc)
        gcc = unflat(gcc)
        grc = grc.reshape(N_HEADS * N_GATE, nb, tc).transpose(1, 0, 2)
        gpl, gptl = _gateprep(gcl, grl, bias_c, alog_c, bias_r, alog_r)
        gpc, gptc = _gateprep(gcc, grc, bias_c, alog_c, bias_r, alog_r)

        y_sg_l = _spatial_gating(zl, sgn, sgw, sg_bias)
        y_ml_c, y_ml_l = _mlstm(zc, gpc, gptc, zl, gpl, gptl, mln, ctx_out)
        y_gd_c, y_gd_l = _gdn(zc, gpc, gptc, zl, gpl, gptl, gd_conv[l], gdn_g, ctx_out)
        y_at_l = _attention(zl, zc, zl, cos, sin, qn, kn)
        x = _merge(x, mod, mix_norm[l], [y_sg_l, y_ml_l, y_gd_l, y_at_l], w_gate, wb, wo, None)
        if ctx_out:
            y_sg_c = _spatial_gating(zc, sgn, sgw, sg_bias)
            y_at_c = _attention(zc, zc, None, None, None, qn, kn)
            xc = unflat(_merge(flat(xc), mod, mix_norm[l], [flat(y) for y in (y_sg_c, y_ml_c, y_gd_c, y_at_c)],
                               w_gate, wb, wo, ctx_row))

        x = _ffn(x, mod, ffn2_norm[l], f2_in, f2_out, 6, None)
        if ctx_out:
            xc = unflat(_ffn(flat(xc), mod, ffn2_norm[l], f2_in, f2_out, 6, ctx_row))
    return x
```

```python
import functools
import math

import jax
import jax.numpy as jnp
from jax import lax
from jax.experimental import pallas as pl
from jax.experimental.pallas import tpu as pltpu

F32 = jnp.float32
BF16 = jnp.bfloat16
EPS = 1e-6

D_MODEL = 2048
D_FF = 5632
N_MOD = 9
N_BRANCH = 4
BRANCH_W = 512
HEAD_DIM = 128
N_HEADS = 4
SG_GROUPS = 4
ML_CHUNK = 128
GD_CHUNK = 64
GD_SUPER = 256
AT_Q_HEADS = 4
AT_KV_HEADS = 2
GRID_W = 64
ROPE_THETA = 10000.0
FFN_RESIDUAL = 0.5

LANES = 128
N_GATE = 8
MOD_ROWS = 16

C_ML_K, C_ML_V = 0, 512
C_GD_K, C_GD_V = 1024, 1536
C_AT_K, C_AT_V = 2048, 2304
KV_MAIN = 2560
C_SG_U, C_SG_V = 2560, 3072
C_ML_Q, C_ML_O = 3584, 4096
C_GD_Q, C_GD_Z = 4608, 5120
C_AT_Q = 5632
N_MAIN = 6144
R_ML_IF = 1024
R_GD_KV = 1040
R_GD_BA = 2064
R_AT_KV = 2080
R_REST = 2592
R_GATE = 6176

VMEM_LIMIT = 56 * 1024 * 1024

TM_FFN = 512
TF_FFN = 512
TM_PROJ = 1024
TN_PROJ = 512
TM_MERGE = 1024
TN_MERGE = 256
TN_OUT = 512
TQ_ATTN = 256
TR_SG = 256
TG_GATE = 256
TN_MOD = 2048


def _params(sem):
    return pltpu.CompilerParams(dimension_semantics=sem, vmem_limit_bytes=VMEM_LIMIT)


def _mm(a, b):
    return jnp.dot(a.astype(BF16), b.astype(BF16), preferred_element_type=F32)


def _mm_nt(a, b):
    return lax.dot_general(a.astype(BF16), b.astype(BF16), (((1,), (1,)), ((), ())),
                           preferred_element_type=F32)


def _mm_tn(a, b):
    return lax.dot_general(a.astype(BF16), b.astype(BF16), (((0,), (0,)), ((), ())),
                           preferred_element_type=F32)


def _sigmoid(x):
    return 1.0 / (1.0 + jnp.exp(-x))


def _silu(x):
    return x * _sigmoid(x)


def _softplus(x):
    return jnp.maximum(x, 0.0) + jnp.log1p(jnp.exp(-jnp.abs(x)))


def _gelu_tanh(x):
    c = math.sqrt(2.0 / math.pi)
    return x * (0.5 * (1.0 + jnp.tanh(c * (x + 0.044715 * (x * x * x)))))


def _rms(x):
    return x * lax.rsqrt(jnp.mean(x * x, axis=-1, keepdims=True) + EPS)


def _norm_mod(x, gain, shift, scale):
    return (_rms(x) * gain) * (1.0 + scale) + shift


def _mod_kernel(cc_ref, w_ref, b_ref, o_ref):
    o_ref[...] = _mm(_silu(cc_ref[...]), w_ref[...]) + b_ref[...]


def _modulation(cc, mod_w, mod_b):
    depth = mod_w.shape[0]
    n = mod_w.shape[2]
    return pl.pallas_call(
        _mod_kernel,
        grid=(depth, n // TN_MOD),
        in_specs=[pl.BlockSpec((MOD_ROWS, D_MODEL), lambda l, j: (0, 0)),
                  pl.BlockSpec((None, D_MODEL, TN_MOD), lambda l, j: (l, 0, j)),
                  pl.BlockSpec((None, 1, TN_MOD), lambda l, j: (l, 0, j))],
        out_specs=pl.BlockSpec((None, MOD_ROWS, TN_MOD), lambda l, j: (l, 0, j)),
        out_shape=jax.ShapeDtypeStruct((depth, MOD_ROWS, n), F32),
        compiler_params=_params(("parallel", "arbitrary")),
        name="modulation",
    )(cc, mod_w, mod_b.reshape(depth, 1, n))


def _ffn_kernel(x_ref, mod_ref, g_ref, wa_ref, wb_ref, wo_ref, o_ref, h_scr, acc_scr, *, base):
    j = pl.program_id(2)

    @pl.when(j == 0)
    def _():
        h = _norm_mod(x_ref[...], g_ref[...], mod_ref[base:base + 1, :], mod_ref[base + 1:base + 2, :])
        h_scr[...] = h.astype(BF16)
        acc_scr[...] = jnp.zeros_like(acc_scr)

    h = h_scr[...]
    a = jnp.dot(h, wa_ref[...], preferred_element_type=F32)
    b = jnp.dot(h, wb_ref[...], preferred_element_type=F32)
    acc_scr[...] += _mm(_silu(a) * b, wo_ref[...])

    @pl.when(j == pl.num_programs(2) - 1)
    def _():
        o_ref[...] = x_ref[...] + (FFN_RESIDUAL * mod_ref[base + 2:base + 3, :]) * acc_scr[...]


def _ffn(x, mod, gain, w_in, w_out, base, mod_row):
    groups, rows, _ = x.shape
    nj = D_FF // TF_FFN
    mod_idx = (lambda b, i, j: (b, 0, 0)) if mod_row is None else (lambda b, i, j: (mod_row, 0, 0))
    return pl.pallas_call(
        functools.partial(_ffn_kernel, base=base),
        grid=(groups, rows // TM_FFN, nj),
        in_specs=[pl.BlockSpec((None, TM_FFN, D_MODEL), lambda b, i, j: (b, i, 0)),
                  pl.BlockSpec((None, N_MOD, D_MODEL), mod_idx),
                  pl.BlockSpec((1, D_MODEL), lambda b, i, j: (0, 0)),
                  pl.BlockSpec((D_MODEL, TF_FFN), lambda b, i, j: (0, j)),
                  pl.BlockSpec((D_MODEL, TF_FFN), lambda b, i, j: (0, j + nj)),
                  pl.BlockSpec((TF_FFN, D_MODEL), lambda b, i, j: (j, 0))],
        out_specs=pl.BlockSpec((None, TM_FFN, D_MODEL), lambda b, i, j: (b, i, 0)),
        out_shape=jax.ShapeDtypeStruct(x.shape, F32),
        scratch_shapes=[pltpu.VMEM((TM_FFN, D_MODEL), BF16), pltpu.VMEM((TM_FFN, D_MODEL), F32)],
        compiler_params=_params(("parallel", "parallel", "arbitrary")),
        name="ffn",
    )(x, mod, gain.reshape(1, D_MODEL), w_in, w_in, w_out)


def _inproj_kernel(x_ref, mod_ref, g_ref, w_ref, wgc_ref, wgr_ref, z_ref, gc_ref, gr_ref, h_ref):
    n = pl.program_id(2)

    @pl.when(n == 0)
    def _():
        h = _norm_mod(x_ref[...], g_ref[...], mod_ref[3:4, :], mod_ref[4:5, :]).astype(BF16)
        h_ref[...] = h
        gc_ref[...] = jnp.dot(h, wgc_ref[...], preferred_element_type=F32)
        gr_ref[...] = lax.dot_general(wgr_ref[...], h, (((1,), (1,)), ((), ())), preferred_element_type=F32)

    z_ref[...] = jnp.dot(h_ref[...], w_ref[...], preferred_element_type=F32)


def _inproj(x, mod, gain, w_main, w_gc, w_gr, n_cols, mod_row):
    groups, rows, _ = x.shape
    mod_idx = (lambda b, i, n: (b, 0, 0)) if mod_row is None else (lambda b, i, n: (mod_row, 0, 0))
    return pl.pallas_call(
        _inproj_kernel,
        grid=(groups, rows // TM_PROJ, n_cols // TN_PROJ),
        in_specs=[pl.BlockSpec((None, TM_PROJ, D_MODEL), lambda b, i, n: (b, i, 0)),
                  pl.BlockSpec((None, N_MOD, D_MODEL), mod_idx),
                  pl.BlockSpec((1, D_MODEL), lambda b, i, n: (0, 0)),
                  pl.BlockSpec((D_MODEL, TN_PROJ), lambda b, i, n: (0, n)),
                  pl.BlockSpec((D_MODEL, N_HEADS * LANES), lambda b, i, n: (0, 0)),
                  pl.BlockSpec((N_HEADS * N_GATE, D_MODEL), lambda b, i, n: (0, 0))],
        out_specs=[pl.BlockSpec((None, TM_PROJ, TN_PROJ), lambda b, i, n: (b, i, n)),
                   pl.BlockSpec((None, TM_PROJ, N_HEADS * LANES), lambda b, i, n: (b, i, 0)),
                   pl.BlockSpec((None, N_HEADS * N_GATE, TM_PROJ), lambda b, i, n: (b, 0, i)),
                   pl.BlockSpec((None, TM_PROJ, D_MODEL), lambda b, i, n: (b, i, 0))],
        out_shape=[jax.ShapeDtypeStruct((groups, rows, n_cols), F32),
                   jax.ShapeDtypeStruct((groups, rows, N_HEADS * LANES), F32),
                   jax.ShapeDtypeStruct((groups, N_HEADS * N_GATE, rows), F32),
                   jax.ShapeDtypeStruct((groups, rows, D_MODEL), BF16)],
        compiler_params=_params(("parallel", "parallel", "arbitrary")),
        name="inproj",
    )(x, mod, gain.reshape(1, D_MODEL), w_main, w_gc, w_gr)


def _split3(e):
    e1 = e.astype(BF16)
    r1 = e - e1.astype(F32)
    e2 = r1.astype(BF16)
    e3 = (r1 - e2.astype(F32)).astype(BF16)
    return e1, e2, e3


def _gate_values(pre, a_log, cls):
    neg_sp = -_softplus(-pre)
    return jnp.where(cls < 2, pre,
                     jnp.where(cls < 4, neg_sp,
                               jnp.where(cls < 6, _sigmoid(pre), -jnp.exp(a_log) * _softplus(pre))))


def _gateprep_kernel(gc_ref, gr_ref, bc_ref, ac_ref, br_ref, ar_ref, oc_ref, or_ref):
    for i in range(TG_GATE // ML_CHUNK):
        tile = slice(i * ML_CHUNK, (i + 1) * ML_CHUNK)
        oc_ref[tile, :], or_ref[i] = _gateprep_tile(gc_ref[tile, :], gr_ref[:, tile], bc_ref[...], ac_ref[...],
                                                    br_ref[...], ar_ref[...])


def _gateprep_tile(gc, gr, bias_c, alog_c, bias_r, alog_r):
    L = ML_CHUNK
    half = GD_CHUNK
    pre = gc + bias_c
    cls = lax.broadcasted_iota(jnp.int32, pre.shape, 1) & (LANES - 1)
    t = lax.broadcasted_iota(jnp.int32, pre.shape, 0)
    e = _gate_values(pre, alog_c, cls)
    tri = (lax.broadcasted_iota(jnp.int32, (L, L), 1) <= lax.broadcasted_iota(jnp.int32, (L, L), 0)).astype(BF16)
    e1, e2, e3 = _split3(e)
    p = (jnp.dot(tri, e1, preferred_element_type=F32) + jnp.dot(tri, e2, preferred_element_type=F32)
         + jnp.dot(tri, e3, preferred_element_type=F32))
    tot = p[L - 1:L, :]
    mid = p[half - 1:half, :]
    second = t >= half
    pre64 = p - jnp.where(second, mid, 0.0)
    suf64 = jnp.where(second, tot, mid) - p + e
    cum = jnp.where(cls == 2, p, jnp.where(cls == 3, tot - p + e, jnp.where(cls == 6, pre64, suf64)))
    out_c = jnp.where((cls == 2) | (cls == 3) | (cls == 6) | (cls == 7), cum, e)

    pre_r = gr + bias_r
    cls_r = lax.broadcasted_iota(jnp.int32, pre_r.shape, 0) & (N_GATE - 1)
    t_r = lax.broadcasted_iota(jnp.int32, pre_r.shape, 1)
    e_r = _gate_values(pre_r, alog_r, cls_r)
    tri_r = (lax.broadcasted_iota(jnp.int32, (L, L), 0) <= lax.broadcasted_iota(jnp.int32, (L, L), 1)).astype(BF16)
    f1, f2, f3 = _split3(e_r)
    pr = (jnp.dot(f1, tri_r, preferred_element_type=F32) + jnp.dot(f2, tri_r, preferred_element_type=F32)
          + jnp.dot(f3, tri_r, preferred_element_type=F32))
    tot_r = pr[:, L - 1:L]
    mid_r = pr[:, half - 1:half]
    second_r = t_r >= half
    pre64_r = pr - jnp.where(second_r, mid_r, 0.0)
    suf64_r = jnp.where(second_r, tot_r, mid_r) - pr + e_r
    cum_r = jnp.where(cls_r == 2, pr, jnp.where(cls_r == 3, tot_r - pr + e_r, jnp.where(cls_r == 6, pre64_r, suf64_r)))
    return out_c, jnp.where((cls_r == 2) | (cls_r == 3) | (cls_r == 6) | (cls_r == 7), cum_r, e_r)


def _gateprep(gc, gr, bias_c, alog_c, bias_r, alog_r):
    groups, rows, _ = gc.shape
    nt = rows // ML_CHUNK
    hw = N_HEADS * LANES
    hg = N_HEADS * N_GATE
    tiles = TG_GATE // ML_CHUNK
    return pl.pallas_call(
        _gateprep_kernel,
        grid=(groups, rows // TG_GATE),
        in_specs=[pl.BlockSpec((None, TG_GATE, hw), lambda b, c: (b, c, 0)),
                  pl.BlockSpec((None, hg, TG_GATE), lambda b, c: (b, 0, c)),
                  pl.BlockSpec((1, hw), lambda b, c: (0, 0)),
                  pl.BlockSpec((1, hw), lambda b, c: (0, 0)),
                  pl.BlockSpec((hg, 1), lambda b, c: (0, 0)),
                  pl.BlockSpec((hg, 1), lambda b, c: (0, 0))],
        out_specs=[pl.BlockSpec((None, TG_GATE, hw), lambda b, c: (b, c, 0)),
                   pl.BlockSpec((None, tiles, hg, ML_CHUNK), lambda b, c: (b, c, 0, 0))],
        out_shape=[jax.ShapeDtypeStruct((groups, rows, hw), F32),
                   jax.ShapeDtypeStruct((groups, nt, hg, ML_CHUNK), F32)],
        compiler_params=_params(("parallel", "parallel")),
        name="gateprep",
    )(gc, gr, bias_c, alog_c, bias_r, alog_r)


def _sg_kernel(u_ref, v_ref, gn_ref, w_ref, bias_ref, o_ref):
    gw = BRANCH_W // SG_GROUPS
    for ch in range(TR_SG // ML_CHUNK):
        rows = slice(ch * ML_CHUNK, (ch + 1) * ML_CHUNK)
        v = _rms(_gelu_tanh(v_ref[rows, :])) * gn_ref[...]
        u = _gelu_tanh(u_ref[rows, :])
        for g in range(SG_GROUPS):
            cols = slice(g * gw, (g + 1) * gw)
            s = _mm(w_ref[g], v[:, cols]) + bias_ref[:, cols]
            o_ref[rows, cols] = (u[:, cols] * s).astype(o_ref.dtype)


def _spatial_gating(z, gn, w_s, bias_full):
    groups, rows, _ = z.shape
    ub, vb = C_SG_U // BRANCH_W, C_SG_V // BRANCH_W
    return pl.pallas_call(
        _sg_kernel,
        grid=(groups, rows // TR_SG),
        in_specs=[pl.BlockSpec((None, TR_SG, BRANCH_W), lambda b, i: (b, i, ub)),
                  pl.BlockSpec((None, TR_SG, BRANCH_W), lambda b, i: (b, i, vb)),
                  pl.BlockSpec((1, BRANCH_W), lambda b, i: (0, 0)),
                  pl.BlockSpec((SG_GROUPS, ML_CHUNK, ML_CHUNK), lambda b, i: (0, 0, 0)),
                  pl.BlockSpec((ML_CHUNK, BRANCH_W), lambda b, i: (0, 0))],
        out_specs=pl.BlockSpec((None, TR_SG, BRANCH_W), lambda b, i: (b, i, 0)),
        out_shape=jax.ShapeDtypeStruct((groups, rows, BRANCH_W), BF16),
        compiler_params=_params(("parallel", "parallel")),
        name="spatial_gating",
    )(z, z, gn, w_s, bias_full)


def _mlstm_chunk(q, k, v, gp, gpt, state, rev, with_out):
    L = ML_CHUNK
    j_i, j_b = (1, 3) if rev else (0, 2)
    li_c = gp[:, j_i:j_i + 1]
    b_c = gp[:, j_b:j_b + 1]
    b_end = b_c[0:1, :] if rev else b_c[L - 1:L, :]
    ks = k * (HEAD_DIM ** -0.5)
    a = b_end - b_c + li_c
    m_loc = jnp.max(a, axis=0, keepdims=True)
    wgt = jnp.exp(a - m_loc)
    ct_loc = _mm_tn(ks, v * wgt)
    n_loc = jnp.sum(ks * wgt, axis=0, keepdims=True)
    ct, n, m = state
    m_new = jnp.maximum(b_end + m, m_loc)
    sp = jnp.exp(b_end + m - m_new)
    sl = jnp.exp(m_loc - m_new)
    new_state = (sp * ct + sl * ct_loc, sp * n + sl * n_loc, m_new)
    if not with_out:
        return new_state, None
    li_r = gpt[j_i:j_i + 1, :]
    b_r = gpt[j_b:j_b + 1, :]
    t = lax.broadcasted_iota(jnp.int32, (L, L), 0)
    s = lax.broadcasted_iota(jnp.int32, (L, L), 1)
    incl = (s >= t) if rev else (s <= t)
    dm = jnp.where(incl, b_c - b_r + li_r, -jnp.inf)
    g = b_c + m
    m_t = jnp.maximum(jnp.max(dm, axis=1, keepdims=True), g)
    p = jnp.exp(dm - m_t) * _mm_nt(q, ks)
    inter = jnp.exp(g - m_t)
    num = _mm(p, v) + inter * _mm(q, ct)
    den = jnp.sum(p, axis=1, keepdims=True) + inter * jnp.sum(q * n, axis=1, keepdims=True)
    return new_state, num / jnp.maximum(jnp.abs(den), jnp.exp(-m_t))


def _mlstm_kernel(*refs, ctx_out, n_ctx, n_lat):
    if ctx_out:
        (kc_ref, vc_ref, gpc_ref, gptc_ref, qc_ref, oc_ref, kl_ref, vl_ref, gpl_ref, gptl_ref, ql_ref, ol_ref,
         norm_ref, yc_ref, yl_ref, hfc, hbc, hfl, hbl) = refs
    else:
        (kc_ref, vc_ref, gpc_ref, gptc_ref, kl_ref, vl_ref, gpl_ref, gptl_ref, ql_ref, ol_ref,
         norm_ref, yl_ref, hfl, hbl) = refs
        qc_ref = oc_ref = yc_ref = hfc = hbc = None
    L = ML_CHUNK

    def run(k_ref, v_ref, gp_ref, gpt_ref, q_ref, hf, hb, n_chunks, st_f, st_b, with_out):
        def body(j, carry):
            st_f, st_b = carry
            rf = pl.ds(pl.multiple_of(j * L, L), L)
            jb = n_chunks - 1 - j
            rb = pl.ds(pl.multiple_of(jb * L, L), L)
            st_f, h_f = _mlstm_chunk(q_ref[rf, :] if with_out else None, k_ref[rf, :], v_ref[rf, :],
                                     gp_ref[rf, :], gpt_ref[j], st_f, False, with_out)
            st_b, h_b = _mlstm_chunk(q_ref[rb, :] if with_out else None, k_ref[rb, :], v_ref[rb, :],
                                     gp_ref[rb, :], gpt_ref[jb], st_b, True, with_out)
            if with_out:
                hf[rf, :] = h_f
                hb[rb, :] = h_b
            return st_f, st_b
        return lax.fori_loop(0, n_chunks, body, (st_f, st_b))

    zero = (jnp.zeros((HEAD_DIM, HEAD_DIM), F32), jnp.zeros((1, HEAD_DIM), F32), jnp.zeros((1, 1), F32))
    st_f, st_b = run(kc_ref, vc_ref, gpc_ref, gptc_ref, qc_ref, hfc, hbc, n_ctx, zero, zero, ctx_out)
    run(kl_ref, vl_ref, gpl_ref, gptl_ref, ql_ref, hfl, hbl, n_lat, st_f, st_b, True)

    def finish(hf, hb, o_ref, y_ref):
        y_ref[...] = (_sigmoid(o_ref[...]) * (_rms(hf[...] + hb[...]) * norm_ref[...])).astype(y_ref.dtype)

    finish(hfl, hbl, ol_ref, yl_ref)
    if ctx_out:
        finish(hfc, hbc, oc_ref, yc_ref)


def _seq_spec(rows, col):
    return pl.BlockSpec((None, rows, HEAD_DIM), lambda b, h: (b, 0, col // HEAD_DIM + h))


def _gp_specs(rows):
    return [pl.BlockSpec((None, rows, LANES), lambda b, h: (b, 0, h)),
            pl.BlockSpec((None, rows // ML_CHUNK, N_GATE, ML_CHUNK), lambda b, h: (b, 0, h, 0))]


def _mlstm(zc, gpc, gptc, zl, gpl, gptl, norm, ctx_out):
    nb, tc, _ = zc.shape
    tl = zl.shape[1]
    ctx_in = [zc, zc, gpc, gptc] + ([zc, zc] if ctx_out else [])
    ctx_specs = [_seq_spec(tc, C_ML_K), _seq_spec(tc, C_ML_V)] + _gp_specs(tc) + (
        [_seq_spec(tc, C_ML_Q), _seq_spec(tc, C_ML_O)] if ctx_out else [])
    lat_in = [zl, zl, gpl, gptl, zl, zl]
    lat_specs = [_seq_spec(tl, C_ML_K), _seq_spec(tl, C_ML_V)] + _gp_specs(tl) + [
        _seq_spec(tl, C_ML_Q), _seq_spec(tl, C_ML_O)]
    out_spec = lambda rows: pl.BlockSpec((None, rows, HEAD_DIM), lambda b, h: (b, 0, h))
    out_shape = lambda rows: jax.ShapeDtypeStruct((nb, rows, BRANCH_W), BF16)
    scr = lambda rows: [pltpu.VMEM((rows, HEAD_DIM), F32), pltpu.VMEM((rows, HEAD_DIM), F32)]
    res = pl.pallas_call(
        functools.partial(_mlstm_kernel, ctx_out=ctx_out, n_ctx=tc // ML_CHUNK, n_lat=tl // ML_CHUNK),
        grid=(nb, N_HEADS),
        in_specs=ctx_specs + lat_specs + [pl.BlockSpec((1, HEAD_DIM), lambda b, h: (0, h))],
        out_specs=([out_spec(tc)] if ctx_out else []) + [out_spec(tl)],
        out_shape=([out_shape(tc)] if ctx_out else []) + [out_shape(tl)],
        scratch_shapes=(scr(tc) if ctx_out else []) + scr(tl),
        compiler_params=_params(("parallel", "parallel")),
        name="mlstm",
    )(*ctx_in, *lat_in, norm)
    return (res[0], res[1]) if ctx_out else (None, res[0])


def _conv_silu(x, w3):
    rows = x.shape[0]
    t = lax.broadcasted_iota(jnp.int32, x.shape, 0)
    prev = jnp.where(t == 0, 0.0, pltpu.roll(x, 1, 0))
    nxt = jnp.where(t == rows - 1, 0.0, pltpu.roll(x, rows - 1, 0))
    return _silu(w3[0:1, :] * prev + w3[1:2, :] * x + w3[2:3, :] * nxt)


def _l2norm(x):
    return x * lax.rsqrt(jnp.sum(x * x, axis=-1, keepdims=True) + EPS)


def _chunk_totals(gc_r, rev):
    L = GD_CHUNK
    pos = lax.broadcasted_iota(jnp.int32, gc_r.shape, 1)
    tot = jnp.zeros_like(gc_r)
    for c in range(GD_SUPER // L):
        i = c * L if rev else c * L + L - 1
        tot = jnp.where((pos >= c * L) & (pos < (c + 1) * L), gc_r[:, i:i + 1], tot)
    return tot


def _gdn_prepare(k, v, qs, gp, gpt):
    dirs = (False, True)
    kk0 = _mm_nt(k, k)
    kT = k.T
    qk0 = None if qs is None else _mm_nt(qs, k)
    t = lax.broadcasted_iota(jnp.int32, (GD_SUPER, GD_SUPER), 0)
    s = lax.broadcasted_iota(jnp.int32, (GD_SUPER, GD_SUPER), 1)
    x = t ^ s
    same = x < GD_CHUNK
    beta, gc_c, gc_r, decay, m, eg = [], [], [], [], [], []
    for rev in dirs:
        j_b, j_g = (5, 7) if rev else (4, 6)
        beta.append(gp[:, j_b:j_b + 1])
        gc_c.append(gp[:, j_g:j_g + 1])
        gc_r.append(gpt[j_g:j_g + 1, :])
        incl = same & ((s >= t) if rev else (s <= t))
        strict = same & ((s > t) if rev else (s < t))
        decay.append(jnp.exp(jnp.where(incl, gc_c[-1] - gc_r[-1], -jnp.inf)))
        m.append(jnp.where(strict, (beta[-1] * kk0) * decay[-1], 0.0))
        eg.append(jnp.exp(gc_c[-1]))
    tinv = [jnp.where(x == 0, 1.0, 0.0) - jnp.where(x == 1, m[d], 0.0) for d in range(2)]
    for lvl in range(1, 6):
        tc = [_mm(tinv[d], jnp.where((x >> lvl) == 1, m[d], 0.0)) for d in range(2)]
        tinv = [tinv[d] - _mm(tc[d], tinv[d]) for d in range(2)]
    out = []
    for d, rev in enumerate(dirs):
        y = _mm(tinv[d], jnp.concatenate([v * beta[d], (k * beta[d]) * eg[d]], axis=1))
        kendT = (kT * jnp.exp(_chunk_totals(gc_r[d], rev) - gc_r[d])).astype(BF16)
        qd = None if qs is None else (qs * eg[d]).astype(BF16)
        qk = None if qs is None else (qk0 * decay[d]).astype(BF16)
        out.append((y[:, :HEAD_DIM], y[:, HEAD_DIM:].astype(BF16), kendT, qd, qk))
    return out


def _gdn_kernel(*refs, ctx_out, n_ctx, n_lat):
    n_in = 16 if ctx_out else 14
    n_out = 2 if ctx_out else 1
    if ctx_out:
        (kc_ref, vc_ref, gpc_ref, gptc_ref, qc_ref, zc_ref, kl_ref, vl_ref, gpl_ref, gptl_ref, ql_ref, zl_ref,
         wq_ref, wk_ref, wv_ref, norm_ref, yc_ref, yl_ref) = refs[:n_in + n_out]
    else:
        (kc_ref, vc_ref, gpc_ref, gptc_ref, kl_ref, vl_ref, gpl_ref, gptl_ref, ql_ref, zl_ref,
         wq_ref, wk_ref, wv_ref, norm_ref, yl_ref) = refs[:n_in + n_out]
        qc_ref = zc_ref = yc_ref = None
    ks, vs, qs, gpa, gpta, o_f, o_b = refs[n_in + n_out:n_in + n_out + 7]
    per_dir = refs[n_in + n_out + 7:]
    u_s, w_s, kT_s, qd_s, qk_s = (per_dir[0:2], per_dir[2:4], per_dir[4:6], per_dir[6:8], per_dir[8:10])
    L = GD_CHUNK
    SB = GD_SUPER
    tc = n_ctx * SB
    tiles = SB // ML_CHUNK

    ks[0:tc, :] = _l2norm(_conv_silu(kc_ref[...], wk_ref[...]))
    vs[0:tc, :] = _conv_silu(vc_ref[...], wv_ref[...])
    ks[tc:, :] = _l2norm(_conv_silu(kl_ref[...], wk_ref[...]))
    vs[tc:, :] = _conv_silu(vl_ref[...], wv_ref[...])
    qs[tc:, :] = _l2norm(_conv_silu(ql_ref[...], wq_ref[...])) * (HEAD_DIM ** -0.5)
    if ctx_out:
        qs[0:tc, :] = _l2norm(_conv_silu(qc_ref[...], wq_ref[...])) * (HEAD_DIM ** -0.5)
    gpa[0:tc, :] = gpc_ref[...]
    gpa[tc:, :] = gpl_ref[...]
    gpta[0:n_ctx * tiles] = gptc_ref[...]
    gpta[n_ctx * tiles:] = gptl_ref[...]

    def prepare(j, with_out):
        rows = pl.ds(pl.multiple_of(j * SB, SB), SB)
        gpt = jnp.concatenate([gpta[j * tiles + i] for i in range(tiles)], axis=1)
        res = _gdn_prepare(ks[rows, :], vs[rows, :], qs[rows, :] if with_out else None, gpa[rows, :], gpt)
        for d, (u, w, kT, qd, qk) in enumerate(res):
            u_s[d][rows, :] = u
            w_s[d][rows, :] = w
            kT_s[d][j] = kT
            if with_out:
                qd_s[d][rows, :] = qd
                qk_s[d][rows, :] = qk

    def prepare_loop(lo, n, with_out):
        def body(j, carry):
            prepare(j, with_out)
            return carry
        lax.fori_loop(lo, lo + n, body, 0)

    prepare_loop(0, n_ctx, ctx_out)
    prepare_loop(n_ctx, n_lat, True)

    def step(d, j, c, S, with_out):
        row0 = pl.multiple_of(j * SB + c * L, L)
        rows = pl.ds(row0, L)
        cols = slice(c * L, (c + 1) * L)
        edge = gpa[pl.ds(row0 + (0 if d else L - 1), 1), :]
        g_end = jnp.exp(edge[:, 7:8] if d else edge[:, 6:7])
        Sb = S.astype(BF16)
        v_new = (u_s[d][rows, :] - jnp.dot(w_s[d][rows, :], Sb, preferred_element_type=F32)).astype(BF16)
        if with_out:
            out = (jnp.dot(qd_s[d][rows, :], Sb, preferred_element_type=F32)
                   + jnp.dot(qk_s[d][rows, cols], v_new, preferred_element_type=F32))
            (o_b if d else o_f)[rows, :] = out
        return S * g_end + jnp.dot(kT_s[d][j, :, cols], v_new, preferred_element_type=F32)

    def scan(lo, n, S_f, S_b, with_out):
        def body(i, carry):
            S_f, S_b = carry
            for c in range(SB // L):
                S_f = step(0, lo + i, c, S_f, with_out)
                S_b = step(1, lo + n - 1 - i, SB // L - 1 - c, S_b, with_out)
            return S_f, S_b
        return lax.fori_loop(0, n, body, (S_f, S_b))

    zero = jnp.zeros((HEAD_DIM, HEAD_DIM), F32)
    S_f, S_b = scan(0, n_ctx, zero, zero, ctx_out)
    scan(n_ctx, n_lat, S_f, S_b, True)

    def finish(rows, z_ref, y_ref):
        y_ref[...] = ((_rms(o_f[rows, :] + o_b[rows, :]) * norm_ref[...]) * _silu(z_ref[...])).astype(y_ref.dtype)

    finish(slice(tc, None), zl_ref, yl_ref)
    if ctx_out:
        finish(slice(0, tc), zc_ref, yc_ref)


def _gdn(zc, gpc, gptc, zl, gpl, gptl, conv, norm, ctx_out):
    nb, tc, _ = zc.shape
    tl = zl.shape[1]
    ctx_in = [zc, zc, gpc, gptc] + ([zc, zc] if ctx_out else [])
    ctx_specs = [_seq_spec(tc, C_GD_K), _seq_spec(tc, C_GD_V)] + _gp_specs(tc) + (
        [_seq_spec(tc, C_GD_Q), _seq_spec(tc, C_GD_Z)] if ctx_out else [])
    lat_in = [zl, zl, gpl, gptl, zl, zl]
    lat_specs = [_seq_spec(tl, C_GD_K), _seq_spec(tl, C_GD_V)] + _gp_specs(tl) + [
        _seq_spec(tl, C_GD_Q), _seq_spec(tl, C_GD_Z)]
    conv_spec = lambda off: pl.BlockSpec((3, HEAD_DIM), lambda b, h: (0, off // HEAD_DIM + h))
    out_spec = lambda rows: pl.BlockSpec((None, rows, HEAD_DIM), lambda b, h: (b, 0, h))
    out_shape = lambda rows: jax.ShapeDtypeStruct((nb, rows, BRANCH_W), BF16)
    rows = tc + tl
    seq = lambda dt: pltpu.VMEM((rows, HEAD_DIM), dt)
    two = lambda spec: [spec, spec]
    scratch = ([seq(F32), seq(F32), seq(F32), seq(F32), pltpu.VMEM((rows // ML_CHUNK, N_GATE, ML_CHUNK), F32),
                seq(F32), seq(F32)]
               + two(seq(F32)) + two(seq(BF16)) + two(pltpu.VMEM((rows // GD_SUPER, HEAD_DIM, GD_SUPER), BF16))
               + two(seq(BF16)) + two(pltpu.VMEM((rows, GD_SUPER), BF16)))
    res = pl.pallas_call(
        functools.partial(_gdn_kernel, ctx_out=ctx_out, n_ctx=tc // GD_SUPER, n_lat=tl // GD_SUPER),
        grid=(nb, N_HEADS),
        in_specs=ctx_specs + lat_specs + [conv_spec(0), conv_spec(BRANCH_W), conv_spec(2 * BRANCH_W),
                                          pl.BlockSpec((1, HEAD_DIM), lambda b, h: (0, 0))],
        out_specs=([out_spec(tc)] if ctx_out else []) + [out_spec(tl)],
        out_shape=([out_shape(tc)] if ctx_out else []) + [out_shape(tl)],
        scratch_shapes=scratch,
        compiler_params=_params(("parallel", "parallel")),
        name="gdn",
    )(*ctx_in, *lat_in, conv, conv, conv, norm)
    return (res[0], res[1]) if ctx_out else (None, res[0])


def _rope(x, cos, sin):
    lane = lax.broadcasted_iota(jnp.int32, x.shape, 1)
    quarter = HEAD_DIM // 4
    partner = jnp.where((lane & (2 * quarter - 1)) < quarter,
                        pltpu.roll(x, HEAD_DIM - quarter, 1), pltpu.roll(x, quarter, 1))
    return x * cos + partner * sin


def _attn_kernel(*refs, latent, tc):
    if latent:
        (q_ref, kc_ref, vc_ref, kl_ref, vl_ref, cos_ref, sin_ref, cosq_ref, sinq_ref, qn_ref, kn_ref,
         o_ref, k_scr, v_scr) = refs
    else:
        q_ref, kc_ref, vc_ref, qn_ref, kn_ref, o_ref, k_scr, v_scr = refs

    @pl.when(pl.program_id(2) == 0)
    def _():
        k_scr[0:tc, :] = (_rms(kc_ref[...]) * kn_ref[...]).astype(BF16)
        v_scr[0:tc, :] = vc_ref[...].astype(BF16)
        if latent:
            kl = _rope(_rms(kl_ref[...]) * kn_ref[...], cos_ref[...], sin_ref[...])
            k_scr[tc:, :] = kl.astype(BF16)
            v_scr[tc:, :] = vl_ref[...].astype(BF16)

    scale = HEAD_DIM ** -0.5
    for g in range(AT_Q_HEADS // AT_KV_HEADS):
        cols = slice(g * HEAD_DIM, (g + 1) * HEAD_DIM)
        q = _rms(q_ref[:, cols]) * qn_ref[...]
        if latent:
            q = _rope(q, cosq_ref[...], sinq_ref[...])
        s = lax.dot_general((q * scale).astype(BF16), k_scr[...], (((1,), (1,)), ((), ())),
                            preferred_element_type=F32)
        p = jnp.exp(s - jnp.max(s, axis=-1, keepdims=True))
        l = jnp.sum(p, axis=-1, keepdims=True)
        o_ref[:, cols] = (jnp.dot(p.astype(BF16), v_scr[...], preferred_element_type=F32) / l).astype(o_ref.dtype)


def _attention(zq, zc, zl, cos, sin, qn, kn):
    latent = zl is not None
    nb, tq_total, _ = zq.shape
    tc = zc.shape[1]
    tk = tc + (zl.shape[1] if latent else 0)
    gq = (AT_Q_HEADS // AT_KV_HEADS) * HEAD_DIM
    head_spec = lambda rows, col: pl.BlockSpec((None, rows, HEAD_DIM), lambda b, h, i: (b, 0, col // HEAD_DIM + h))
    vec_spec = pl.BlockSpec((1, HEAD_DIM), lambda b, h, i: (0, 0))
    in_specs = [pl.BlockSpec((None, TQ_ATTN, gq), lambda b, h, i: (b, i, C_AT_Q // gq + h)),
                head_spec(tc, C_AT_K), head_spec(tc, C_AT_V)]
    args = [zq, zc, zc]
    if latent:
        tl = zl.shape[1]
        in_specs += [head_spec(tl, C_AT_K), head_spec(tl, C_AT_V),
                     pl.BlockSpec((tl, HEAD_DIM), lambda b, h, i: (0, 0)),
                     pl.BlockSpec((tl, HEAD_DIM), lambda b, h, i: (0, 0)),
                     pl.BlockSpec((TQ_ATTN, HEAD_DIM), lambda b, h, i: (i, 0)),
                     pl.BlockSpec((TQ_ATTN, HEAD_DIM), lambda b, h, i: (i, 0))]
        args += [zl, zl, cos, sin, cos, sin]
    return pl.pallas_call(
        functools.partial(_attn_kernel, latent=latent, tc=tc),
        grid=(nb, AT_KV_HEADS, tq_total // TQ_ATTN),
        in_specs=in_specs + [vec_spec, vec_spec],
        out_specs=pl.BlockSpec((None, TQ_ATTN, gq), lambda b, h, i: (b, i, h)),
        out_shape=jax.ShapeDtypeStruct((nb, tq_total, BRANCH_W), BF16),
        scratch_shapes=[pltpu.VMEM((tk, HEAD_DIM), BF16), pltpu.VMEM((tk, HEAD_DIM), BF16)],
        compiler_params=_params(("parallel", "parallel", "arbitrary")),
        name="attention",
    )(*args, qn, kn)


def _merge_kernel(h_ref, xcol_ref, modcol_ref, y0_ref, y1_ref, y2_ref, y3_ref,
                  wg0_ref, wg1_ref, wg2_ref, wg3_ref, wb_ref, wo_ref, o_ref, m_scr):
    s = pl.program_id(2)
    n_gate = D_MODEL // TN_MERGE
    ys = (y0_ref, y1_ref, y2_ref, y3_ref)
    wgs = (wg0_ref, wg1_ref, wg2_ref, wg3_ref)

    @pl.when(s < n_gate)
    def _():
        h = h_ref[...]
        acc = jnp.zeros((TM_MERGE, TN_MERGE), F32)
        for n in range(N_BRANCH):
            gate = jnp.dot(h, wgs[n][...], preferred_element_type=F32)
            acc = acc + _sigmoid(gate) * jnp.dot(ys[n][...], wb_ref[n], preferred_element_type=F32)
        m_scr[s] = acc.astype(BF16)

    @pl.when(s >= n_gate)
    def _():
        acc = jnp.zeros((TM_MERGE, TN_OUT), F32)
        for kk in range(n_gate):
            acc = acc + jnp.dot(m_scr[kk], wo_ref[kk * TN_MERGE:(kk + 1) * TN_MERGE, :],
                                preferred_element_type=F32)
        o_ref[...] = xcol_ref[...] + modcol_ref[5:6, :] * acc


def _merge(x, h, mod, ys, w_gate, w_branch, w_out, mod_row):
    groups, rows, _ = x.shape
    n_gate = D_MODEL // TN_MERGE
    n_out = D_MODEL // TN_OUT
    mrow = (lambda b: b) if mod_row is None else (lambda b: mod_row)
    gate_col = lambda s: jnp.minimum(s, n_gate - 1)
    out_col = lambda s: jnp.maximum(s - n_gate, 0)
    y_spec = pl.BlockSpec((None, TM_MERGE, BRANCH_W), lambda b, i, s: (b, i, 0))
    wg_spec = lambda n: pl.BlockSpec((D_MODEL, TN_MERGE), lambda b, i, s: (0, n * n_gate + gate_col(s)))
    return pl.pallas_call(
        _merge_kernel,
        grid=(groups, rows // TM_MERGE, n_gate + n_out),
        in_specs=[pl.BlockSpec((None, TM_MERGE, D_MODEL), lambda b, i, s: (b, i, 0)),
                  pl.BlockSpec((None, TM_MERGE, TN_OUT), lambda b, i, s: (b, i, out_col(s))),
                  pl.BlockSpec((None, N_MOD, TN_OUT), lambda b, i, s: (mrow(b), 0, out_col(s))),
                  y_spec, y_spec, y_spec, y_spec,
                  wg_spec(0), wg_spec(1), wg_spec(2), wg_spec(3),
                  pl.BlockSpec((N_BRANCH, BRANCH_W, TN_MERGE), lambda b, i, s: (0, 0, gate_col(s))),
                  pl.BlockSpec((D_MODEL, TN_OUT), lambda b, i, s: (0, out_col(s)))],
        out_specs=pl.BlockSpec((None, TM_MERGE, TN_OUT), lambda b, i, s: (b, i, out_col(s))),
        out_shape=jax.ShapeDtypeStruct(x.shape, F32),
        scratch_shapes=[pltpu.VMEM((n_gate, TM_MERGE, TN_MERGE), BF16)],
        compiler_params=_params(("parallel", "parallel", "arbitrary")),
        name="merge",
    )(h, x, mod, *ys, w_gate, w_gate, w_gate, w_gate, w_branch, w_out)


def _rope_tables(seq):
    n = HEAD_DIM // 4
    inv = ROPE_THETA ** (-jnp.arange(n, dtype=F32) / n)
    pos = jnp.arange(seq)
    ar = (pos // GRID_W).astype(F32)[:, None] * inv
    ac = (pos % GRID_W).astype(F32)[:, None] * inv
    cos = jnp.concatenate([jnp.cos(ar), jnp.cos(ar), jnp.cos(ac), jnp.cos(ac)], axis=1)
    sin = jnp.concatenate([-jnp.sin(ar), jnp.sin(ar), -jnp.sin(ac), jnp.sin(ac)], axis=1)
    return cos, sin


def _gate_params(ml_if_bias, gd_a_log, gd_dt_bias):
    zeros = jnp.zeros((2, N_HEADS), F32)
    bias = jnp.concatenate([ml_if_bias.reshape(4, N_HEADS), zeros, gd_dt_bias], axis=0).T
    alog = jnp.concatenate([jnp.zeros((6, N_HEADS), F32), gd_a_log], axis=0).T
    pad = lambda a: jnp.pad(a, ((0, 0), (0, LANES - N_GATE))).reshape(1, N_HEADS * LANES)
    col = lambda a: a.reshape(N_HEADS * N_GATE, 1)
    return pad(bias), pad(alog), col(bias), col(alog)


def _gate_weights(w_in):
    g = jnp.concatenate([w_in[:, R_ML_IF:R_ML_IF + 16], w_in[:, R_GD_BA:R_GD_BA + 16]], axis=1)
    g = g.reshape(D_MODEL, N_GATE, N_HEADS).transpose(0, 2, 1)
    w_gc = jnp.pad(g, ((0, 0), (0, 0), (0, LANES - N_GATE))).reshape(D_MODEL, N_HEADS * LANES)
    w_gr = g.reshape(D_MODEL, N_HEADS * N_GATE).T
    return w_gc.astype(BF16), w_gr.astype(BF16)


def kernel(x, c, ctx, c_ctx, mod_w, mod_b, ffn1_norm, ffn1_w_in, ffn1_w_out, mix_norm, w_in, sg_norm, sg_w, sg_b, ml_if_bias, ml_norm, gd_conv, gd_a_log, gd_dt_bias, gd_norm, at_q_norm, at_k_norm, w_branch, w_out, ffn2_norm, ffn2_w_in, ffn2_w_out):
    nb, seq, _ = x.shape
    tc = ctx.shape[1]
    depth = mod_w.shape[0]
    ctx_row = nb
    cos, sin = _rope_tables(seq)

    cc = jnp.zeros((MOD_ROWS, D_MODEL), F32).at[:nb].set(c).at[ctx_row].set(c_ctx)
    mods = _modulation(cc, mod_w, mod_b).reshape(depth, MOD_ROWS, N_MOD, D_MODEL)

    flat = lambda a: a.reshape(1, nb * tc, a.shape[-1])
    unflat = lambda a: a.reshape(nb, tc, a.shape[-1])
    xc = ctx
    for l in range(depth):
        last = l == depth - 1
        ctx_out = not last
        mod = mods[l]
        wl = w_in[l]
        w_main = jnp.concatenate([wl[:, :R_ML_IF], wl[:, R_GD_KV:R_GD_BA], wl[:, R_AT_KV:R_GATE]], axis=1).astype(BF16)
        w_gate = wl[:, R_GATE:].astype(BF16)
        w_gc, w_gr = _gate_weights(wl)
        bias_c, alog_c, bias_r, alog_r = _gate_params(ml_if_bias[l], gd_a_log[l], gd_dt_bias[l])
        f1_in, f1_out = ffn1_w_in[l].astype(BF16), ffn1_w_out[l].astype(BF16)
        f2_in, f2_out = ffn2_w_in[l].astype(BF16), ffn2_w_out[l].astype(BF16)
        wb, wo = w_branch[l].astype(BF16), w_out[l].astype(BF16)
        sgw = sg_w[l].astype(BF16)
        sg_bias = jnp.repeat(sg_b[l].T, BRANCH_W // SG_GROUPS, axis=1)
        sgn = sg_norm[l].reshape(1, BRANCH_W)
        mln = ml_norm[l].reshape(1, BRANCH_W)
        gdn_g = gd_norm[l].reshape(1, HEAD_DIM)
        qn, kn = at_q_norm[l].reshape(1, HEAD_DIM), at_k_norm[l].reshape(1, HEAD_DIM)

        x = _ffn(x, mod, ffn1_norm[l], f1_in, f1_out, 0, None)
        xc = unflat(_ffn(flat(xc), mod, ffn1_norm[l], f1_in, f1_out, 0, ctx_row))

        zl, gcl, grl, hl = _inproj(x, mod, mix_norm[l], w_main, w_gc, w_gr, N_MAIN, None)
        zc, gcc, grc, hc = _inproj(flat(xc), mod, mix_norm[l], w_main, w_gc, w_gr,
                                   N_MAIN if ctx_out else KV_MAIN, ctx_row)
        zc = unflat(zc)
        gcc = unflat(gcc)
        grc = grc.reshape(N_HEADS * N_GATE, nb, tc).transpose(1, 0, 2)
        gpl, gptl = _gateprep(gcl, grl, bias_c, alog_c, bias_r, alog_r)
        gpc, gptc = _gateprep(gcc, grc, bias_c, alog_c, bias_r, alog_r)

        y_sg_l = _spatial_gating(zl, sgn, sgw, sg_bias)
        y_ml_c, y_ml_l = _mlstm(zc, gpc, gptc, zl, gpl, gptl, mln, ctx_out)
        y_gd_c, y_gd_l = _gdn(zc, gpc, gptc, zl, gpl, gptl, gd_conv[l], gdn_g, ctx_out)
        y_at_l = _attention(zl, zc, zl, cos, sin, qn, kn)
        x = _merge(x, hl, mod, [y_sg_l, y_ml_l, y_gd_l, y_at_l], w_gate, wb, wo, None)
        if ctx_out:
            y_sg_c = _spatial_gating(zc, sgn, sgw, sg_bias)
            y_at_c = _attention(zc, zc, None, None, None, qn, kn)
            xc = unflat(_merge(flat(xc), hc, mod, [flat(y) for y in (y_sg_c, y_ml_c, y_gd_c, y_at_c)],
                               w_gate, wb, wo, ctx_row))

        x = _ffn(x, mod, ffn2_norm[l], f2_in, f2_out, 6, None)
        if ctx_out:
            xc = unflat(_ffn(flat(xc), mod, ffn2_norm[l], f2_in, f2_out, 6, ctx_row))
    return x
```

```python
import functools
import math

import jax
import jax.numpy as jnp
from jax import lax
from jax.experimental import pallas as pl
from jax.experimental.pallas import tpu as pltpu

F32 = jnp.float32
BF16 = jnp.bfloat16
EPS = 1e-6

D_MODEL = 2048
D_FF = 5632
N_MOD = 9
N_BRANCH = 4
BRANCH_W = 512
HEAD_DIM = 128
N_HEADS = 4
SG_GROUPS = 4
ML_CHUNK = 128
GD_CHUNK = 64
GD_SUPER = 256
GD_PREP_GROUP = 2
AT_Q_HEADS = 4
AT_KV_HEADS = 2
GRID_W = 64
ROPE_THETA = 10000.0
FFN_RESIDUAL = 0.5

LANES = 128
N_GATE = 8
MOD_ROWS = 16

C_ML_K, C_ML_V = 0, 512
C_GD_K, C_GD_V = 1024, 1536
C_AT_K, C_AT_V = 2048, 2304
KV_MAIN = 2560
C_SG_U, C_SG_V = 2560, 3072
C_ML_Q, C_ML_O = 3584, 4096
C_GD_Q, C_GD_Z = 4608, 5120
C_AT_Q = 5632
N_MAIN = 6144
R_ML_IF = 1024
R_GD_KV = 1040
R_GD_BA = 2064
R_AT_KV = 2080
R_REST = 2592
R_GATE = 6176

VMEM_LIMIT = 56 * 1024 * 1024

TM_FFN = 512
TF_FFN = 512
TM_PROJ = 1024
TN_PROJ = 512
TM_MERGE = 1024
TN_MERGE = 256
TN_OUT = 512
TQ_ATTN = 256
TR_SG = 256
TG_GATE = 256
CAST_BLOCK_BYTES = 8 * 1024 * 1024
TN_MOD = 2048


def _params(sem):
    return pltpu.CompilerParams(dimension_semantics=sem, vmem_limit_bytes=VMEM_LIMIT)


def _mm(a, b):
    return jnp.dot(a.astype(BF16), b.astype(BF16), preferred_element_type=F32)


def _mm_nt(a, b):
    return lax.dot_general(a.astype(BF16), b.astype(BF16), (((1,), (1,)), ((), ())),
                           preferred_element_type=F32)


def _mm_tn(a, b):
    return lax.dot_general(a.astype(BF16), b.astype(BF16), (((0,), (0,)), ((), ())),
                           preferred_element_type=F32)


def _sigmoid(x):
    return 1.0 / (1.0 + jnp.exp(-x))


def _silu(x):
    return x * _sigmoid(x)


def _softplus(x):
    return jnp.maximum(x, 0.0) + jnp.log1p(jnp.exp(-jnp.abs(x)))


def _gelu_tanh(x):
    c = math.sqrt(2.0 / math.pi)
    return x * (0.5 * (1.0 + jnp.tanh(c * (x + 0.044715 * (x * x * x)))))


def _rms(x):
    return x * lax.rsqrt(jnp.mean(x * x, axis=-1, keepdims=True) + EPS)


def _norm_mod(x, gain, shift, scale):
    return (_rms(x) * gain) * (1.0 + scale) + shift


def _cast_kernel(x_ref, o_ref):
    o_ref[...] = x_ref[...].astype(o_ref.dtype)


def _to_bf16(w):
    depth, r, c = w.shape
    rows = 1 << (max(16, min(r, CAST_BLOCK_BYTES // (4 * c))).bit_length() - 1)
    while r % rows:
        rows //= 2
    return pl.pallas_call(
        _cast_kernel,
        grid=(depth, r // rows),
        in_specs=[pl.BlockSpec((None, rows, c), lambda l, i: (l, i, 0))],
        out_specs=pl.BlockSpec((None, rows, c), lambda l, i: (l, i, 0)),
        out_shape=jax.ShapeDtypeStruct(w.shape, BF16),
        compiler_params=_params(("parallel", "parallel")),
        name="cast_bf16",
    )(w)


def _mod_kernel(cc_ref, w_ref, b_ref, o_ref):
    o_ref[...] = _mm(_silu(cc_ref[...]), w_ref[...]) + b_ref[...]


def _modulation(cc, mod_w, mod_b):
    depth = mod_w.shape[0]
    n = mod_w.shape[2]
    return pl.pallas_call(
        _mod_kernel,
        grid=(depth, n // TN_MOD),
        in_specs=[pl.BlockSpec((MOD_ROWS, D_MODEL), lambda l, j: (0, 0)),
                  pl.BlockSpec((None, D_MODEL, TN_MOD), lambda l, j: (l, 0, j)),
                  pl.BlockSpec((None, 1, TN_MOD), lambda l, j: (l, 0, j))],
        out_specs=pl.BlockSpec((None, MOD_ROWS, TN_MOD), lambda l, j: (l, 0, j)),
        out_shape=jax.ShapeDtypeStruct((depth, MOD_ROWS, n), F32),
        compiler_params=_params(("parallel", "arbitrary")),
        name="modulation",
    )(cc, mod_w, mod_b.reshape(depth, 1, n))


def _ffn_kernel(x_ref, mod_ref, g_ref, wa_ref, wb_ref, wo_ref, o_ref, h_scr, acc_scr, *, base):
    j = pl.program_id(2)

    @pl.when(j == 0)
    def _():
        h = _norm_mod(x_ref[...], g_ref[...], mod_ref[base:base + 1, :], mod_ref[base + 1:base + 2, :])
        h_scr[...] = h.astype(BF16)
        acc_scr[...] = jnp.zeros_like(acc_scr)

    h = h_scr[...]
    a = jnp.dot(h, wa_ref[...], preferred_element_type=F32)
    b = jnp.dot(h, wb_ref[...], preferred_element_type=F32)
    acc_scr[...] += _mm(_silu(a) * b, wo_ref[...])

    @pl.when(j == pl.num_programs(2) - 1)
    def _():
        o_ref[...] = x_ref[...] + (FFN_RESIDUAL * mod_ref[base + 2:base + 3, :]) * acc_scr[...]


def _ffn(x, mod, gain, w_in, w_out, base, mod_row):
    groups, rows, _ = x.shape
    nj = D_FF // TF_FFN
    mod_idx = (lambda b, i, j: (b, 0, 0)) if mod_row is None else (lambda b, i, j: (mod_row, 0, 0))
    return pl.pallas_call(
        functools.partial(_ffn_kernel, base=base),
        grid=(groups, rows // TM_FFN, nj),
        in_specs=[pl.BlockSpec((None, TM_FFN, D_MODEL), lambda b, i, j: (b, i, 0)),
                  pl.BlockSpec((None, N_MOD, D_MODEL), mod_idx),
                  pl.BlockSpec((1, D_MODEL), lambda b, i, j: (0, 0)),
                  pl.BlockSpec((D_MODEL, TF_FFN), lambda b, i, j: (0, j)),
                  pl.BlockSpec((D_MODEL, TF_FFN), lambda b, i, j: (0, j + nj)),
                  pl.BlockSpec((TF_FFN, D_MODEL), lambda b, i, j: (j, 0))],
        out_specs=pl.BlockSpec((None, TM_FFN, D_MODEL), lambda b, i, j: (b, i, 0)),
        out_shape=jax.ShapeDtypeStruct(x.shape, F32),
        scratch_shapes=[pltpu.VMEM((TM_FFN, D_MODEL), BF16), pltpu.VMEM((TM_FFN, D_MODEL), F32)],
        compiler_params=_params(("parallel", "parallel", "arbitrary")),
        name="ffn",
    )(x, mod, gain.reshape(1, D_MODEL), w_in, w_in, w_out)


def _inproj_kernel(x_ref, mod_ref, g_ref, w_ref, wgc_ref, wgr_ref, z_ref, gc_ref, gr_ref, h_ref):
    n = pl.program_id(2)

    @pl.when(n == 0)
    def _():
        h = _norm_mod(x_ref[...], g_ref[...], mod_ref[3:4, :], mod_ref[4:5, :]).astype(BF16)
        h_ref[...] = h
        gc_ref[...] = jnp.dot(h, wgc_ref[...], preferred_element_type=F32)
        gr_ref[...] = lax.dot_general(wgr_ref[...], h, (((1,), (1,)), ((), ())), preferred_element_type=F32)

    z_ref[...] = jnp.dot(h_ref[...], w_ref[...], preferred_element_type=F32)


def _inproj(x, mod, gain, w_main, w_gc, w_gr, n_cols, mod_row):
    groups, rows, _ = x.shape
    mod_idx = (lambda b, i, n: (b, 0, 0)) if mod_row is None else (lambda b, i, n: (mod_row, 0, 0))
    return pl.pallas_call(
        _inproj_kernel,
        grid=(groups, rows // TM_PROJ, n_cols // TN_PROJ),
        in_specs=[pl.BlockSpec((None, TM_PROJ, D_MODEL), lambda b, i, n: (b, i, 0)),
                  pl.BlockSpec((None, N_MOD, D_MODEL), mod_idx),
                  pl.BlockSpec((1, D_MODEL), lambda b, i, n: (0, 0)),
                  pl.BlockSpec((D_MODEL, TN_PROJ), lambda b, i, n: (0, n)),
                  pl.BlockSpec((D_MODEL, N_HEADS * LANES), lambda b, i, n: (0, 0)),
                  pl.BlockSpec((N_HEADS * N_GATE, D_MODEL), lambda b, i, n: (0, 0))],
        out_specs=[pl.BlockSpec((None, TM_PROJ, TN_PROJ), lambda b, i, n: (b, i, n)),
                   pl.BlockSpec((None, TM_PROJ, N_HEADS * LANES), lambda b, i, n: (b, i, 0)),
                   pl.BlockSpec((None, N_HEADS * N_GATE, TM_PROJ), lambda b, i, n: (b, 0, i)),
                   pl.BlockSpec((None, TM_PROJ, D_MODEL), lambda b, i, n: (b, i, 0))],
        out_shape=[jax.ShapeDtypeStruct((groups, rows, n_cols), F32),
                   jax.ShapeDtypeStruct((groups, rows, N_HEADS * LANES), F32),
                   jax.ShapeDtypeStruct((groups, N_HEADS * N_GATE, rows), F32),
                   jax.ShapeDtypeStruct((groups, rows, D_MODEL), BF16)],
        compiler_params=_params(("parallel", "parallel", "arbitrary")),
        name="inproj",
    )(x, mod, gain.reshape(1, D_MODEL), w_main, w_gc, w_gr)


def _split3(e):
    e1 = e.astype(BF16)
    r1 = e - e1.astype(F32)
    e2 = r1.astype(BF16)
    e3 = (r1 - e2.astype(F32)).astype(BF16)
    return e1, e2, e3


def _gate_values(pre, a_log, cls):
    neg_sp = -_softplus(-pre)
    return jnp.where(cls < 2, pre,
                     jnp.where(cls < 4, neg_sp,
                               jnp.where(cls < 6, _sigmoid(pre), -jnp.exp(a_log) * _softplus(pre))))


def _gateprep_kernel(gc_ref, gr_ref, bc_ref, ac_ref, br_ref, ar_ref, oc_ref, or_ref):
    for i in range(TG_GATE // ML_CHUNK):
        tile = slice(i * ML_CHUNK, (i + 1) * ML_CHUNK)
        oc_ref[tile, :], or_ref[i] = _gateprep_tile(gc_ref[tile, :], gr_ref[:, tile], bc_ref[...], ac_ref[...],
                                                    br_ref[...], ar_ref[...])


def _gateprep_tile(gc, gr, bias_c, alog_c, bias_r, alog_r):
    L = ML_CHUNK
    half = GD_CHUNK
    pre = gc + bias_c
    cls = lax.broadcasted_iota(jnp.int32, pre.shape, 1) & (LANES - 1)
    t = lax.broadcasted_iota(jnp.int32, pre.shape, 0)
    e = _gate_values(pre, alog_c, cls)
    tri = (lax.broadcasted_iota(jnp.int32, (L, L), 1) <= lax.broadcasted_iota(jnp.int32, (L, L), 0)).astype(BF16)
    e1, e2, e3 = _split3(e)
    p = (jnp.dot(tri, e1, preferred_element_type=F32) + jnp.dot(tri, e2, preferred_element_type=F32)
         + jnp.dot(tri, e3, preferred_element_type=F32))
    tot = p[L - 1:L, :]
    mid = p[half - 1:half, :]
    second = t >= half
    pre64 = p - jnp.where(second, mid, 0.0)
    suf64 = jnp.where(second, tot, mid) - p + e
    cum = jnp.where(cls == 2, p, jnp.where(cls == 3, tot - p + e, jnp.where(cls == 6, pre64, suf64)))
    out_c = jnp.where((cls == 2) | (cls == 3) | (cls == 6) | (cls == 7), cum, e)

    pre_r = gr + bias_r
    cls_r = lax.broadcasted_iota(jnp.int32, pre_r.shape, 0) & (N_GATE - 1)
    t_r = lax.broadcasted_iota(jnp.int32, pre_r.shape, 1)
    e_r = _gate_values(pre_r, alog_r, cls_r)
    tri_r = (lax.broadcasted_iota(jnp.int32, (L, L), 0) <= lax.broadcasted_iota(jnp.int32, (L, L), 1)).astype(BF16)
    f1, f2, f3 = _split3(e_r)
    pr = (jnp.dot(f1, tri_r, preferred_element_type=F32) + jnp.dot(f2, tri_r, preferred_element_type=F32)
          + jnp.dot(f3, tri_r, preferred_element_type=F32))
    tot_r = pr[:, L - 1:L]
    mid_r = pr[:, half - 1:half]
    second_r = t_r >= half
    pre64_r = pr - jnp.where(second_r, mid_r, 0.0)
    suf64_r = jnp.where(second_r, tot_r, mid_r) - pr + e_r
    cum_r = jnp.where(cls_r == 2, pr, jnp.where(cls_r == 3, tot_r - pr + e_r, jnp.where(cls_r == 6, pre64_r, suf64_r)))
    return out_c, jnp.where((cls_r == 2) | (cls_r == 3) | (cls_r == 6) | (cls_r == 7), cum_r, e_r)


def _gateprep(gc, gr, bias_c, alog_c, bias_r, alog_r):
    groups, rows, _ = gc.shape
    nt = rows // ML_CHUNK
    hw = N_HEADS * LANES
    hg = N_HEADS * N_GATE
    tiles = TG_GATE // ML_CHUNK
    return pl.pallas_call(
        _gateprep_kernel,
        grid=(groups, rows // TG_GATE),
        in_specs=[pl.BlockSpec((None, TG_GATE, hw), lambda b, c: (b, c, 0)),
                  pl.BlockSpec((None, hg, TG_GATE), lambda b, c: (b, 0, c)),
                  pl.BlockSpec((1, hw), lambda b, c: (0, 0)),
                  pl.BlockSpec((1, hw), lambda b, c: (0, 0)),
                  pl.BlockSpec((hg, 1), lambda b, c: (0, 0)),
                  pl.BlockSpec((hg, 1), lambda b, c: (0, 0))],
        out_specs=[pl.BlockSpec((None, TG_GATE, hw), lambda b, c: (b, c, 0)),
                   pl.BlockSpec((None, tiles, hg, ML_CHUNK), lambda b, c: (b, c, 0, 0))],
        out_shape=[jax.ShapeDtypeStruct((groups, rows, hw), F32),
                   jax.ShapeDtypeStruct((groups, nt, hg, ML_CHUNK), F32)],
        compiler_params=_params(("parallel", "parallel")),
        name="gateprep",
    )(gc, gr, bias_c, alog_c, bias_r, alog_r)


def _sg_kernel(u_ref, v_ref, gn_ref, w_ref, bias_ref, o_ref):
    gw = BRANCH_W // SG_GROUPS
    for ch in range(TR_SG // ML_CHUNK):
        rows = slice(ch * ML_CHUNK, (ch + 1) * ML_CHUNK)
        v = _rms(_gelu_tanh(v_ref[rows, :])) * gn_ref[...]
        u = _gelu_tanh(u_ref[rows, :])
        for g in range(SG_GROUPS):
            cols = slice(g * gw, (g + 1) * gw)
            s = _mm(w_ref[g], v[:, cols]) + bias_ref[:, cols]
            o_ref[rows, cols] = (u[:, cols] * s).astype(o_ref.dtype)


def _spatial_gating(z, gn, w_s, bias_full):
    groups, rows, _ = z.shape
    ub, vb = C_SG_U // BRANCH_W, C_SG_V // BRANCH_W
    return pl.pallas_call(
        _sg_kernel,
        grid=(groups, rows // TR_SG),
        in_specs=[pl.BlockSpec((None, TR_SG, BRANCH_W), lambda b, i: (b, i, ub)),
                  pl.BlockSpec((None, TR_SG, BRANCH_W), lambda b, i: (b, i, vb)),
                  pl.BlockSpec((1, BRANCH_W), lambda b, i: (0, 0)),
                  pl.BlockSpec((SG_GROUPS, ML_CHUNK, ML_CHUNK), lambda b, i: (0, 0, 0)),
                  pl.BlockSpec((ML_CHUNK, BRANCH_W), lambda b, i: (0, 0))],
        out_specs=pl.BlockSpec((None, TR_SG, BRANCH_W), lambda b, i: (b, i, 0)),
        out_shape=jax.ShapeDtypeStruct((groups, rows, BRANCH_W), BF16),
        compiler_params=_params(("parallel", "parallel")),
        name="spatial_gating",
    )(z, z, gn, w_s, bias_full)


def _mlstm_chunk(q, k, v, gp, gpt, state, rev, with_out):
    L = ML_CHUNK
    j_i, j_b = (1, 3) if rev else (0, 2)
    li_c = gp[:, j_i:j_i + 1]
    b_c = gp[:, j_b:j_b + 1]
    b_end = b_c[0:1, :] if rev else b_c[L - 1:L, :]
    ks = k * (HEAD_DIM ** -0.5)
    a = b_end - b_c + li_c
    m_loc = jnp.max(a, axis=0, keepdims=True)
    wgt = jnp.broadcast_to(jnp.exp(a - m_loc), (L, HEAD_DIM))
    ct_loc = _mm_tn(ks, jnp.concatenate([v * wgt, wgt], axis=1))
    ct, m = state
    m_new = jnp.maximum(b_end + m, m_loc)
    sp = jnp.exp(b_end + m - m_new)
    sl = jnp.exp(m_loc - m_new)
    new_state = (sp * ct + sl * ct_loc, m_new)
    if not with_out:
        return new_state, None
    li_r = gpt[j_i:j_i + 1, :]
    b_r = gpt[j_b:j_b + 1, :]
    t = lax.broadcasted_iota(jnp.int32, (L, L), 0)
    s = lax.broadcasted_iota(jnp.int32, (L, L), 1)
    incl = (s >= t) if rev else (s <= t)
    e = jnp.where(incl, li_r - b_r, -jnp.inf)
    c = jnp.broadcast_to(jnp.maximum(jnp.max(e, axis=1, keepdims=True), m), (L, L))
    p = jnp.exp(e - c) * _mm_nt(q, ks)
    inter = jnp.exp(m - c)
    pv = _mm(p, jnp.concatenate([v, jnp.ones_like(v)], axis=1))
    qc = _mm(q, ct)
    num = pv[:, :HEAD_DIM] + inter * qc[:, :HEAD_DIM]
    den = pv[:, HEAD_DIM:] + inter * qc[:, HEAD_DIM:]
    m_t = jnp.broadcast_to(b_c, (L, L)) + c
    return new_state, num / jnp.maximum(jnp.abs(den), jnp.exp(-m_t))


def _mlstm_kernel(*refs, ctx_out, n_ctx, n_lat):
    if ctx_out:
        (kc_ref, vc_ref, gpc_ref, gptc_ref, qc_ref, oc_ref, kl_ref, vl_ref, gpl_ref, gptl_ref, ql_ref, ol_ref,
         norm_ref, yc_ref, yl_ref, hfc, hbc, hfl, hbl) = refs
    else:
        (kc_ref, vc_ref, gpc_ref, gptc_ref, kl_ref, vl_ref, gpl_ref, gptl_ref, ql_ref, ol_ref,
         norm_ref, yl_ref, hfl, hbl) = refs
        qc_ref = oc_ref = yc_ref = hfc = hbc = None
    L = ML_CHUNK

    def run(k_ref, v_ref, gp_ref, gpt_ref, q_ref, hf, hb, n_chunks, st_f, st_b, with_out):
        def body(j, carry):
            st_f, st_b = carry
            rf = pl.ds(pl.multiple_of(j * L, L), L)
            jb = n_chunks - 1 - j
            rb = pl.ds(pl.multiple_of(jb * L, L), L)
            st_f, h_f = _mlstm_chunk(q_ref[rf, :] if with_out else None, k_ref[rf, :], v_ref[rf, :],
                                     gp_ref[rf, :], gpt_ref[j], st_f, False, with_out)
            st_b, h_b = _mlstm_chunk(q_ref[rb, :] if with_out else None, k_ref[rb, :], v_ref[rb, :],
                                     gp_ref[rb, :], gpt_ref[jb], st_b, True, with_out)
            if with_out:
                hf[rf, :] = h_f
                hb[rb, :] = h_b
            return st_f, st_b
        return lax.fori_loop(0, n_chunks, body, (st_f, st_b))

    zero = (jnp.zeros((HEAD_DIM, 2 * HEAD_DIM), F32), jnp.zeros((1, 1), F32))
    st_f, st_b = run(kc_ref, vc_ref, gpc_ref, gptc_ref, qc_ref, hfc, hbc, n_ctx, zero, zero, ctx_out)
    run(kl_ref, vl_ref, gpl_ref, gptl_ref, ql_ref, hfl, hbl, n_lat, st_f, st_b, True)

    def finish(hf, hb, o_ref, y_ref):
        y_ref[...] = (_sigmoid(o_ref[...]) * (_rms(hf[...] + hb[...]) * norm_ref[...])).astype(y_ref.dtype)

    finish(hfl, hbl, ol_ref, yl_ref)
    if ctx_out:
        finish(hfc, hbc, oc_ref, yc_ref)


def _seq_spec(rows, col):
    return pl.BlockSpec((None, rows, HEAD_DIM), lambda b, h: (b, 0, col // HEAD_DIM + h))


def _gp_specs(rows):
    return [pl.BlockSpec((None, rows, LANES), lambda b, h: (b, 0, h)),
            pl.BlockSpec((None, rows // ML_CHUNK, N_GATE, ML_CHUNK), lambda b, h: (b, 0, h, 0))]


def _mlstm(zc, gpc, gptc, zl, gpl, gptl, norm, ctx_out):
    nb, tc, _ = zc.shape
    tl = zl.shape[1]
    ctx_in = [zc, zc, gpc, gptc] + ([zc, zc] if ctx_out else [])
    ctx_specs = [_seq_spec(tc, C_ML_K), _seq_spec(tc, C_ML_V)] + _gp_specs(tc) + (
        [_seq_spec(tc, C_ML_Q), _seq_spec(tc, C_ML_O)] if ctx_out else [])
    lat_in = [zl, zl, gpl, gptl, zl, zl]
    lat_specs = [_seq_spec(tl, C_ML_K), _seq_spec(tl, C_ML_V)] + _gp_specs(tl) + [
        _seq_spec(tl, C_ML_Q), _seq_spec(tl, C_ML_O)]
    out_spec = lambda rows: pl.BlockSpec((None, rows, HEAD_DIM), lambda b, h: (b, 0, h))
    out_shape = lambda rows: jax.ShapeDtypeStruct((nb, rows, BRANCH_W), BF16)
    scr = lambda rows: [pltpu.VMEM((rows, HEAD_DIM), F32), pltpu.VMEM((rows, HEAD_DIM), F32)]
    res = pl.pallas_call(
        functools.partial(_mlstm_kernel, ctx_out=ctx_out, n_ctx=tc // ML_CHUNK, n_lat=tl // ML_CHUNK),
        grid=(nb, N_HEADS),
        in_specs=ctx_specs + lat_specs + [pl.BlockSpec((1, HEAD_DIM), lambda b, h: (0, h))],
        out_specs=([out_spec(tc)] if ctx_out else []) + [out_spec(tl)],
        out_shape=([out_shape(tc)] if ctx_out else []) + [out_shape(tl)],
        scratch_shapes=(scr(tc) if ctx_out else []) + scr(tl),
        compiler_params=_params(("parallel", "parallel")),
        name="mlstm",
    )(*ctx_in, *lat_in, norm)
    return (res[0], res[1]) if ctx_out else (None, res[0])


def _conv_silu(x, w3):
    rows = x.shape[0]
    t = lax.broadcasted_iota(jnp.int32, x.shape, 0)
    prev = jnp.where(t == 0, 0.0, pltpu.roll(x, 1, 0))
    nxt = jnp.where(t == rows - 1, 0.0, pltpu.roll(x, rows - 1, 0))
    return _silu(w3[0:1, :] * prev + w3[1:2, :] * x + w3[2:3, :] * nxt)


def _l2norm(x):
    return x * lax.rsqrt(jnp.sum(x * x, axis=-1, keepdims=True) + EPS)


def _chunk_totals(gc_r, rev):
    L = GD_CHUNK
    pos = lax.broadcasted_iota(jnp.int32, gc_r.shape, 1)
    tot = jnp.zeros_like(gc_r)
    for c in range(GD_SUPER // L):
        i = c * L if rev else c * L + L - 1
        tot = jnp.where((pos >= c * L) & (pos < (c + 1) * L), gc_r[:, i:i + 1], tot)
    return tot


def _gdn_prepare(blocks):
    t = lax.broadcasted_iota(jnp.int32, (GD_SUPER, GD_SUPER), 0)
    s = lax.broadcasted_iota(jnp.int32, (GD_SUPER, GD_SUPER), 1)
    x = t ^ s
    same = x < GD_CHUNK
    kk0 = [_mm_nt(k, k) for k, _, _, _, _ in blocks]
    qk0 = [None if qs is None else _mm_nt(qs, k) for k, _, qs, _, _ in blocks]
    kT = [k.T for k, _, _, _, _ in blocks]
    prob = []
    for i, (k, v, qs, gp, gpt) in enumerate(blocks):
        for rev in (False, True):
            j_b, j_g = (5, 7) if rev else (4, 6)
            beta = gp[:, j_b:j_b + 1]
            gc_c = gp[:, j_g:j_g + 1]
            gc_r = gpt[j_g:j_g + 1, :]
            incl = same & ((s >= t) if rev else (s <= t))
            strict = same & ((s > t) if rev else (s < t))
            decay = jnp.exp(jnp.where(incl, gc_c - gc_r, -jnp.inf))
            m = jnp.where(strict, (beta * kk0[i]) * decay, 0.0)
            prob.append(dict(i=i, rev=rev, beta=beta, gc_r=gc_r, decay=decay, m=m, eg=jnp.exp(gc_c)))
    tinv = [jnp.where(x == 0, 1.0, 0.0) - jnp.where(x == 1, p["m"], 0.0) for p in prob]
    for lvl in range(1, 6):
        tc = [_mm(tinv[n], jnp.where((x >> lvl) == 1, p["m"], 0.0)) for n, p in enumerate(prob)]
        tinv = [tinv[n] - _mm(tc[n], tinv[n]) for n in range(len(prob))]
    out = [[] for _ in blocks]
    for n, p in enumerate(prob):
        k, v, qs, _, _ = blocks[p["i"]]
        y = _mm(tinv[n], jnp.concatenate([v * p["beta"], (k * p["beta"]) * p["eg"]], axis=1))
        kendT = (kT[p["i"]] * jnp.exp(_chunk_totals(p["gc_r"], p["rev"]) - p["gc_r"])).astype(BF16)
        qd = None if qs is None else (qs * p["eg"]).astype(BF16)
        qk = None if qs is None else (qk0[p["i"]] * p["decay"]).astype(BF16)
        out[p["i"]].append((y[:, :HEAD_DIM], y[:, HEAD_DIM:].astype(BF16), kendT, qd, qk))
    return out


def _gdn_kernel(*refs, ctx_out, n_ctx, n_lat):
    n_in = 16 if ctx_out else 14
    n_out = 2 if ctx_out else 1
    if ctx_out:
        (kc_ref, vc_ref, gpc_ref, gptc_ref, qc_ref, zc_ref, kl_ref, vl_ref, gpl_ref, gptl_ref, ql_ref, zl_ref,
         wq_ref, wk_ref, wv_ref, norm_ref, yc_ref, yl_ref) = refs[:n_in + n_out]
    else:
        (kc_ref, vc_ref, gpc_ref, gptc_ref, kl_ref, vl_ref, gpl_ref, gptl_ref, ql_ref, zl_ref,
         wq_ref, wk_ref, wv_ref, norm_ref, yl_ref) = refs[:n_in + n_out]
        qc_ref = zc_ref = yc_ref = None
    ks, vs, qs, gpa, gpta, o_f, o_b = refs[n_in + n_out:n_in + n_out + 7]
    per_dir = refs[n_in + n_out + 7:]
    u_s, w_s, kT_s, qd_s, qk_s = (per_dir[0:2], per_dir[2:4], per_dir[4:6], per_dir[6:8], per_dir[8:10])
    L = GD_CHUNK
    SB = GD_SUPER
    tc = n_ctx * SB
    tiles = SB // ML_CHUNK

    ks[0:tc, :] = _l2norm(_conv_silu(kc_ref[...], wk_ref[...]))
    vs[0:tc, :] = _conv_silu(vc_ref[...], wv_ref[...])
    ks[tc:, :] = _l2norm(_conv_silu(kl_ref[...], wk_ref[...]))
    vs[tc:, :] = _conv_silu(vl_ref[...], wv_ref[...])
    qs[tc:, :] = _l2norm(_conv_silu(ql_ref[...], wq_ref[...])) * (HEAD_DIM ** -0.5)
    if ctx_out:
        qs[0:tc, :] = _l2norm(_conv_silu(qc_ref[...], wq_ref[...])) * (HEAD_DIM ** -0.5)
    gpa[0:tc, :] = gpc_ref[...]
    gpa[tc:, :] = gpl_ref[...]
    gpta[0:n_ctx * tiles] = gptc_ref[...]
    gpta[n_ctx * tiles:] = gptl_ref[...]

    def prepare(js, with_out):
        rows = [pl.ds(pl.multiple_of(j * SB, SB), SB) for j in js]
        blocks = [(ks[r, :], vs[r, :], qs[r, :] if with_out else None, gpa[r, :],
                   jnp.concatenate([gpta[j * tiles + i] for i in range(tiles)], axis=1))
                  for j, r in zip(js, rows)]
        for j, r, res in zip(js, rows, _gdn_prepare(blocks)):
            for d, (u, w, kT, qd, qk) in enumerate(res):
                u_s[d][r, :] = u
                w_s[d][r, :] = w
                kT_s[d][j] = kT
                if with_out:
                    qd_s[d][r, :] = qd
                    qk_s[d][r, :] = qk

    def prepare_loop(lo, n, with_out):
        group = GD_PREP_GROUP if n % GD_PREP_GROUP == 0 else 1

        def body(i, carry):
            prepare([lo + i * group + g for g in range(group)], with_out)
            return carry
        lax.fori_loop(0, n // group, body, 0)

    prepare_loop(0, n_ctx, ctx_out)
    prepare_loop(n_ctx, n_lat, True)

    def step(d, j, c, S, with_out):
        row0 = pl.multiple_of(j * SB + c * L, L)
        rows = pl.ds(row0, L)
        cols = slice(c * L, (c + 1) * L)
        edge = gpa[pl.ds(row0 + (0 if d else L - 1), 1), :]
        g_end = jnp.exp(edge[:, 7:8] if d else edge[:, 6:7])
        Sb = S.astype(BF16)
        v_new = (u_s[d][rows, :] - jnp.dot(w_s[d][rows, :], Sb, preferred_element_type=F32)).astype(BF16)
        if with_out:
            out = (jnp.dot(qd_s[d][rows, :], Sb, preferred_element_type=F32)
                   + jnp.dot(qk_s[d][rows, cols], v_new, preferred_element_type=F32))
            (o_b if d else o_f)[rows, :] = out
        return S * g_end + jnp.dot(kT_s[d][j, :, cols], v_new, preferred_element_type=F32)

    def scan(lo, n, S_f, S_b, with_out):
        def body(i, carry):
            S_f, S_b = carry
            for c in range(SB // L):
                S_f = step(0, lo + i, c, S_f, with_out)
                S_b = step(1, lo + n - 1 - i, SB // L - 1 - c, S_b, with_out)
            return S_f, S_b
        return lax.fori_loop(0, n, body, (S_f, S_b))

    zero = jnp.zeros((HEAD_DIM, HEAD_DIM), F32)
    S_f, S_b = scan(0, n_ctx, zero, zero, ctx_out)
    scan(n_ctx, n_lat, S_f, S_b, True)

    def finish(rows, z_ref, y_ref):
        y_ref[...] = ((_rms(o_f[rows, :] + o_b[rows, :]) * norm_ref[...]) * _silu(z_ref[...])).astype(y_ref.dtype)

    finish(slice(tc, None), zl_ref, yl_ref)
    if ctx_out:
        finish(slice(0, tc), zc_ref, yc_ref)


def _gdn(zc, gpc, gptc, zl, gpl, gptl, conv, norm, ctx_out):
    nb, tc, _ = zc.shape
    tl = zl.shape[1]
    ctx_in = [zc, zc, gpc, gptc] + ([zc, zc] if ctx_out else [])
    ctx_specs = [_seq_spec(tc, C_GD_K), _seq_spec(tc, C_GD_V)] + _gp_specs(tc) + (
        [_seq_spec(tc, C_GD_Q), _seq_spec(tc, C_GD_Z)] if ctx_out else [])
    lat_in = [zl, zl, gpl, gptl, zl, zl]
    lat_specs = [_seq_spec(tl, C_GD_K), _seq_spec(tl, C_GD_V)] + _gp_specs(tl) + [
        _seq_spec(tl, C_GD_Q), _seq_spec(tl, C_GD_Z)]
    conv_spec = lambda off: pl.BlockSpec((3, HEAD_DIM), lambda b, h: (0, off // HEAD_DIM + h))
    out_spec = lambda rows: pl.BlockSpec((None, rows, HEAD_DIM), lambda b, h: (b, 0, h))
    out_shape = lambda rows: jax.ShapeDtypeStruct((nb, rows, BRANCH_W), BF16)
    rows = tc + tl
    seq = lambda dt: pltpu.VMEM((rows, HEAD_DIM), dt)
    two = lambda spec: [spec, spec]
    scratch = ([seq(F32), seq(F32), seq(F32), seq(F32), pltpu.VMEM((rows // ML_CHUNK, N_GATE, ML_CHUNK), F32),
                seq(F32), seq(F32)]
               + two(seq(F32)) + two(seq(BF16)) + two(pltpu.VMEM((rows // GD_SUPER, HEAD_DIM, GD_SUPER), BF16))
               + two(seq(BF16)) + two(pltpu.VMEM((rows, GD_SUPER), BF16)))
    res = pl.pallas_call(
        functools.partial(_gdn_kernel, ctx_out=ctx_out, n_ctx=tc // GD_SUPER, n_lat=tl // GD_SUPER),
        grid=(nb, N_HEADS),
        in_specs=ctx_specs + lat_specs + [conv_spec(0), conv_spec(BRANCH_W), conv_spec(2 * BRANCH_W),
                                          pl.BlockSpec((1, HEAD_DIM), lambda b, h: (0, 0))],
        out_specs=([out_spec(tc)] if ctx_out else []) + [out_spec(tl)],
        out_shape=([out_shape(tc)] if ctx_out else []) + [out_shape(tl)],
        scratch_shapes=scratch,
        compiler_params=_params(("parallel", "parallel")),
        name="gdn",
    )(*ctx_in, *lat_in, conv, conv, conv, norm)
    return (res[0], res[1]) if ctx_out else (None, res[0])


def _rope(x, cos, sin):
    lane = lax.broadcasted_iota(jnp.int32, x.shape, 1)
    quarter = HEAD_DIM // 4
    partner = jnp.where((lane & (2 * quarter - 1)) < quarter,
                        pltpu.roll(x, HEAD_DIM - quarter, 1), pltpu.roll(x, quarter, 1))
    return x * cos + partner * sin


def _attn_kernel(*refs, latent, tc):
    if latent:
        (q_ref, kc_ref, vc_ref, kl_ref, vl_ref, cos_ref, sin_ref, cosq_ref, sinq_ref, qn_ref, kn_ref,
         o_ref, k_scr, v_scr) = refs
    else:
        q_ref, kc_ref, vc_ref, qn_ref, kn_ref, o_ref, k_scr, v_scr = refs

    @pl.when(pl.program_id(2) == 0)
    def _():
        k_scr[0:tc, :] = (_rms(kc_ref[...]) * kn_ref[...]).astype(BF16)
        v_scr[0:tc, :] = vc_ref[...].astype(BF16)
        if latent:
            kl = _rope(_rms(kl_ref[...]) * kn_ref[...], cos_ref[...], sin_ref[...])
            k_scr[tc:, :] = kl.astype(BF16)
            v_scr[tc:, :] = vl_ref[...].astype(BF16)

    scale = HEAD_DIM ** -0.5
    for g in range(AT_Q_HEADS // AT_KV_HEADS):
        cols = slice(g * HEAD_DIM, (g + 1) * HEAD_DIM)
        q = _rms(q_ref[:, cols]) * qn_ref[...]
        if latent:
            q = _rope(q, cosq_ref[...], sinq_ref[...])
        s = lax.dot_general((q * scale).astype(BF16), k_scr[...], (((1,), (1,)), ((), ())),
                            preferred_element_type=F32)
        p = jnp.exp(s - jnp.max(s, axis=-1, keepdims=True))
        l = jnp.sum(p, axis=-1, keepdims=True)
        o_ref[:, cols] = (jnp.dot(p.astype(BF16), v_scr[...], preferred_element_type=F32) / l).astype(o_ref.dtype)


def _attention(zq, zc, zl, cos, sin, qn, kn):
    latent = zl is not None
    nb, tq_total, _ = zq.shape
    tc = zc.shape[1]
    tk = tc + (zl.shape[1] if latent else 0)
    gq = (AT_Q_HEADS // AT_KV_HEADS) * HEAD_DIM
    head_spec = lambda rows, col: pl.BlockSpec((None, rows, HEAD_DIM), lambda b, h, i: (b, 0, col // HEAD_DIM + h))
    vec_spec = pl.BlockSpec((1, HEAD_DIM), lambda b, h, i: (0, 0))
    in_specs = [pl.BlockSpec((None, TQ_ATTN, gq), lambda b, h, i: (b, i, C_AT_Q // gq + h)),
                head_spec(tc, C_AT_K), head_spec(tc, C_AT_V)]
    args = [zq, zc, zc]
    if latent:
        tl = zl.shape[1]
        in_specs += [head_spec(tl, C_AT_K), head_spec(tl, C_AT_V),
                     pl.BlockSpec((tl, HEAD_DIM), lambda b, h, i: (0, 0)),
                     pl.BlockSpec((tl, HEAD_DIM), lambda b, h, i: (0, 0)),
                     pl.BlockSpec((TQ_ATTN, HEAD_DIM), lambda b, h, i: (i, 0)),
                     pl.BlockSpec((TQ_ATTN, HEAD_DIM), lambda b, h, i: (i, 0))]
        args += [zl, zl, cos, sin, cos, sin]
    return pl.pallas_call(
        functools.partial(_attn_kernel, latent=latent, tc=tc),
        grid=(nb, AT_KV_HEADS, tq_total // TQ_ATTN),
        in_specs=in_specs + [vec_spec, vec_spec],
        out_specs=pl.BlockSpec((None, TQ_ATTN, gq), lambda b, h, i: (b, i, h)),
        out_shape=jax.ShapeDtypeStruct((nb, tq_total, BRANCH_W), BF16),
        scratch_shapes=[pltpu.VMEM((tk, HEAD_DIM), BF16), pltpu.VMEM((tk, HEAD_DIM), BF16)],
        compiler_params=_params(("parallel", "parallel", "arbitrary")),
        name="attention",
    )(*args, qn, kn)


def _merge_kernel(h_ref, xcol_ref, modcol_ref, y0_ref, y1_ref, y2_ref, y3_ref,
                  wg0_ref, wg1_ref, wg2_ref, wg3_ref, wb_ref, wo_ref, o_ref, m_scr):
    s = pl.program_id(2)
    n_gate = D_MODEL // TN_MERGE
    ys = (y0_ref, y1_ref, y2_ref, y3_ref)
    wgs = (wg0_ref, wg1_ref, wg2_ref, wg3_ref)

    @pl.when(s < n_gate)
    def _():
        h = h_ref[...]
        acc = jnp.zeros((TM_MERGE, TN_MERGE), F32)
        for n in range(N_BRANCH):
            gate = jnp.dot(h, wgs[n][...], preferred_element_type=F32)
            acc = acc + _sigmoid(gate) * jnp.dot(ys[n][...], wb_ref[n], preferred_element_type=F32)
        m_scr[s] = acc.astype(BF16)

    @pl.when(s >= n_gate)
    def _():
        acc = jnp.zeros((TM_MERGE, TN_OUT), F32)
        for kk in range(n_gate):
            acc = acc + jnp.dot(m_scr[kk], wo_ref[kk * TN_MERGE:(kk + 1) * TN_MERGE, :],
                                preferred_element_type=F32)
        o_ref[...] = xcol_ref[...] + modcol_ref[5:6, :] * acc


def _merge(x, h, mod, ys, w_gate, w_branch, w_out, mod_row):
    groups, rows, _ = x.shape
    n_gate = D_MODEL // TN_MERGE
    n_out = D_MODEL // TN_OUT
    mrow = (lambda b: b) if mod_row is None else (lambda b: mod_row)
    gate_col = lambda s: jnp.minimum(s, n_gate - 1)
    out_col = lambda s: jnp.maximum(s - n_gate, 0)
    y_spec = pl.BlockSpec((None, TM_MERGE, BRANCH_W), lambda b, i, s: (b, i, 0))
    wg_spec = lambda n: pl.BlockSpec((D_MODEL, TN_MERGE), lambda b, i, s: (0, n * n_gate + gate_col(s)))
    return pl.pallas_call(
        _merge_kernel,
        grid=(groups, rows // TM_MERGE, n_gate + n_out),
        in_specs=[pl.BlockSpec((None, TM_MERGE, D_MODEL), lambda b, i, s: (b, i, 0)),
                  pl.BlockSpec((None, TM_MERGE, TN_OUT), lambda b, i, s: (b, i, out_col(s))),
                  pl.BlockSpec((None, N_MOD, TN_OUT), lambda b, i, s: (mrow(b), 0, out_col(s))),
                  y_spec, y_spec, y_spec, y_spec,
                  wg_spec(0), wg_spec(1), wg_spec(2), wg_spec(3),
                  pl.BlockSpec((N_BRANCH, BRANCH_W, TN_MERGE), lambda b, i, s: (0, 0, gate_col(s))),
                  pl.BlockSpec((D_MODEL, TN_OUT), lambda b, i, s: (0, out_col(s)))],
        out_specs=pl.BlockSpec((None, TM_MERGE, TN_OUT), lambda b, i, s: (b, i, out_col(s))),
        out_shape=jax.ShapeDtypeStruct(x.shape, F32),
        scratch_shapes=[pltpu.VMEM((n_gate, TM_MERGE, TN_MERGE), BF16)],
        compiler_params=_params(("parallel", "parallel", "arbitrary")),
        name="merge",
    )(h, x, mod, *ys, w_gate, w_gate, w_gate, w_gate, w_branch, w_out)


def _rope_tables(seq):
    n = HEAD_DIM // 4
    inv = ROPE_THETA ** (-jnp.arange(n, dtype=F32) / n)
    pos = jnp.arange(seq)
    ar = (pos // GRID_W).astype(F32)[:, None] * inv
    ac = (pos % GRID_W).astype(F32)[:, None] * inv
    cos = jnp.concatenate([jnp.cos(ar), jnp.cos(ar), jnp.cos(ac), jnp.cos(ac)], axis=1)
    sin = jnp.concatenate([-jnp.sin(ar), jnp.sin(ar), -jnp.sin(ac), jnp.sin(ac)], axis=1)
    return cos, sin


def _gate_params(ml_if_bias, gd_a_log, gd_dt_bias):
    zeros = jnp.zeros((2, N_HEADS), F32)
    bias = jnp.concatenate([ml_if_bias.reshape(4, N_HEADS), zeros, gd_dt_bias], axis=0).T
    alog = jnp.concatenate([jnp.zeros((6, N_HEADS), F32), gd_a_log], axis=0).T
    pad = lambda a: jnp.pad(a, ((0, 0), (0, LANES - N_GATE))).reshape(1, N_HEADS * LANES)
    col = lambda a: a.reshape(N_HEADS * N_GATE, 1)
    return pad(bias), pad(alog), col(bias), col(alog)


def _gate_weights(w_in):
    g = jnp.concatenate([w_in[:, R_ML_IF:R_ML_IF + 16], w_in[:, R_GD_BA:R_GD_BA + 16]], axis=1)
    g = g.reshape(D_MODEL, N_GATE, N_HEADS).transpose(0, 2, 1)
    w_gc = jnp.pad(g, ((0, 0), (0, 0), (0, LANES - N_GATE))).reshape(D_MODEL, N_HEADS * LANES)
    w_gr = g.reshape(D_MODEL, N_HEADS * N_GATE).T
    return w_gc.astype(BF16), w_gr.astype(BF16)


def kernel(x, c, ctx, c_ctx, mod_w, mod_b, ffn1_norm, ffn1_w_in, ffn1_w_out, mix_norm, w_in, sg_norm, sg_w, sg_b, ml_if_bias, ml_norm, gd_conv, gd_a_log, gd_dt_bias, gd_norm, at_q_norm, at_k_norm, w_branch, w_out, ffn2_norm, ffn2_w_in, ffn2_w_out):
    nb, seq, _ = x.shape
    tc = ctx.shape[1]
    depth = mod_w.shape[0]
    ctx_row = nb
    cos, sin = _rope_tables(seq)

    cc = jnp.zeros((MOD_ROWS, D_MODEL), F32).at[:nb].set(c).at[ctx_row].set(c_ctx)
    mods = _modulation(cc, mod_w, mod_b).reshape(depth, MOD_ROWS, N_MOD, D_MODEL)

    ffn_bf16 = [_to_bf16(w) for w in (ffn1_w_in, ffn1_w_out, ffn2_w_in, ffn2_w_out)]
    flat = lambda a: a.reshape(1, nb * tc, a.shape[-1])
    unflat = lambda a: a.reshape(nb, tc, a.shape[-1])
    xc = ctx
    for l in range(depth):
        last = l == depth - 1
        ctx_out = not last
        mod = mods[l]
        wl = w_in[l]
        w_main = jnp.concatenate([wl[:, :R_ML_IF], wl[:, R_GD_KV:R_GD_BA], wl[:, R_AT_KV:R_GATE]], axis=1).astype(BF16)
        w_gate = wl[:, R_GATE:].astype(BF16)
        w_gc, w_gr = _gate_weights(wl)
        bias_c, alog_c, bias_r, alog_r = _gate_params(ml_if_bias[l], gd_a_log[l], gd_dt_bias[l])
        f1_in, f1_out = ffn_bf16[0][l], ffn_bf16[1][l]
        f2_in, f2_out = ffn_bf16[2][l], ffn_bf16[3][l]
        wb, wo = w_branch[l].astype(BF16), w_out[l].astype(BF16)
        sgw = sg_w[l].astype(BF16)
        sg_bias = jnp.repeat(sg_b[l].T, BRANCH_W // SG_GROUPS, axis=1)
        sgn = sg_norm[l].reshape(1, BRANCH_W)
        mln = ml_norm[l].reshape(1, BRANCH_W)
        gdn_g = gd_norm[l].reshape(1, HEAD_DIM)
        qn, kn = at_q_norm[l].reshape(1, HEAD_DIM), at_k_norm[l].reshape(1, HEAD_DIM)

        x = _ffn(x, mod, ffn1_norm[l], f1_in, f1_out, 0, None)
        xc = unflat(_ffn(flat(xc), mod, ffn1_norm[l], f1_in, f1_out, 0, ctx_row))

        zl, gcl, grl, hl = _inproj(x, mod, mix_norm[l], w_main, w_gc, w_gr, N_MAIN, None)
        zc, gcc, grc, hc = _inproj(flat(xc), mod, mix_norm[l], w_main, w_gc, w_gr,
                                   N_MAIN if ctx_out else KV_MAIN, ctx_row)
        zc = unflat(zc)
        gcc = unflat(gcc)
        grc = grc.reshape(N_HEADS * N_GATE, nb, tc).transpose(1, 0, 2)
        gpl, gptl = _gateprep(gcl, grl, bias_c, alog_c, bias_r, alog_r)
        gpc, gptc = _gateprep(gcc, grc, bias_c, alog_c, bias_r, alog_r)

        y_sg_l = _spatial_gating(zl, sgn, sgw, sg_bias)
        y_ml_c, y_ml_l = _mlstm(zc, gpc, gptc, zl, gpl, gptl, mln, ctx_out)
        y_gd_c, y_gd_l = _gdn(zc, gpc, gptc, zl, gpl, gptl, gd_conv[l], gdn_g, ctx_out)
        y_at_l = _attention(zl, zc, zl, cos, sin, qn, kn)
        x = _merge(x, hl, mod, [y_sg_l, y_ml_l, y_gd_l, y_at_l], w_gate, wb, wo, None)
        if ctx_out:
            y_sg_c = _spatial_gating(zc, sgn, sgw, sg_bias)
            y_at_c = _attention(zc, zc, None, None, None, qn, kn)
            xc = unflat(_merge(flat(xc), hc, mod, [flat(y) for y in (y_sg_c, y_ml_c, y_gd_c, y_at_c)],
                               w_gate, wb, wo, ctx_row))

        x = _ffn(x, mod, ffn2_norm[l], f2_in, f2_out, 6, None)
        if ctx_out:
            xc = unflat(_ffn(flat(xc), mod, ffn2_norm[l], f2_in, f2_out, 6, ctx_row))
    return x
```

```python
import functools
import math

import jax
import jax.numpy as jnp
from jax import lax
from jax.experimental import pallas as pl
from jax.experimental.pallas import tpu as pltpu

F32 = jnp.float32
BF16 = jnp.bfloat16
EPS = 1e-6

D_MODEL = 2048
D_FF = 5632
N_MOD = 9
N_BRANCH = 4
BRANCH_W = 512
HEAD_DIM = 128
N_HEADS = 4
SG_GROUPS = 4
ML_CHUNK = 128
GD_CHUNK = 64
GD_SUPER = 256
GD_PREP_GROUP = 2
AT_Q_HEADS = 4
AT_KV_HEADS = 2
GRID_W = 64
ROPE_THETA = 10000.0
FFN_RESIDUAL = 0.5

LANES = 128
N_GATE = 8
MOD_ROWS = 16

C_ML_K, C_ML_V = 0, 512
C_GD_K, C_GD_V = 1024, 1536
C_AT_K, C_AT_V = 2048, 2304
KV_MAIN = 2560
C_SG_U, C_SG_V = 2560, 3072
C_ML_Q, C_ML_O = 3584, 4096
C_GD_Q, C_GD_Z = 4608, 5120
C_AT_Q = 5632
N_MAIN = 6144
R_ML_IF = 1024
R_GD_KV = 1040
R_GD_BA = 2064
R_AT_KV = 2080
R_REST = 2592
R_GATE = 6176

VMEM_LIMIT = 56 * 1024 * 1024

TM_FFN = 512
TF_FFN = 512
TM_PROJ = 1024
TN_PROJ = 512
TM_MERGE = 1024
TN_MERGE = 256
TN_OUT = 512
TQ_ATTN = 256
TR_SG = 256
TG_GATE = 256
CAST_BLOCK_BYTES = 8 * 1024 * 1024
SPLIT_ROWS = 128
TK_MOD = 128


def _params(sem):
    return pltpu.CompilerParams(dimension_semantics=sem, vmem_limit_bytes=VMEM_LIMIT)


def _mm(a, b):
    return jnp.dot(a.astype(BF16), b.astype(BF16), preferred_element_type=F32)


def _mm_nt(a, b):
    return lax.dot_general(a.astype(BF16), b.astype(BF16), (((1,), (1,)), ((), ())),
                           preferred_element_type=F32)


def _mm_tn(a, b):
    return lax.dot_general(a.astype(BF16), b.astype(BF16), (((0,), (0,)), ((), ())),
                           preferred_element_type=F32)


def _sigmoid(x):
    return 1.0 / (1.0 + jnp.exp(-x))


def _silu(x):
    return x * _sigmoid(x)


def _softplus(x):
    return jnp.maximum(x, 0.0) + jnp.log1p(jnp.exp(-jnp.abs(x)))


def _gelu_tanh(x):
    c = math.sqrt(2.0 / math.pi)
    return x * (0.5 * (1.0 + jnp.tanh(c * (x + 0.044715 * (x * x * x)))))


def _rms(x):
    return x * lax.rsqrt(jnp.mean(x * x, axis=-1, keepdims=True) + EPS)


def _norm_mod(x, gain, shift, scale):
    return (_rms(x) * gain) * (1.0 + scale) + shift


def _cast_kernel(x_ref, o_ref):
    o_ref[...] = x_ref[...].astype(o_ref.dtype)


def _to_bf16(w):
    depth, r, c = w.shape
    rows = 1 << (max(16, min(r, CAST_BLOCK_BYTES // (4 * c))).bit_length() - 1)
    while r % rows:
        rows //= 2
    return pl.pallas_call(
        _cast_kernel,
        grid=(depth, r // rows),
        in_specs=[pl.BlockSpec((None, rows, c), lambda l, i: (l, i, 0))],
        out_specs=pl.BlockSpec((None, rows, c), lambda l, i: (l, i, 0)),
        out_shape=jax.ShapeDtypeStruct(w.shape, BF16),
        compiler_params=_params(("parallel", "parallel")),
        name="cast_bf16",
    )(w)


def _split_w_in_kernel(w_ref, main_ref, gate_ref):
    w = w_ref[...]
    main_ref[...] = jnp.concatenate([w[:, :R_ML_IF], w[:, R_GD_KV:R_GD_BA], w[:, R_AT_KV:R_GATE]],
                                    axis=1).astype(BF16)
    gate_ref[...] = w[:, R_GATE:].astype(BF16)


def _split_w_in(w_in):
    depth, d, n = w_in.shape
    n_gate = n - R_GATE
    return pl.pallas_call(
        _split_w_in_kernel,
        grid=(depth, d // SPLIT_ROWS),
        in_specs=[pl.BlockSpec((None, SPLIT_ROWS, n), lambda l, i: (l, i, 0))],
        out_specs=[pl.BlockSpec((None, SPLIT_ROWS, N_MAIN), lambda l, i: (l, i, 0)),
                   pl.BlockSpec((None, SPLIT_ROWS, n_gate), lambda l, i: (l, i, 0))],
        out_shape=[jax.ShapeDtypeStruct((depth, d, N_MAIN), BF16),
                   jax.ShapeDtypeStruct((depth, d, n_gate), BF16)],
        compiler_params=_params(("parallel", "parallel")),
        name="split_w_in",
    )(w_in)


def _mod_kernel(cc_ref, w_ref, b_ref, o_ref):
    @pl.when(pl.program_id(1) == 0)
    def _():
        o_ref[...] = jnp.broadcast_to(b_ref[...], o_ref.shape)

    o_ref[...] += _mm(_silu(cc_ref[...]), w_ref[...])


def _modulation(cc, mod_w, mod_b):
    depth, d, n = mod_w.shape
    ccs = cc.reshape(MOD_ROWS, d // TK_MOD, TK_MOD).transpose(1, 0, 2)
    return pl.pallas_call(
        _mod_kernel,
        grid=(depth, d // TK_MOD),
        in_specs=[pl.BlockSpec((None, MOD_ROWS, TK_MOD), lambda l, k: (k, 0, 0)),
                  pl.BlockSpec((None, TK_MOD, n), lambda l, k: (l, k, 0)),
                  pl.BlockSpec((None, 1, n), lambda l, k: (l, 0, 0))],
        out_specs=pl.BlockSpec((None, MOD_ROWS, n), lambda l, k: (l, 0, 0)),
        out_shape=jax.ShapeDtypeStruct((depth, MOD_ROWS, n), F32),
        compiler_params=_params(("parallel", "arbitrary")),
        name="modulation",
    )(ccs, mod_w, mod_b.reshape(depth, 1, n))


def _ffn_kernel(x_ref, mod_ref, g_ref, wa_ref, wb_ref, wo_ref, o_ref, h_scr, acc_scr, *, base):
    j = pl.program_id(2)

    @pl.when(j == 0)
    def _():
        h = _norm_mod(x_ref[...], g_ref[...], mod_ref[base:base + 1, :], mod_ref[base + 1:base + 2, :])
        h_scr[...] = h.astype(BF16)
        acc_scr[...] = jnp.zeros_like(acc_scr)

    h = h_scr[...]
    a = jnp.dot(h, wa_ref[...], preferred_element_type=F32)
    b = jnp.dot(h, wb_ref[...], preferred_element_type=F32)
    acc_scr[...] += _mm(_silu(a) * b, wo_ref[...])

    @pl.when(j == pl.num_programs(2) - 1)
    def _():
        o_ref[...] = x_ref[...] + (FFN_RESIDUAL * mod_ref[base + 2:base + 3, :]) * acc_scr[...]


def _ffn(x, mod, gain, w_in, w_out, base, mod_row):
    groups, rows, _ = x.shape
    nj = D_FF // TF_FFN
    mod_idx = (lambda b, i, j: (b, 0, 0)) if mod_row is None else (lambda b, i, j: (mod_row, 0, 0))
    return pl.pallas_call(
        functools.partial(_ffn_kernel, base=base),
        grid=(groups, rows // TM_FFN, nj),
        in_specs=[pl.BlockSpec((None, TM_FFN, D_MODEL), lambda b, i, j: (b, i, 0)),
                  pl.BlockSpec((None, N_MOD, D_MODEL), mod_idx),
                  pl.BlockSpec((1, D_MODEL), lambda b, i, j: (0, 0)),
                  pl.BlockSpec((D_MODEL, TF_FFN), lambda b, i, j: (0, j)),
                  pl.BlockSpec((D_MODEL, TF_FFN), lambda b, i, j: (0, j + nj)),
                  pl.BlockSpec((TF_FFN, D_MODEL), lambda b, i, j: (j, 0))],
        out_specs=pl.BlockSpec((None, TM_FFN, D_MODEL), lambda b, i, j: (b, i, 0)),
        out_shape=jax.ShapeDtypeStruct(x.shape, F32),
        scratch_shapes=[pltpu.VMEM((TM_FFN, D_MODEL), BF16), pltpu.VMEM((TM_FFN, D_MODEL), F32)],
        compiler_params=_params(("parallel", "parallel", "arbitrary")),
        name="ffn",
    )(x, mod, gain.reshape(1, D_MODEL), w_in, w_in, w_out)


def _inproj_kernel(x_ref, mod_ref, g_ref, w_ref, wgc_ref, wgr_ref, z_ref, gc_ref, gr_ref, h_ref):
    n = pl.program_id(2)

    @pl.when(n == 0)
    def _():
        h = _norm_mod(x_ref[...], g_ref[...], mod_ref[3:4, :], mod_ref[4:5, :]).astype(BF16)
        h_ref[...] = h
        gc_ref[...] = jnp.dot(h, wgc_ref[...], preferred_element_type=F32)
        gr_ref[...] = lax.dot_general(wgr_ref[...], h, (((1,), (1,)), ((), ())), preferred_element_type=F32)

    z_ref[...] = jnp.dot(h_ref[...], w_ref[...], preferred_element_type=F32)


def _inproj(x, mod, gain, w_main, w_gc, w_gr, n_cols, mod_row):
    groups, rows, _ = x.shape
    mod_idx = (lambda b, i, n: (b, 0, 0)) if mod_row is None else (lambda b, i, n: (mod_row, 0, 0))
    return pl.pallas_call(
        _inproj_kernel,
        grid=(groups, rows // TM_PROJ, n_cols // TN_PROJ),
        in_specs=[pl.BlockSpec((None, TM_PROJ, D_MODEL), lambda b, i, n: (b, i, 0)),
                  pl.BlockSpec((None, N_MOD, D_MODEL), mod_idx),
                  pl.BlockSpec((1, D_MODEL), lambda b, i, n: (0, 0)),
                  pl.BlockSpec((D_MODEL, TN_PROJ), lambda b, i, n: (0, n)),
                  pl.BlockSpec((D_MODEL, LANES), lambda b, i, n: (0, 0)),
                  pl.BlockSpec((N_HEADS * N_GATE, D_MODEL), lambda b, i, n: (0, 0))],
        out_specs=[pl.BlockSpec((None, TM_PROJ, TN_PROJ), lambda b, i, n: (b, i, n)),
                   pl.BlockSpec((None, TM_PROJ, LANES), lambda b, i, n: (b, i, 0)),
                   pl.BlockSpec((None, N_HEADS * N_GATE, TM_PROJ), lambda b, i, n: (b, 0, i)),
                   pl.BlockSpec((None, TM_PROJ, D_MODEL), lambda b, i, n: (b, i, 0))],
        out_shape=[jax.ShapeDtypeStruct((groups, rows, n_cols), F32),
                   jax.ShapeDtypeStruct((groups, rows, LANES), F32),
                   jax.ShapeDtypeStruct((groups, N_HEADS * N_GATE, rows), F32),
                   jax.ShapeDtypeStruct((groups, rows, D_MODEL), BF16)],
        compiler_params=_params(("parallel", "parallel", "arbitrary")),
        name="inproj",
    )(x, mod, gain.reshape(1, D_MODEL), w_main, w_gc, w_gr)


def _split3(e):
    e1 = e.astype(BF16)
    r1 = e - e1.astype(F32)
    e2 = r1.astype(BF16)
    e3 = (r1 - e2.astype(F32)).astype(BF16)
    return e1, e2, e3


def _gate_values(pre, a_log, cls):
    neg_sp = -_softplus(-pre)
    return jnp.where(cls < 2, pre,
                     jnp.where(cls < 4, neg_sp,
                               jnp.where(cls < 6, _sigmoid(pre), -jnp.exp(a_log) * _softplus(pre))))


def _gateprep_kernel(gc_ref, gr_ref, bc_ref, ac_ref, br_ref, ar_ref, oc_ref, or_ref):
    for i in range(TG_GATE // ML_CHUNK):
        tile = slice(i * ML_CHUNK, (i + 1) * ML_CHUNK)
        out_c, or_ref[i] = _gateprep_tile(gc_ref[tile, :], gr_ref[:, tile], bc_ref[...], ac_ref[...],
                                          br_ref[...], ar_ref[...])
        for h in range(N_HEADS):
            oc_ref[tile, h * LANES:(h + 1) * LANES] = (
                out_c if h == 0 else pltpu.roll(out_c, LANES - h * N_GATE, 1))


def _gateprep_tile(gc, gr, bias_c, alog_c, bias_r, alog_r):
    L = ML_CHUNK
    half = GD_CHUNK
    pre = gc + bias_c
    cls = lax.broadcasted_iota(jnp.int32, pre.shape, 1) & (N_GATE - 1)
    t = lax.broadcasted_iota(jnp.int32, pre.shape, 0)
    e = _gate_values(pre, alog_c, cls)
    tri = (lax.broadcasted_iota(jnp.int32, (L, L), 1) <= lax.broadcasted_iota(jnp.int32, (L, L), 0)).astype(BF16)
    e1, e2, e3 = _split3(e)
    p = (jnp.dot(tri, e1, preferred_element_type=F32) + jnp.dot(tri, e2, preferred_element_type=F32)
         + jnp.dot(tri, e3, preferred_element_type=F32))
    tot = p[L - 1:L, :]
    mid = p[half - 1:half, :]
    second = t >= half
    pre64 = p - jnp.where(second, mid, 0.0)
    suf64 = jnp.where(second, tot, mid) - p + e
    cum = jnp.where(cls == 2, p, jnp.where(cls == 3, tot - p + e, jnp.where(cls == 6, pre64, suf64)))
    out_c = jnp.where((cls == 2) | (cls == 3) | (cls == 6) | (cls == 7), cum, e)

    pre_r = gr + bias_r
    cls_r = lax.broadcasted_iota(jnp.int32, pre_r.shape, 0) & (N_GATE - 1)
    t_r = lax.broadcasted_iota(jnp.int32, pre_r.shape, 1)
    e_r = _gate_values(pre_r, alog_r, cls_r)
    tri_r = (lax.broadcasted_iota(jnp.int32, (L, L), 0) <= lax.broadcasted_iota(jnp.int32, (L, L), 1)).astype(BF16)
    f1, f2, f3 = _split3(e_r)
    pr = (jnp.dot(f1, tri_r, preferred_element_type=F32) + jnp.dot(f2, tri_r, preferred_element_type=F32)
          + jnp.dot(f3, tri_r, preferred_element_type=F32))
    tot_r = pr[:, L - 1:L]
    mid_r = pr[:, half - 1:half]
    second_r = t_r >= half
    pre64_r = pr - jnp.where(second_r, mid_r, 0.0)
    suf64_r = jnp.where(second_r, tot_r, mid_r) - pr + e_r
    cum_r = jnp.where(cls_r == 2, pr, jnp.where(cls_r == 3, tot_r - pr + e_r, jnp.where(cls_r == 6, pre64_r, suf64_r)))
    return out_c, jnp.where((cls_r == 2) | (cls_r == 3) | (cls_r == 6) | (cls_r == 7), cum_r, e_r)


def _gateprep(gc, gr, bias_c, alog_c, bias_r, alog_r):
    groups, rows, _ = gc.shape
    nt = rows // ML_CHUNK
    hw = N_HEADS * LANES
    hg = N_HEADS * N_GATE
    tiles = TG_GATE // ML_CHUNK
    return pl.pallas_call(
        _gateprep_kernel,
        grid=(groups, rows // TG_GATE),
        in_specs=[pl.BlockSpec((None, TG_GATE, LANES), lambda b, c: (b, c, 0)),
                  pl.BlockSpec((None, hg, TG_GATE), lambda b, c: (b, 0, c)),
                  pl.BlockSpec((1, LANES), lambda b, c: (0, 0)),
                  pl.BlockSpec((1, LANES), lambda b, c: (0, 0)),
                  pl.BlockSpec((hg, 1), lambda b, c: (0, 0)),
                  pl.BlockSpec((hg, 1), lambda b, c: (0, 0))],
        out_specs=[pl.BlockSpec((None, TG_GATE, hw), lambda b, c: (b, c, 0)),
                   pl.BlockSpec((None, tiles, hg, ML_CHUNK), lambda b, c: (b, c, 0, 0))],
        out_shape=[jax.ShapeDtypeStruct((groups, rows, hw), F32),
                   jax.ShapeDtypeStruct((groups, nt, hg, ML_CHUNK), F32)],
        compiler_params=_params(("parallel", "parallel")),
        name="gateprep",
    )(gc, gr, bias_c, alog_c, bias_r, alog_r)


def _sg_kernel(u_ref, v_ref, gn_ref, w_ref, bias_ref, o_ref):
    gw = BRANCH_W // SG_GROUPS
    for ch in range(TR_SG // ML_CHUNK):
        rows = slice(ch * ML_CHUNK, (ch + 1) * ML_CHUNK)
        v = _rms(_gelu_tanh(v_ref[rows, :])) * gn_ref[...]
        u = _gelu_tanh(u_ref[rows, :])
        for g in range(SG_GROUPS):
            cols = slice(g * gw, (g + 1) * gw)
            s = _mm(w_ref[g], v[:, cols]) + bias_ref[:, cols]
            o_ref[rows, cols] = (u[:, cols] * s).astype(o_ref.dtype)


def _spatial_gating(z, gn, w_s, bias_full):
    groups, rows, _ = z.shape
    ub, vb = C_SG_U // BRANCH_W, C_SG_V // BRANCH_W
    return pl.pallas_call(
        _sg_kernel,
        grid=(groups, rows // TR_SG),
        in_specs=[pl.BlockSpec((None, TR_SG, BRANCH_W), lambda b, i: (b, i, ub)),
                  pl.BlockSpec((None, TR_SG, BRANCH_W), lambda b, i: (b, i, vb)),
                  pl.BlockSpec((1, BRANCH_W), lambda b, i: (0, 0)),
                  pl.BlockSpec((SG_GROUPS, ML_CHUNK, ML_CHUNK), lambda b, i: (0, 0, 0)),
                  pl.BlockSpec((ML_CHUNK, BRANCH_W), lambda b, i: (0, 0))],
        out_specs=pl.BlockSpec((None, TR_SG, BRANCH_W), lambda b, i: (b, i, 0)),
        out_shape=jax.ShapeDtypeStruct((groups, rows, BRANCH_W), BF16),
        compiler_params=_params(("parallel", "parallel")),
        name="spatial_gating",
    )(z, z, gn, w_s, bias_full)


def _mlstm_chunk(q, k, v, gp, gpt, state, rev, with_out):
    L = ML_CHUNK
    j_i, j_b = (1, 3) if rev else (0, 2)
    li_c = gp[:, j_i:j_i + 1]
    b_c = gp[:, j_b:j_b + 1]
    b_end = b_c[0:1, :] if rev else b_c[L - 1:L, :]
    ks = k * (HEAD_DIM ** -0.5)
    a = b_end - b_c + li_c
    m_loc = jnp.max(a, axis=0, keepdims=True)
    wgt = jnp.broadcast_to(jnp.exp(a - m_loc), (L, HEAD_DIM))
    ct_loc = _mm_tn(ks, jnp.concatenate([v * wgt, wgt], axis=1))
    ct, m = state
    m_new = jnp.maximum(b_end + m, m_loc)
    sp = jnp.exp(b_end + m - m_new)
    sl = jnp.exp(m_loc - m_new)
    new_state = (sp * ct + sl * ct_loc, m_new)
    if not with_out:
        return new_state, None
    li_r = gpt[j_i:j_i + 1, :]
    b_r = gpt[j_b:j_b + 1, :]
    t = lax.broadcasted_iota(jnp.int32, (L, L), 0)
    s = lax.broadcasted_iota(jnp.int32, (L, L), 1)
    incl = (s >= t) if rev else (s <= t)
    e = jnp.where(incl, li_r - b_r, -jnp.inf)
    c = jnp.broadcast_to(jnp.maximum(jnp.max(e, axis=1, keepdims=True), m), (L, L))
    p = jnp.exp(e - c) * _mm_nt(q, ks)
    inter = jnp.exp(m - c)
    pv = _mm(p, jnp.concatenate([v, jnp.ones_like(v)], axis=1))
    qc = _mm(q, ct)
    num = pv[:, :HEAD_DIM] + inter * qc[:, :HEAD_DIM]
    den = pv[:, HEAD_DIM:] + inter * qc[:, HEAD_DIM:]
    m_t = jnp.broadcast_to(b_c, (L, L)) + c
    return new_state, num / jnp.maximum(jnp.abs(den), jnp.exp(-m_t))


def _mlstm_kernel(*refs, ctx_out, n_ctx, n_lat):
    if ctx_out:
        (kc_ref, vc_ref, gpc_ref, gptc_ref, qc_ref, oc_ref, kl_ref, vl_ref, gpl_ref, gptl_ref, ql_ref, ol_ref,
         norm_ref, yc_ref, yl_ref, hfc, hbc, hfl, hbl) = refs
    else:
        (kc_ref, vc_ref, gpc_ref, gptc_ref, kl_ref, vl_ref, gpl_ref, gptl_ref, ql_ref, ol_ref,
         norm_ref, yl_ref, hfl, hbl) = refs
        qc_ref = oc_ref = yc_ref = hfc = hbc = None
    L = ML_CHUNK

    def run(k_ref, v_ref, gp_ref, gpt_ref, q_ref, hf, hb, n_chunks, st_f, st_b, with_out):
        def body(j, carry):
            st_f, st_b = carry
            rf = pl.ds(pl.multiple_of(j * L, L), L)
            jb = n_chunks - 1 - j
            rb = pl.ds(pl.multiple_of(jb * L, L), L)
            st_f, h_f = _mlstm_chunk(q_ref[rf, :] if with_out else None, k_ref[rf, :], v_ref[rf, :],
                                     gp_ref[rf, :], gpt_ref[j], st_f, False, with_out)
            st_b, h_b = _mlstm_chunk(q_ref[rb, :] if with_out else None, k_ref[rb, :], v_ref[rb, :],
                                     gp_ref[rb, :], gpt_ref[jb], st_b, True, with_out)
            if with_out:
                hf[rf, :] = h_f
                hb[rb, :] = h_b
            return st_f, st_b
        return lax.fori_loop(0, n_chunks, body, (st_f, st_b))

    zero = (jnp.zeros((HEAD_DIM, 2 * HEAD_DIM), F32), jnp.zeros((1, 1), F32))
    st_f, st_b = run(kc_ref, vc_ref, gpc_ref, gptc_ref, qc_ref, hfc, hbc, n_ctx, zero, zero, ctx_out)
    run(kl_ref, vl_ref, gpl_ref, gptl_ref, ql_ref, hfl, hbl, n_lat, st_f, st_b, True)

    def finish(hf, hb, o_ref, y_ref):
        y_ref[...] = (_sigmoid(o_ref[...]) * (_rms(hf[...] + hb[...]) * norm_ref[...])).astype(y_ref.dtype)

    finish(hfl, hbl, ol_ref, yl_ref)
    if ctx_out:
        finish(hfc, hbc, oc_ref, yc_ref)


def _seq_spec(rows, col):
    return pl.BlockSpec((None, rows, HEAD_DIM), lambda b, h: (b, 0, col // HEAD_DIM + h))


def _gp_specs(rows):
    return [pl.BlockSpec((None, rows, LANES), lambda b, h: (b, 0, h)),
            pl.BlockSpec((None, rows // ML_CHUNK, N_GATE, ML_CHUNK), lambda b, h: (b, 0, h, 0))]


def _mlstm(zc, gpc, gptc, zl, gpl, gptl, norm, ctx_out):
    nb, tc, _ = zc.shape
    tl = zl.shape[1]
    ctx_in = [zc, zc, gpc, gptc] + ([zc, zc] if ctx_out else [])
    ctx_specs = [_seq_spec(tc, C_ML_K), _seq_spec(tc, C_ML_V)] + _gp_specs(tc) + (
        [_seq_spec(tc, C_ML_Q), _seq_spec(tc, C_ML_O)] if ctx_out else [])
    lat_in = [zl, zl, gpl, gptl, zl, zl]
    lat_specs = [_seq_spec(tl, C_ML_K), _seq_spec(tl, C_ML_V)] + _gp_specs(tl) + [
        _seq_spec(tl, C_ML_Q), _seq_spec(tl, C_ML_O)]
    out_spec = lambda rows: pl.BlockSpec((None, rows, HEAD_DIM), lambda b, h: (b, 0, h))
    out_shape = lambda rows: jax.ShapeDtypeStruct((nb, rows, BRANCH_W), BF16)
    scr = lambda rows: [pltpu.VMEM((rows, HEAD_DIM), F32), pltpu.VMEM((rows, HEAD_DIM), F32)]
    res = pl.pallas_call(
        functools.partial(_mlstm_kernel, ctx_out=ctx_out, n_ctx=tc // ML_CHUNK, n_lat=tl // ML_CHUNK),
        grid=(nb, N_HEADS),
        in_specs=ctx_specs + lat_specs + [pl.BlockSpec((1, HEAD_DIM), lambda b, h: (0, h))],
        out_specs=([out_spec(tc)] if ctx_out else []) + [out_spec(tl)],
        out_shape=([out_shape(tc)] if ctx_out else []) + [out_shape(tl)],
        scratch_shapes=(scr(tc) if ctx_out else []) + scr(tl),
        compiler_params=_params(("parallel", "parallel")),
        name="mlstm",
    )(*ctx_in, *lat_in, norm)
    return (res[0], res[1]) if ctx_out else (None, res[0])


def _conv_silu(x, w3):
    rows = x.shape[0]
    t = lax.broadcasted_iota(jnp.int32, x.shape, 0)
    prev = jnp.where(t == 0, 0.0, pltpu.roll(x, 1, 0))
    nxt = jnp.where(t == rows - 1, 0.0, pltpu.roll(x, rows - 1, 0))
    return _silu(w3[0:1, :] * prev + w3[1:2, :] * x + w3[2:3, :] * nxt)


def _l2norm(x):
    return x * lax.rsqrt(jnp.sum(x * x, axis=-1, keepdims=True) + EPS)


def _chunk_totals(gc_r, rev):
    L = GD_CHUNK
    pos = lax.broadcasted_iota(jnp.int32, gc_r.shape, 1)
    tot = jnp.zeros_like(gc_r)
    for c in range(GD_SUPER // L):
        i = c * L if rev else c * L + L - 1
        tot = jnp.where((pos >= c * L) & (pos < (c + 1) * L), gc_r[:, i:i + 1], tot)
    return tot


def _gdn_prepare(blocks):
    t = lax.broadcasted_iota(jnp.int32, (GD_SUPER, GD_SUPER), 0)
    s = lax.broadcasted_iota(jnp.int32, (GD_SUPER, GD_SUPER), 1)
    x = t ^ s
    same = x < GD_CHUNK
    kk0 = [_mm_nt(k, k) for k, _, _, _, _ in blocks]
    qk0 = [None if qs is None else _mm_nt(qs, k) for k, _, qs, _, _ in blocks]
    kT = [k.T for k, _, _, _, _ in blocks]
    prob = []
    for i, (k, v, qs, gp, gpt) in enumerate(blocks):
        for rev in (False, True):
            j_b, j_g = (5, 7) if rev else (4, 6)
            beta = gp[:, j_b:j_b + 1]
            gc_c = gp[:, j_g:j_g + 1]
            gc_r = gpt[j_g:j_g + 1, :]
            incl = same & ((s >= t) if rev else (s <= t))
            strict = same & ((s > t) if rev else (s < t))
            decay = jnp.exp(jnp.where(incl, gc_c - gc_r, -jnp.inf))
            m = jnp.where(strict, (beta * kk0[i]) * decay, 0.0)
            prob.append(dict(i=i, rev=rev, beta=beta, gc_r=gc_r, decay=decay, m=m, eg=jnp.exp(gc_c)))
    tinv = [jnp.where(x == 0, 1.0, 0.0) - jnp.where(x == 1, p["m"], 0.0) for p in prob]
    for lvl in range(1, 6):
        tc = [_mm(tinv[n], jnp.where((x >> lvl) == 1, p["m"], 0.0)) for n, p in enumerate(prob)]
        tinv = [tinv[n] - _mm(tc[n], tinv[n]) for n in range(len(prob))]
    out = [[] for _ in blocks]
    for n, p in enumerate(prob):
        k, v, qs, _, _ = blocks[p["i"]]
        y = _mm(tinv[n], jnp.concatenate([v * p["beta"], (k * p["beta"]) * p["eg"]], axis=1))
        kendT = (kT[p["i"]] * jnp.exp(_chunk_totals(p["gc_r"], p["rev"]) - p["gc_r"])).astype(BF16)
        qd = None if qs is None else (qs * p["eg"]).astype(BF16)
        qk = None if qs is None else (qk0[p["i"]] * p["decay"]).astype(BF16)
        out[p["i"]].append((y[:, :HEAD_DIM], y[:, HEAD_DIM:].astype(BF16), kendT, qd, qk))
    return out


def _gdn_kernel(*refs, ctx_out, n_ctx, n_lat):
    n_in = 16 if ctx_out else 14
    n_out = 2 if ctx_out else 1
    if ctx_out:
        (kc_ref, vc_ref, gpc_ref, gptc_ref, qc_ref, zc_ref, kl_ref, vl_ref, gpl_ref, gptl_ref, ql_ref, zl_ref,
         wq_ref, wk_ref, wv_ref, norm_ref, yc_ref, yl_ref) = refs[:n_in + n_out]
    else:
        (kc_ref, vc_ref, gpc_ref, gptc_ref, kl_ref, vl_ref, gpl_ref, gptl_ref, ql_ref, zl_ref,
         wq_ref, wk_ref, wv_ref, norm_ref, yl_ref) = refs[:n_in + n_out]
        qc_ref = zc_ref = yc_ref = None
    ks, vs, qs, gpa, gpta, o_f, o_b = refs[n_in + n_out:n_in + n_out + 7]
    per_dir = refs[n_in + n_out + 7:]
    u_s, w_s, kT_s, qd_s, qk_s = (per_dir[0:2], per_dir[2:4], per_dir[4:6], per_dir[6:8], per_dir[8:10])
    L = GD_CHUNK
    SB = GD_SUPER
    tc = n_ctx * SB
    tiles = SB // ML_CHUNK

    ks[0:tc, :] = _l2norm(_conv_silu(kc_ref[...], wk_ref[...]))
    vs[0:tc, :] = _conv_silu(vc_ref[...], wv_ref[...])
    ks[tc:, :] = _l2norm(_conv_silu(kl_ref[...], wk_ref[...]))
    vs[tc:, :] = _conv_silu(vl_ref[...], wv_ref[...])
    qs[tc:, :] = _l2norm(_conv_silu(ql_ref[...], wq_ref[...])) * (HEAD_DIM ** -0.5)
    if ctx_out:
        qs[0:tc, :] = _l2norm(_conv_silu(qc_ref[...], wq_ref[...])) * (HEAD_DIM ** -0.5)
    gpa[0:tc, :] = gpc_ref[...]
    gpa[tc:, :] = gpl_ref[...]
    gpta[0:n_ctx * tiles] = gptc_ref[...]
    gpta[n_ctx * tiles:] = gptl_ref[...]

    def prepare(js, with_out):
        rows = [pl.ds(pl.multiple_of(j * SB, SB), SB) for j in js]
        blocks = [(ks[r, :], vs[r, :], qs[r, :] if with_out else None, gpa[r, :],
                   jnp.concatenate([gpta[j * tiles + i] for i in range(tiles)], axis=1))
                  for j, r in zip(js, rows)]
        for j, r, res in zip(js, rows, _gdn_prepare(blocks)):
            for d, (u, w, kT, qd, qk) in enumerate(res):
                u_s[d][r, :] = u
                w_s[d][r, :] = w
                kT_s[d][j] = kT
                if with_out:
                    qd_s[d][r, :] = qd
                    qk_s[d][r, :] = qk

    def prepare_loop(lo, n, with_out):
        group = GD_PREP_GROUP if n % GD_PREP_GROUP == 0 else 1

        def body(i, carry):
            prepare([lo + i * group + g for g in range(group)], with_out)
            return carry
        lax.fori_loop(0, n // group, body, 0)

    def chunk_rows(j, c):
        return pl.ds(pl.multiple_of(j * SB + c * L, L), L)

    def step_load(d, j, c, with_out):
        rows = chunk_rows(j, c)
        cols = slice(c * L, (c + 1) * L)
        edge = gpa[pl.ds(pl.multiple_of(j * SB + c * L, L) + (0 if d else L - 1), 1), :]
        g_end = jnp.exp(edge[:, 7:8] if d else edge[:, 6:7])
        out_ops = (qd_s[d][rows, :], qk_s[d][rows, cols]) if with_out else None
        return g_end, u_s[d][rows, :], w_s[d][rows, :], kT_s[d][j, :, cols], out_ops

    def step_compute(ops, S):
        g_end, u, w, kT, out_ops = ops
        Sb = S.astype(BF16)
        v_new = (u - jnp.dot(w, Sb, preferred_element_type=F32)).astype(BF16)
        out = None
        if out_ops is not None:
            out = (jnp.dot(out_ops[0], Sb, preferred_element_type=F32)
                   + jnp.dot(out_ops[1], v_new, preferred_element_type=F32))
        return S * g_end + jnp.dot(kT, v_new, preferred_element_type=F32), out

    def chunk_pair(lo, n, i, c):
        return (lo + i, c), (lo + n - 1 - i, SB // L - 1 - c)

    def scan(lo, n, first, last, carry, with_out):
        def body(i, carry):
            for c in range(SB // L):
                where = chunk_pair(lo, n, i, c)
                res = [step_compute(step_load(d, *where[d], with_out), carry[d]) for d in range(2)]
                if with_out:
                    o_f[chunk_rows(*where[0]), :] = res[0][1]
                    o_b[chunk_rows(*where[1]), :] = res[1][1]
                carry = (res[0][0], res[1][0])
            return carry
        return lax.fori_loop(first, last, body, carry)

    zero = jnp.zeros((HEAD_DIM, HEAD_DIM), F32)
    prepare_loop(0, n_ctx, ctx_out)
    prepare_loop(n_ctx, n_lat, True)
    carry = scan(0, n_ctx, 0, n_ctx, (zero, zero), ctx_out)
    scan(n_ctx, n_lat, 0, n_lat, carry, True)

    def finish(rows, z_ref, y_ref):
        y_ref[...] = ((_rms(o_f[rows, :] + o_b[rows, :]) * norm_ref[...]) * _silu(z_ref[...])).astype(y_ref.dtype)

    finish(slice(tc, None), zl_ref, yl_ref)
    if ctx_out:
        finish(slice(0, tc), zc_ref, yc_ref)


def _gdn(zc, gpc, gptc, zl, gpl, gptl, conv, norm, ctx_out):
    nb, tc, _ = zc.shape
    tl = zl.shape[1]
    assert tc % GD_SUPER == 0 and tl % GD_SUPER == 0, "sequences are processed in whole 256-token superblocks"
    ctx_in = [zc, zc, gpc, gptc] + ([zc, zc] if ctx_out else [])
    ctx_specs = [_seq_spec(tc, C_GD_K), _seq_spec(tc, C_GD_V)] + _gp_specs(tc) + (
        [_seq_spec(tc, C_GD_Q), _seq_spec(tc, C_GD_Z)] if ctx_out else [])
    lat_in = [zl, zl, gpl, gptl, zl, zl]
    lat_specs = [_seq_spec(tl, C_GD_K), _seq_spec(tl, C_GD_V)] + _gp_specs(tl) + [
        _seq_spec(tl, C_GD_Q), _seq_spec(tl, C_GD_Z)]
    conv_spec = lambda off: pl.BlockSpec((3, HEAD_DIM), lambda b, h: (0, off // HEAD_DIM + h))
    out_spec = lambda rows: pl.BlockSpec((None, rows, HEAD_DIM), lambda b, h: (b, 0, h))
    out_shape = lambda rows: jax.ShapeDtypeStruct((nb, rows, BRANCH_W), BF16)
    rows = tc + tl
    seq = lambda dt: pltpu.VMEM((rows, HEAD_DIM), dt)
    two = lambda spec: [spec, spec]
    scratch = ([seq(F32), seq(F32), seq(F32), seq(F32), pltpu.VMEM((rows // ML_CHUNK, N_GATE, ML_CHUNK), F32),
                seq(F32), seq(F32)]
               + two(seq(F32)) + two(seq(BF16)) + two(pltpu.VMEM((rows // GD_SUPER, HEAD_DIM, GD_SUPER), BF16))
               + two(seq(BF16)) + two(pltpu.VMEM((rows, GD_SUPER), BF16)))
    res = pl.pallas_call(
        functools.partial(_gdn_kernel, ctx_out=ctx_out, n_ctx=tc // GD_SUPER, n_lat=tl // GD_SUPER),
        grid=(nb, N_HEADS),
        in_specs=ctx_specs + lat_specs + [conv_spec(0), conv_spec(BRANCH_W), conv_spec(2 * BRANCH_W),
                                          pl.BlockSpec((1, HEAD_DIM), lambda b, h: (0, 0))],
        out_specs=([out_spec(tc)] if ctx_out else []) + [out_spec(tl)],
        out_shape=([out_shape(tc)] if ctx_out else []) + [out_shape(tl)],
        scratch_shapes=scratch,
        compiler_params=_params(("parallel", "parallel")),
        name="gdn",
    )(*ctx_in, *lat_in, conv, conv, conv, norm)
    return (res[0], res[1]) if ctx_out else (None, res[0])


def _rope(x, cos, sin):
    lane = lax.broadcasted_iota(jnp.int32, x.shape, 1)
    quarter = HEAD_DIM // 4
    partner = jnp.where((lane & (2 * quarter - 1)) < quarter,
                        pltpu.roll(x, HEAD_DIM - quarter, 1), pltpu.roll(x, quarter, 1))
    return x * cos + partner * sin


def _attn_kernel(*refs, latent, tc):
    if latent:
        (q_ref, kc_ref, vc_ref, kl_ref, vl_ref, cos_ref, sin_ref, cosq_ref, sinq_ref, qn_ref, kn_ref,
         o_ref, k_scr, v_scr) = refs
    else:
        q_ref, kc_ref, vc_ref, qn_ref, kn_ref, o_ref, k_scr, v_scr = refs

    @pl.when(pl.program_id(2) == 0)
    def _():
        k_scr[0:tc, :] = (_rms(kc_ref[...]) * kn_ref[...]).astype(BF16)
        v_scr[0:tc, :] = vc_ref[...].astype(BF16)
        if latent:
            kl = _rope(_rms(kl_ref[...]) * kn_ref[...], cos_ref[...], sin_ref[...])
            k_scr[tc:, :] = kl.astype(BF16)
            v_scr[tc:, :] = vl_ref[...].astype(BF16)

    scale = HEAD_DIM ** -0.5
    for g in range(AT_Q_HEADS // AT_KV_HEADS):
        cols = slice(g * HEAD_DIM, (g + 1) * HEAD_DIM)
        q = _rms(q_ref[:, cols]) * qn_ref[...]
        if latent:
            q = _rope(q, cosq_ref[...], sinq_ref[...])
        s = lax.dot_general((q * scale).astype(BF16), k_scr[...], (((1,), (1,)), ((), ())),
                            preferred_element_type=F32)
        p = jnp.exp(s - jnp.max(s, axis=-1, keepdims=True))
        l = jnp.sum(p, axis=-1, keepdims=True)
        o_ref[:, cols] = (jnp.dot(p.astype(BF16), v_scr[...], preferred_element_type=F32) / l).astype(o_ref.dtype)


def _attention(zq, zc, zl, cos, sin, qn, kn):
    latent = zl is not None
    nb, tq_total, _ = zq.shape
    tc = zc.shape[1]
    tk = tc + (zl.shape[1] if latent else 0)
    gq = (AT_Q_HEADS // AT_KV_HEADS) * HEAD_DIM
    head_spec = lambda rows, col: pl.BlockSpec((None, rows, HEAD_DIM), lambda b, h, i: (b, 0, col // HEAD_DIM + h))
    vec_spec = pl.BlockSpec((1, HEAD_DIM), lambda b, h, i: (0, 0))
    in_specs = [pl.BlockSpec((None, TQ_ATTN, gq), lambda b, h, i: (b, i, C_AT_Q // gq + h)),
                head_spec(tc, C_AT_K), head_spec(tc, C_AT_V)]
    args = [zq, zc, zc]
    if latent:
        tl = zl.shape[1]
        in_specs += [head_spec(tl, C_AT_K), head_spec(tl, C_AT_V),
                     pl.BlockSpec((tl, HEAD_DIM), lambda b, h, i: (0, 0)),
                     pl.BlockSpec((tl, HEAD_DIM), lambda b, h, i: (0, 0)),
                     pl.BlockSpec((TQ_ATTN, HEAD_DIM), lambda b, h, i: (i, 0)),
                     pl.BlockSpec((TQ_ATTN, HEAD_DIM), lambda b, h, i: (i, 0))]
        args += [zl, zl, cos, sin, cos, sin]
    return pl.pallas_call(
        functools.partial(_attn_kernel, latent=latent, tc=tc),
        grid=(nb, AT_KV_HEADS, tq_total // TQ_ATTN),
        in_specs=in_specs + [vec_spec, vec_spec],
        out_specs=pl.BlockSpec((None, TQ_ATTN, gq), lambda b, h, i: (b, i, h)),
        out_shape=jax.ShapeDtypeStruct((nb, tq_total, BRANCH_W), BF16),
        scratch_shapes=[pltpu.VMEM((tk, HEAD_DIM), BF16), pltpu.VMEM((tk, HEAD_DIM), BF16)],
        compiler_params=_params(("parallel", "parallel", "arbitrary")),
        name="attention",
    )(*args, qn, kn)


def _merge_kernel(h_ref, xcol_ref, modcol_ref, y0_ref, y1_ref, y2_ref, y3_ref,
                  wg0_ref, wg1_ref, wg2_ref, wg3_ref, wb_ref, wo_ref, o_ref, m_scr):
    s = pl.program_id(2)
    n_gate = D_MODEL // TN_MERGE
    ys = (y0_ref, y1_ref, y2_ref, y3_ref)
    wgs = (wg0_ref, wg1_ref, wg2_ref, wg3_ref)

    @pl.when(s < n_gate)
    def _():
        h = h_ref[...]
        acc = jnp.zeros((TM_MERGE, TN_MERGE), F32)
        for n in range(N_BRANCH):
            gate = jnp.dot(h, wgs[n][...], preferred_element_type=F32)
            acc = acc + _sigmoid(gate) * jnp.dot(ys[n][...], wb_ref[n], preferred_element_type=F32)
        m_scr[s] = acc.astype(BF16)

    @pl.when(s >= n_gate)
    def _():
        acc = jnp.zeros((TM_MERGE, TN_OUT), F32)
        for kk in range(n_gate):
            acc = acc + jnp.dot(m_scr[kk], wo_ref[kk * TN_MERGE:(kk + 1) * TN_MERGE, :],
                                preferred_element_type=F32)
        o_ref[...] = xcol_ref[...] + modcol_ref[5:6, :] * acc


def _merge(x, h, mod, ys, w_gate, w_branch, w_out, mod_row):
    groups, rows, _ = x.shape
    n_gate = D_MODEL // TN_MERGE
    n_out = D_MODEL // TN_OUT
    mrow = (lambda b: b) if mod_row is None else (lambda b: mod_row)
    gate_col = lambda s: jnp.minimum(s, n_gate - 1)
    out_col = lambda s: jnp.maximum(s - n_gate, 0)
    y_spec = pl.BlockSpec((None, TM_MERGE, BRANCH_W), lambda b, i, s: (b, i, 0))
    wg_spec = lambda n: pl.BlockSpec((D_MODEL, TN_MERGE), lambda b, i, s: (0, n * n_gate + gate_col(s)))
    return pl.pallas_call(
        _merge_kernel,
        grid=(groups, rows // TM_MERGE, n_gate + n_out),
        in_specs=[pl.BlockSpec((None, TM_MERGE, D_MODEL), lambda b, i, s: (b, i, 0)),
                  pl.BlockSpec((None, TM_MERGE, TN_OUT), lambda b, i, s: (b, i, out_col(s))),
                  pl.BlockSpec((None, N_MOD, TN_OUT), lambda b, i, s: (mrow(b), 0, out_col(s))),
                  y_spec, y_spec, y_spec, y_spec,
                  wg_spec(0), wg_spec(1), wg_spec(2), wg_spec(3),
                  pl.BlockSpec((N_BRANCH, BRANCH_W, TN_MERGE), lambda b, i, s: (0, 0, gate_col(s))),
                  pl.BlockSpec((D_MODEL, TN_OUT), lambda b, i, s: (0, out_col(s)))],
        out_specs=pl.BlockSpec((None, TM_MERGE, TN_OUT), lambda b, i, s: (b, i, out_col(s))),
        out_shape=jax.ShapeDtypeStruct(x.shape, F32),
        scratch_shapes=[pltpu.VMEM((n_gate, TM_MERGE, TN_MERGE), BF16)],
        compiler_params=_params(("parallel", "parallel", "arbitrary")),
        name="merge",
    )(h, x, mod, *ys, w_gate, w_gate, w_gate, w_gate, w_branch, w_out)


def _rope_tables(seq):
    n = HEAD_DIM // 4
    inv = ROPE_THETA ** (-jnp.arange(n, dtype=F32) / n)
    pos = jnp.arange(seq)
    ar = (pos // GRID_W).astype(F32)[:, None] * inv
    ac = (pos % GRID_W).astype(F32)[:, None] * inv
    cos = jnp.concatenate([jnp.cos(ar), jnp.cos(ar), jnp.cos(ac), jnp.cos(ac)], axis=1)
    sin = jnp.concatenate([-jnp.sin(ar), jnp.sin(ar), -jnp.sin(ac), jnp.sin(ac)], axis=1)
    return cos, sin


def _gate_params(ml_if_bias, gd_a_log, gd_dt_bias):
    zeros = jnp.zeros((2, N_HEADS), F32)
    bias = jnp.concatenate([ml_if_bias.reshape(4, N_HEADS), zeros, gd_dt_bias], axis=0).T
    alog = jnp.concatenate([jnp.zeros((6, N_HEADS), F32), gd_a_log], axis=0).T
    pad = lambda a: jnp.pad(a.reshape(1, N_HEADS * N_GATE), ((0, 0), (0, LANES - N_HEADS * N_GATE)))
    col = lambda a: a.reshape(N_HEADS * N_GATE, 1)
    return pad(bias), pad(alog), col(bias), col(alog)


def _gate_weights(w_in):
    g = jnp.concatenate([w_in[:, R_ML_IF:R_ML_IF + 16], w_in[:, R_GD_BA:R_GD_BA + 16]], axis=1)
    g = g.reshape(D_MODEL, N_GATE, N_HEADS).transpose(0, 2, 1).reshape(D_MODEL, N_HEADS * N_GATE)
    w_gc = jnp.pad(g, ((0, 0), (0, LANES - N_HEADS * N_GATE)))
    w_gr = g.T
    return w_gc.astype(BF16), w_gr.astype(BF16)


def kernel(x, c, ctx, c_ctx, mod_w, mod_b, ffn1_norm, ffn1_w_in, ffn1_w_out, mix_norm, w_in, sg_norm, sg_w, sg_b, ml_if_bias, ml_norm, gd_conv, gd_a_log, gd_dt_bias, gd_norm, at_q_norm, at_k_norm, w_branch, w_out, ffn2_norm, ffn2_w_in, ffn2_w_out):
    nb, seq, _ = x.shape
    tc = ctx.shape[1]
    depth = mod_w.shape[0]
    ctx_row = nb
    cos, sin = _rope_tables(seq)

    cc = jnp.zeros((MOD_ROWS, D_MODEL), F32).at[:nb].set(c).at[ctx_row].set(c_ctx)
    mods = _modulation(cc, mod_w, mod_b).reshape(depth, MOD_ROWS, N_MOD, D_MODEL)

    ffn_bf16 = [_to_bf16(w) for w in (ffn1_w_in, ffn1_w_out, ffn2_w_in, ffn2_w_out)]
    w_main_all, w_gate_all = _split_w_in(w_in)
    flat = lambda a: a.reshape(1, nb * tc, a.shape[-1])
    unflat = lambda a: a.reshape(nb, tc, a.shape[-1])
    xc = ctx
    for l in range(depth):
        last = l == depth - 1
        ctx_out = not last
        mod = mods[l]
        w_main, w_gate = w_main_all[l], w_gate_all[l]
        w_gc, w_gr = _gate_weights(w_in[l])
        bias_c, alog_c, bias_r, alog_r = _gate_params(ml_if_bias[l], gd_a_log[l], gd_dt_bias[l])
        f1_in, f1_out = ffn_bf16[0][l], ffn_bf16[1][l]
        f2_in, f2_out = ffn_bf16[2][l], ffn_bf16[3][l]
        wb, wo = w_branch[l].astype(BF16), w_out[l].astype(BF16)
        sgw = sg_w[l].astype(BF16)
        sg_bias = jnp.repeat(sg_b[l].T, BRANCH_W // SG_GROUPS, axis=1)
        sgn = sg_norm[l].reshape(1, BRANCH_W)
        mln = ml_norm[l].reshape(1, BRANCH_W)
        gdn_g = gd_norm[l].reshape(1, HEAD_DIM)
        qn, kn = at_q_norm[l].reshape(1, HEAD_DIM), at_k_norm[l].reshape(1, HEAD_DIM)

        x = _ffn(x, mod, ffn1_norm[l], f1_in, f1_out, 0, None)
        xc = unflat(_ffn(flat(xc), mod, ffn1_norm[l], f1_in, f1_out, 0, ctx_row))

        zl, gcl, grl, hl = _inproj(x, mod, mix_norm[l], w_main, w_gc, w_gr, N_MAIN, None)
        zc, gcc, grc, hc = _inproj(flat(xc), mod, mix_norm[l], w_main, w_gc, w_gr,
                                   N_MAIN if ctx_out else KV_MAIN, ctx_row)
        zc = unflat(zc)
        gcc = unflat(gcc)
        grc = grc.reshape(N_HEADS * N_GATE, nb, tc).transpose(1, 0, 2)
        gpl, gptl = _gateprep(gcl, grl, bias_c, alog_c, bias_r, alog_r)
        gpc, gptc = _gateprep(gcc, grc, bias_c, alog_c, bias_r, alog_r)

        y_sg_l = _spatial_gating(zl, sgn, sgw, sg_bias)
        y_ml_c, y_ml_l = _mlstm(zc, gpc, gptc, zl, gpl, gptl, mln, ctx_out)
        y_gd_c, y_gd_l = _gdn(zc, gpc, gptc, zl, gpl, gptl, gd_conv[l], gdn_g, ctx_out)
        y_at_l = _attention(zl, zc, zl, cos, sin, qn, kn)
        x = _merge(x, hl, mod, [y_sg_l, y_ml_l, y_gd_l, y_at_l], w_gate, wb, wo, None)
        if ctx_out:
            y_sg_c = _spatial_gating(zc, sgn, sgw, sg_bias)
            y_at_c = _attention(zc, zc, None, None, None, qn, kn)
            xc = unflat(_merge(flat(xc), hc, mod, [flat(y) for y in (y_sg_c, y_ml_c, y_gd_c, y_at_c)],
                               w_gate, wb, wo, ctx_row))

        x = _ffn(x, mod, ffn2_norm[l], f2_in, f2_out, 6, None)
        if ctx_out:
            xc = unflat(_ffn(flat(xc), mod, ffn2_norm[l], f2_in, f2_out, 6, ctx_row))
    return x
```

```python
import functools
import math

import jax
import jax.numpy as jnp
from jax import lax
from jax.experimental import pallas as pl
from jax.experimental.pallas import tpu as pltpu

F32 = jnp.float32
BF16 = jnp.bfloat16
EPS = 1e-6

D_MODEL = 2048
D_FF = 5632
N_MOD = 9
N_BRANCH = 4
BRANCH_W = 512
HEAD_DIM = 128
N_HEADS = 4
SG_GROUPS = 4
ML_CHUNK = 128
GD_CHUNK = 64
GD_SUPER = 256
GD_PREP_GROUP = 2
AT_Q_HEADS = 4
AT_KV_HEADS = 2
GRID_W = 64
ROPE_THETA = 10000.0
FFN_RESIDUAL = 0.5

LANES = 128
N_GATE = 8
MOD_ROWS = 16

C_ML_K, C_ML_V = 0, 512
C_GD_K, C_GD_V = 1024, 1536
C_AT_K, C_AT_V = 2048, 2304
KV_MAIN = 2560
C_SG_U, C_SG_V = 2560, 3072
C_ML_Q, C_ML_O = 3584, 4096
C_GD_Q, C_GD_Z = 4608, 5120
C_AT_Q = 5632
N_MAIN = 6144
R_ML_IF = 1024
R_GD_KV = 1040
R_GD_BA = 2064
R_AT_KV = 2080
R_REST = 2592
R_GATE = 6176

VMEM_LIMIT = 56 * 1024 * 1024

TM_FFN = 512
TF_FFN = 512
TM_PROJ = 1024
TN_PROJ = 512
TM_MERGE = 1024
TN_MERGE = 256
TN_OUT = 512
TQ_ATTN = 256
TR_SG = 256
TG_GATE = 256
CAST_BLOCK_BYTES = 8 * 1024 * 1024
TK_MOD = 128


def _params(sem):
    return pltpu.CompilerParams(dimension_semantics=sem, vmem_limit_bytes=VMEM_LIMIT)


def _mm(a, b):
    return jnp.dot(a.astype(BF16), b.astype(BF16), preferred_element_type=F32)


def _mm_nt(a, b):
    return lax.dot_general(a.astype(BF16), b.astype(BF16), (((1,), (1,)), ((), ())),
                           preferred_element_type=F32)


def _mm_tn(a, b):
    return lax.dot_general(a.astype(BF16), b.astype(BF16), (((0,), (0,)), ((), ())),
                           preferred_element_type=F32)


def _sigmoid(x):
    return 1.0 / (1.0 + jnp.exp(-x))


def _silu(x):
    return x * _sigmoid(x)


def _softplus(x):
    return jnp.maximum(x, 0.0) + jnp.log1p(jnp.exp(-jnp.abs(x)))


def _gelu_tanh(x):
    c = math.sqrt(2.0 / math.pi)
    return x * (0.5 * (1.0 + jnp.tanh(c * (x + 0.044715 * (x * x * x)))))


def _rms(x):
    return x * lax.rsqrt(jnp.mean(x * x, axis=-1, keepdims=True) + EPS)


def _norm_mod(x, gain, shift, scale):
    return (_rms(x) * gain) * (1.0 + scale) + shift


def _cast_kernel(x_ref, o_ref):
    o_ref[...] = x_ref[...].astype(o_ref.dtype)


def _to_bf16(w):
    depth, r, c = w.shape
    rows = 1 << (max(16, min(r, CAST_BLOCK_BYTES // (4 * c))).bit_length() - 1)
    while r % rows:
        rows //= 2
    return pl.pallas_call(
        _cast_kernel,
        grid=(depth, r // rows),
        in_specs=[pl.BlockSpec((None, rows, c), lambda l, i: (l, i, 0))],
        out_specs=pl.BlockSpec((None, rows, c), lambda l, i: (l, i, 0)),
        out_shape=jax.ShapeDtypeStruct(w.shape, BF16),
        compiler_params=_params(("parallel", "parallel")),
        name="cast_bf16",
    )(w)


def _mod_kernel(cc_ref, w_ref, b_ref, o_ref):
    @pl.when(pl.program_id(1) == 0)
    def _():
        o_ref[...] = jnp.broadcast_to(b_ref[...], o_ref.shape)

    o_ref[...] += _mm(_silu(cc_ref[...]), w_ref[...])


def _modulation(cc, mod_w, mod_b):
    depth, d, n = mod_w.shape
    ccs = cc.reshape(MOD_ROWS, d // TK_MOD, TK_MOD).transpose(1, 0, 2)
    return pl.pallas_call(
        _mod_kernel,
        grid=(depth, d // TK_MOD),
        in_specs=[pl.BlockSpec((None, MOD_ROWS, TK_MOD), lambda l, k: (k, 0, 0)),
                  pl.BlockSpec((None, TK_MOD, n), lambda l, k: (l, k, 0)),
                  pl.BlockSpec((None, 1, n), lambda l, k: (l, 0, 0))],
        out_specs=pl.BlockSpec((None, MOD_ROWS, n), lambda l, k: (l, 0, 0)),
        out_shape=jax.ShapeDtypeStruct((depth, MOD_ROWS, n), F32),
        compiler_params=_params(("parallel", "arbitrary")),
        name="modulation",
    )(ccs, mod_w, mod_b.reshape(depth, 1, n))


def _ffn_kernel(x_ref, mod_ref, g_ref, wa_ref, wb_ref, wo_ref, o_ref, h_scr, acc_scr, *, base):
    j = pl.program_id(2)

    @pl.when(j == 0)
    def _():
        h = _norm_mod(x_ref[...], g_ref[...], mod_ref[base:base + 1, :], mod_ref[base + 1:base + 2, :])
        h_scr[...] = h.astype(BF16)
        acc_scr[...] = jnp.zeros_like(acc_scr)

    h = h_scr[...]
    a = jnp.dot(h, wa_ref[...], preferred_element_type=F32)
    b = jnp.dot(h, wb_ref[...], preferred_element_type=F32)
    acc_scr[...] += _mm(_silu(a) * b, wo_ref[...])

    @pl.when(j == pl.num_programs(2) - 1)
    def _():
        o_ref[...] = x_ref[...] + (FFN_RESIDUAL * mod_ref[base + 2:base + 3, :]) * acc_scr[...]


def _ffn(x, mod, gain, w_in, w_out, layer, base, mod_row):
    groups, rows, _ = x.shape
    nj = D_FF // TF_FFN
    mod_idx = (lambda b, i, j: (b, 0, 0)) if mod_row is None else (lambda b, i, j: (mod_row, 0, 0))
    return pl.pallas_call(
        functools.partial(_ffn_kernel, base=base),
        grid=(groups, rows // TM_FFN, nj),
        in_specs=[pl.BlockSpec((None, TM_FFN, D_MODEL), lambda b, i, j: (b, i, 0)),
                  pl.BlockSpec((None, N_MOD, D_MODEL), mod_idx),
                  pl.BlockSpec((1, D_MODEL), lambda b, i, j: (0, 0)),
                  pl.BlockSpec((None, D_MODEL, TF_FFN), lambda b, i, j: (layer, 0, j)),
                  pl.BlockSpec((None, D_MODEL, TF_FFN), lambda b, i, j: (layer, 0, j + nj)),
                  pl.BlockSpec((None, TF_FFN, D_MODEL), lambda b, i, j: (layer, j, 0))],
        out_specs=pl.BlockSpec((None, TM_FFN, D_MODEL), lambda b, i, j: (b, i, 0)),
        out_shape=jax.ShapeDtypeStruct(x.shape, F32),
        scratch_shapes=[pltpu.VMEM((TM_FFN, D_MODEL), BF16), pltpu.VMEM((TM_FFN, D_MODEL), F32)],
        compiler_params=_params(("parallel", "parallel", "arbitrary")),
        name="ffn",
    )(x, mod, gain.reshape(1, D_MODEL), w_in, w_in, w_out)


def _inproj_kernel(x_ref, mod_ref, g_ref, w_ref, wgc_ref, wgr_ref, z_ref, gc_ref, gr_ref, h_ref):
    n = pl.program_id(2)

    @pl.when(n == 0)
    def _():
        h = _norm_mod(x_ref[...], g_ref[...], mod_ref[3:4, :], mod_ref[4:5, :]).astype(BF16)
        h_ref[...] = h
        gc_ref[...] = jnp.dot(h, wgc_ref[...], preferred_element_type=F32)
        gr_ref[...] = lax.dot_general(wgr_ref[...], h, (((1,), (1,)), ((), ())), preferred_element_type=F32)

    z_ref[...] = jnp.dot(h_ref[...], w_ref[...], preferred_element_type=F32)


def _inproj(x, mod, gain, w_main, layer, w_gc, w_gr, n_cols, mod_row):
    groups, rows, _ = x.shape
    mod_idx = (lambda b, i, n: (b, 0, 0)) if mod_row is None else (lambda b, i, n: (mod_row, 0, 0))
    return pl.pallas_call(
        _inproj_kernel,
        grid=(groups, rows // TM_PROJ, n_cols // TN_PROJ),
        in_specs=[pl.BlockSpec((None, TM_PROJ, D_MODEL), lambda b, i, n: (b, i, 0)),
                  pl.BlockSpec((None, N_MOD, D_MODEL), mod_idx),
                  pl.BlockSpec((1, D_MODEL), lambda b, i, n: (0, 0)),
                  pl.BlockSpec((None, D_MODEL, TN_PROJ), lambda b, i, n: (layer, 0, n)),
                  pl.BlockSpec((D_MODEL, LANES), lambda b, i, n: (0, 0)),
                  pl.BlockSpec((N_HEADS * N_GATE, D_MODEL), lambda b, i, n: (0, 0))],
        out_specs=[pl.BlockSpec((None, TM_PROJ, TN_PROJ), lambda b, i, n: (b, i, n)),
                   pl.BlockSpec((None, TM_PROJ, LANES), lambda b, i, n: (b, i, 0)),
                   pl.BlockSpec((None, N_HEADS * N_GATE, TM_PROJ), lambda b, i, n: (b, 0, i)),
                   pl.BlockSpec((None, TM_PROJ, D_MODEL), lambda b, i, n: (b, i, 0))],
        out_shape=[jax.ShapeDtypeStruct((groups, rows, n_cols), F32),
                   jax.ShapeDtypeStruct((groups, rows, LANES), F32),
                   jax.ShapeDtypeStruct((groups, N_HEADS * N_GATE, rows), F32),
                   jax.ShapeDtypeStruct((groups, rows, D_MODEL), BF16)],
        compiler_params=_params(("parallel", "parallel", "arbitrary")),
        name="inproj",
    )(x, mod, gain.reshape(1, D_MODEL), w_main, w_gc, w_gr)


def _split3(e):
    e1 = e.astype(BF16)
    r1 = e - e1.astype(F32)
    e2 = r1.astype(BF16)
    e3 = (r1 - e2.astype(F32)).astype(BF16)
    return e1, e2, e3


def _gate_values(pre, a_log, cls):
    neg_sp = -_softplus(-pre)
    return jnp.where(cls < 2, pre,
                     jnp.where(cls < 4, neg_sp,
                               jnp.where(cls < 6, _sigmoid(pre), -jnp.exp(a_log) * _softplus(pre))))


def _gateprep_kernel(gc_ref, gr_ref, bc_ref, ac_ref, br_ref, ar_ref, oc_ref, or_ref):
    for i in range(TG_GATE // ML_CHUNK):
        tile = slice(i * ML_CHUNK, (i + 1) * ML_CHUNK)
        out_c, or_ref[i] = _gateprep_tile(gc_ref[tile, :], gr_ref[:, tile], bc_ref[...], ac_ref[...],
                                          br_ref[...], ar_ref[...])
        for h in range(N_HEADS):
            oc_ref[tile, h * LANES:(h + 1) * LANES] = (
                out_c if h == 0 else pltpu.roll(out_c, LANES - h * N_GATE, 1))


def _gateprep_tile(gc, gr, bias_c, alog_c, bias_r, alog_r):
    L = ML_CHUNK
    half = GD_CHUNK
    pre = gc + bias_c
    cls = lax.broadcasted_iota(jnp.int32, pre.shape, 1) & (N_GATE - 1)
    t = lax.broadcasted_iota(jnp.int32, pre.shape, 0)
    e = _gate_values(pre, alog_c, cls)
    tri = (lax.broadcasted_iota(jnp.int32, (L, L), 1) <= lax.broadcasted_iota(jnp.int32, (L, L), 0)).astype(BF16)
    e1, e2, e3 = _split3(e)
    p = (jnp.dot(tri, e1, preferred_element_type=F32) + jnp.dot(tri, e2, preferred_element_type=F32)
         + jnp.dot(tri, e3, preferred_element_type=F32))
    tot = p[L - 1:L, :]
    mid = p[half - 1:half, :]
    second = t >= half
    pre64 = p - jnp.where(second, mid, 0.0)
    suf64 = jnp.where(second, tot, mid) - p + e
    cum = jnp.where(cls == 2, p, jnp.where(cls == 3, tot - p + e, jnp.where(cls == 6, pre64, suf64)))
    out_c = jnp.where((cls == 2) | (cls == 3) | (cls == 6) | (cls == 7), cum, e)

    pre_r = gr + bias_r
    cls_r = lax.broadcasted_iota(jnp.int32, pre_r.shape, 0) & (N_GATE - 1)
    t_r = lax.broadcasted_iota(jnp.int32, pre_r.shape, 1)
    e_r = _gate_values(pre_r, alog_r, cls_r)
    tri_r = (lax.broadcasted_iota(jnp.int32, (L, L), 0) <= lax.broadcasted_iota(jnp.int32, (L, L), 1)).astype(BF16)
    f1, f2, f3 = _split3(e_r)
    pr = (jnp.dot(f1, tri_r, preferred_element_type=F32) + jnp.dot(f2, tri_r, preferred_element_type=F32)
          + jnp.dot(f3, tri_r, preferred_element_type=F32))
    tot_r = pr[:, L - 1:L]
    mid_r = pr[:, half - 1:half]
    second_r = t_r >= half
    pre64_r = pr - jnp.where(second_r, mid_r, 0.0)
    suf64_r = jnp.where(second_r, tot_r, mid_r) - pr + e_r
    cum_r = jnp.where(cls_r == 2, pr, jnp.where(cls_r == 3, tot_r - pr + e_r, jnp.where(cls_r == 6, pre64_r, suf64_r)))
    return out_c, jnp.where((cls_r == 2) | (cls_r == 3) | (cls_r == 6) | (cls_r == 7), cum_r, e_r)


def _gateprep(gc, gr, bias_c, alog_c, bias_r, alog_r):
    groups, rows, _ = gc.shape
    nt = rows // ML_CHUNK
    hw = N_HEADS * LANES
    hg = N_HEADS * N_GATE
    tiles = TG_GATE // ML_CHUNK
    return pl.pallas_call(
        _gateprep_kernel,
        grid=(groups, rows // TG_GATE),
        in_specs=[pl.BlockSpec((None, TG_GATE, LANES), lambda b, c: (b, c, 0)),
                  pl.BlockSpec((None, hg, TG_GATE), lambda b, c: (b, 0, c)),
                  pl.BlockSpec((1, LANES), lambda b, c: (0, 0)),
                  pl.BlockSpec((1, LANES), lambda b, c: (0, 0)),
                  pl.BlockSpec((hg, 1), lambda b, c: (0, 0)),
                  pl.BlockSpec((hg, 1), lambda b, c: (0, 0))],
        out_specs=[pl.BlockSpec((None, TG_GATE, hw), lambda b, c: (b, c, 0)),
                   pl.BlockSpec((None, tiles, hg, ML_CHUNK), lambda b, c: (b, c, 0, 0))],
        out_shape=[jax.ShapeDtypeStruct((groups, rows, hw), F32),
                   jax.ShapeDtypeStruct((groups, nt, hg, ML_CHUNK), F32)],
        compiler_params=_params(("parallel", "parallel")),
        name="gateprep",
    )(gc, gr, bias_c, alog_c, bias_r, alog_r)


def _sg_kernel(u_ref, v_ref, gn_ref, w_ref, bias_ref, o_ref):
    gw = BRANCH_W // SG_GROUPS
    for ch in range(TR_SG // ML_CHUNK):
        rows = slice(ch * ML_CHUNK, (ch + 1) * ML_CHUNK)
        v = _rms(_gelu_tanh(v_ref[rows, :])) * gn_ref[...]
        u = _gelu_tanh(u_ref[rows, :])
        for g in range(SG_GROUPS):
            cols = slice(g * gw, (g + 1) * gw)
            s = _mm(w_ref[g], v[:, cols]) + bias_ref[:, cols]
            o_ref[rows, cols] = (u[:, cols] * s).astype(o_ref.dtype)


def _spatial_gating(z, gn, w_s, bias_full):
    groups, rows, _ = z.shape
    ub, vb = C_SG_U // BRANCH_W, C_SG_V // BRANCH_W
    return pl.pallas_call(
        _sg_kernel,
        grid=(groups, rows // TR_SG),
        in_specs=[pl.BlockSpec((None, TR_SG, BRANCH_W), lambda b, i: (b, i, ub)),
                  pl.BlockSpec((None, TR_SG, BRANCH_W), lambda b, i: (b, i, vb)),
                  pl.BlockSpec((1, BRANCH_W), lambda b, i: (0, 0)),
                  pl.BlockSpec((SG_GROUPS, ML_CHUNK, ML_CHUNK), lambda b, i: (0, 0, 0)),
                  pl.BlockSpec((ML_CHUNK, BRANCH_W), lambda b, i: (0, 0))],
        out_specs=pl.BlockSpec((None, TR_SG, BRANCH_W), lambda b, i: (b, i, 0)),
        out_shape=jax.ShapeDtypeStruct((groups, rows, BRANCH_W), BF16),
        compiler_params=_params(("parallel", "parallel")),
        name="spatial_gating",
    )(z, z, gn, w_s, bias_full)


def _mlstm_chunk(q, k, v, gp, gpt, state, rev, with_out):
    L = ML_CHUNK
    j_i, j_b = (1, 3) if rev else (0, 2)
    li_c = gp[:, j_i:j_i + 1]
    b_c = gp[:, j_b:j_b + 1]
    b_end = b_c[0:1, :] if rev else b_c[L - 1:L, :]
    ks = k * (HEAD_DIM ** -0.5)
    a = b_end - b_c + li_c
    m_loc = jnp.max(a, axis=0, keepdims=True)
    wgt = jnp.broadcast_to(jnp.exp(a - m_loc), (L, HEAD_DIM))
    ct_loc = _mm_tn(ks, jnp.concatenate([v * wgt, wgt], axis=1))
    ct, m = state
    m_new = jnp.maximum(b_end + m, m_loc)
    sp = jnp.exp(b_end + m - m_new)
    sl = jnp.exp(m_loc - m_new)
    new_state = (sp * ct + sl * ct_loc, m_new)
    if not with_out:
        return new_state, None
    li_r = gpt[j_i:j_i + 1, :]
    b_r = gpt[j_b:j_b + 1, :]
    t = lax.broadcasted_iota(jnp.int32, (L, L), 0)
    s = lax.broadcasted_iota(jnp.int32, (L, L), 1)
    incl = (s >= t) if rev else (s <= t)
    e = jnp.where(incl, li_r - b_r, -jnp.inf)
    c = jnp.broadcast_to(jnp.maximum(jnp.max(e, axis=1, keepdims=True), m), (L, L))
    p = jnp.exp(e - c) * _mm_nt(q, ks)
    inter = jnp.exp(m - c)
    pv = _mm(p, jnp.concatenate([v, jnp.ones_like(v)], axis=1))
    qc = _mm(q, ct)
    num = pv[:, :HEAD_DIM] + inter * qc[:, :HEAD_DIM]
    den = pv[:, HEAD_DIM:] + inter * qc[:, HEAD_DIM:]
    m_t = jnp.broadcast_to(b_c, (L, L)) + c
    return new_state, num / jnp.maximum(jnp.abs(den), jnp.exp(-m_t))


def _mlstm_kernel(*refs, ctx_out, n_ctx, n_lat):
    if ctx_out:
        (kc_ref, vc_ref, gpc_ref, gptc_ref, qc_ref, oc_ref, kl_ref, vl_ref, gpl_ref, gptl_ref, ql_ref, ol_ref,
         norm_ref, yc_ref, yl_ref, hfc, hbc, hfl, hbl) = refs
    else:
        (kc_ref, vc_ref, gpc_ref, gptc_ref, kl_ref, vl_ref, gpl_ref, gptl_ref, ql_ref, ol_ref,
         norm_ref, yl_ref, hfl, hbl) = refs
        qc_ref = oc_ref = yc_ref = hfc = hbc = None
    L = ML_CHUNK

    def run(k_ref, v_ref, gp_ref, gpt_ref, q_ref, hf, hb, n_chunks, st_f, st_b, with_out):
        def body(j, carry):
            st_f, st_b = carry
            rf = pl.ds(pl.multiple_of(j * L, L), L)
            jb = n_chunks - 1 - j
            rb = pl.ds(pl.multiple_of(jb * L, L), L)
            st_f, h_f = _mlstm_chunk(q_ref[rf, :] if with_out else None, k_ref[rf, :], v_ref[rf, :],
                                     gp_ref[rf, :], gpt_ref[j], st_f, False, with_out)
            st_b, h_b = _mlstm_chunk(q_ref[rb, :] if with_out else None, k_ref[rb, :], v_ref[rb, :],
                                     gp_ref[rb, :], gpt_ref[jb], st_b, True, with_out)
            if with_out:
                hf[rf, :] = h_f
                hb[rb, :] = h_b
            return st_f, st_b
        return lax.fori_loop(0, n_chunks, body, (st_f, st_b))

    zero = (jnp.zeros((HEAD_DIM, 2 * HEAD_DIM), F32), jnp.zeros((1, 1), F32))
    st_f, st_b = run(kc_ref, vc_ref, gpc_ref, gptc_ref, qc_ref, hfc, hbc, n_ctx, zero, zero, ctx_out)
    run(kl_ref, vl_ref, gpl_ref, gptl_ref, ql_ref, hfl, hbl, n_lat, st_f, st_b, True)

    def finish(hf, hb, o_ref, y_ref):
        y_ref[...] = (_sigmoid(o_ref[...]) * (_rms(hf[...] + hb[...]) * norm_ref[...])).astype(y_ref.dtype)

    finish(hfl, hbl, ol_ref, yl_ref)
    if ctx_out:
        finish(hfc, hbc, oc_ref, yc_ref)


def _seq_spec(rows, col):
    return pl.BlockSpec((None, rows, HEAD_DIM), lambda b, h: (b, 0, col // HEAD_DIM + h))


def _gp_specs(rows):
    return [pl.BlockSpec((None, rows, LANES), lambda b, h: (b, 0, h)),
            pl.BlockSpec((None, rows // ML_CHUNK, N_GATE, ML_CHUNK), lambda b, h: (b, 0, h, 0))]


def _mlstm(zc, gpc, gptc, zl, gpl, gptl, norm, ctx_out):
    nb, tc, _ = zc.shape
    tl = zl.shape[1]
    ctx_in = [zc, zc, gpc, gptc] + ([zc, zc] if ctx_out else [])
    ctx_specs = [_seq_spec(tc, C_ML_K), _seq_spec(tc, C_ML_V)] + _gp_specs(tc) + (
        [_seq_spec(tc, C_ML_Q), _seq_spec(tc, C_ML_O)] if ctx_out else [])
    lat_in = [zl, zl, gpl, gptl, zl, zl]
    lat_specs = [_seq_spec(tl, C_ML_K), _seq_spec(tl, C_ML_V)] + _gp_specs(tl) + [
        _seq_spec(tl, C_ML_Q), _seq_spec(tl, C_ML_O)]
    out_spec = lambda rows: pl.BlockSpec((None, rows, HEAD_DIM), lambda b, h: (b, 0, h))
    out_shape = lambda rows: jax.ShapeDtypeStruct((nb, rows, BRANCH_W), BF16)
    scr = lambda rows: [pltpu.VMEM((rows, HEAD_DIM), F32), pltpu.VMEM((rows, HEAD_DIM), F32)]
    res = pl.pallas_call(
        functools.partial(_mlstm_kernel, ctx_out=ctx_out, n_ctx=tc // ML_CHUNK, n_lat=tl // ML_CHUNK),
        grid=(nb, N_HEADS),
        in_specs=ctx_specs + lat_specs + [pl.BlockSpec((1, HEAD_DIM), lambda b, h: (0, h))],
        out_specs=([out_spec(tc)] if ctx_out else []) + [out_spec(tl)],
        out_shape=([out_shape(tc)] if ctx_out else []) + [out_shape(tl)],
        scratch_shapes=(scr(tc) if ctx_out else []) + scr(tl),
        compiler_params=_params(("parallel", "parallel")),
        name="mlstm",
    )(*ctx_in, *lat_in, norm)
    return (res[0], res[1]) if ctx_out else (None, res[0])


def _conv_silu(x, w3):
    rows = x.shape[0]
    t = lax.broadcasted_iota(jnp.int32, x.shape, 0)
    prev = jnp.where(t == 0, 0.0, pltpu.roll(x, 1, 0))
    nxt = jnp.where(t == rows - 1, 0.0, pltpu.roll(x, rows - 1, 0))
    return _silu(w3[0:1, :] * prev + w3[1:2, :] * x + w3[2:3, :] * nxt)


def _l2norm(x):
    return x * lax.rsqrt(jnp.sum(x * x, axis=-1, keepdims=True) + EPS)


def _chunk_totals(gc_r, rev):
    L = GD_CHUNK
    pos = lax.broadcasted_iota(jnp.int32, gc_r.shape, 1)
    tot = jnp.zeros_like(gc_r)
    for c in range(GD_SUPER // L):
        i = c * L if rev else c * L + L - 1
        tot = jnp.where((pos >= c * L) & (pos < (c + 1) * L), gc_r[:, i:i + 1], tot)
    return tot


def _gdn_prepare(blocks):
    t = lax.broadcasted_iota(jnp.int32, (GD_SUPER, GD_SUPER), 0)
    s = lax.broadcasted_iota(jnp.int32, (GD_SUPER, GD_SUPER), 1)
    x = t ^ s
    same = x < GD_CHUNK
    kk0 = [_mm_nt(k, k) for k, _, _, _, _ in blocks]
    qk0 = [None if qs is None else _mm_nt(qs, k) for k, _, qs, _, _ in blocks]
    kT = [k.T for k, _, _, _, _ in blocks]
    prob = []
    for i, (k, v, qs, gp, gpt) in enumerate(blocks):
        for rev in (False, True):
            j_b, j_g = (5, 7) if rev else (4, 6)
            beta = gp[:, j_b:j_b + 1]
            gc_c = gp[:, j_g:j_g + 1]
            gc_r = gpt[j_g:j_g + 1, :]
            incl = same & ((s >= t) if rev else (s <= t))
            strict = same & ((s > t) if rev else (s < t))
            decay = jnp.exp(jnp.where(incl, gc_c - gc_r, -jnp.inf))
            m = jnp.where(strict, (beta * kk0[i]) * decay, 0.0)
            prob.append(dict(i=i, rev=rev, beta=beta, gc_r=gc_r, decay=decay, m=m, eg=jnp.exp(gc_c)))
    tinv = [jnp.where(x == 0, 1.0, 0.0) - jnp.where(x == 1, p["m"], 0.0) for p in prob]
    for lvl in range(1, 6):
        tc = [_mm(tinv[n], jnp.where((x >> lvl) == 1, p["m"], 0.0)) for n, p in enumerate(prob)]
        tinv = [tinv[n] - _mm(tc[n], tinv[n]) for n in range(len(prob))]
    out = [[] for _ in blocks]
    for n, p in enumerate(prob):
        k, v, qs, _, _ = blocks[p["i"]]
        y = _mm(tinv[n], jnp.concatenate([v * p["beta"], (k * p["beta"]) * p["eg"]], axis=1))
        kendT = (kT[p["i"]] * jnp.exp(_chunk_totals(p["gc_r"], p["rev"]) - p["gc_r"])).astype(BF16)
        qd = None if qs is None else (qs * p["eg"]).astype(BF16)
        qk = None if qs is None else (qk0[p["i"]] * p["decay"]).astype(BF16)
        out[p["i"]].append((y[:, :HEAD_DIM], y[:, HEAD_DIM:].astype(BF16), kendT, qd, qk))
    return out


def _gdn_kernel(*refs, ctx_out, n_ctx, n_lat):
    n_in = 16 if ctx_out else 14
    n_out = 2 if ctx_out else 1
    if ctx_out:
        (kc_ref, vc_ref, gpc_ref, gptc_ref, qc_ref, zc_ref, kl_ref, vl_ref, gpl_ref, gptl_ref, ql_ref, zl_ref,
         wq_ref, wk_ref, wv_ref, norm_ref, yc_ref, yl_ref) = refs[:n_in + n_out]
    else:
        (kc_ref, vc_ref, gpc_ref, gptc_ref, kl_ref, vl_ref, gpl_ref, gptl_ref, ql_ref, zl_ref,
         wq_ref, wk_ref, wv_ref, norm_ref, yl_ref) = refs[:n_in + n_out]
        qc_ref = zc_ref = yc_ref = None
    ks, vs, qs, gpa, gpta, o_f, o_b = refs[n_in + n_out:n_in + n_out + 7]
    per_dir = refs[n_in + n_out + 7:]
    u_s, w_s, kT_s, qd_s, qk_s = (per_dir[0:2], per_dir[2:4], per_dir[4:6], per_dir[6:8], per_dir[8:10])
    L = GD_CHUNK
    SB = GD_SUPER
    tc = n_ctx * SB
    tiles = SB // ML_CHUNK

    ks[0:tc, :] = _l2norm(_conv_silu(kc_ref[...], wk_ref[...]))
    vs[0:tc, :] = _conv_silu(vc_ref[...], wv_ref[...])
    ks[tc:, :] = _l2norm(_conv_silu(kl_ref[...], wk_ref[...]))
    vs[tc:, :] = _conv_silu(vl_ref[...], wv_ref[...])
    qs[tc:, :] = _l2norm(_conv_silu(ql_ref[...], wq_ref[...])) * (HEAD_DIM ** -0.5)
    if ctx_out:
        qs[0:tc, :] = _l2norm(_conv_silu(qc_ref[...], wq_ref[...])) * (HEAD_DIM ** -0.5)
    gpa[0:tc, :] = gpc_ref[...]
    gpa[tc:, :] = gpl_ref[...]
    gpta[0:n_ctx * tiles] = gptc_ref[...]
    gpta[n_ctx * tiles:] = gptl_ref[...]

    def prepare(js, with_out):
        rows = [pl.ds(pl.multiple_of(j * SB, SB), SB) for j in js]
        blocks = [(ks[r, :], vs[r, :], qs[r, :] if with_out else None, gpa[r, :],
                   jnp.concatenate([gpta[j * tiles + i] for i in range(tiles)], axis=1))
                  for j, r in zip(js, rows)]
        for j, r, res in zip(js, rows, _gdn_prepare(blocks)):
            for d, (u, w, kT, qd, qk) in enumerate(res):
                u_s[d][r, :] = u
                w_s[d][r, :] = w
                kT_s[d][j] = kT
                if with_out:
                    qd_s[d][r, :] = qd
                    qk_s[d][r, :] = qk

    def prepare_loop(lo, n, with_out):
        group = GD_PREP_GROUP if n % GD_PREP_GROUP == 0 else 1

        def body(i, carry):
            prepare([lo + i * group + g for g in range(group)], with_out)
            return carry
        lax.fori_loop(0, n // group, body, 0)

    def chunk_rows(j, c):
        return pl.ds(pl.multiple_of(j * SB + c * L, L), L)

    def step_load(d, j, c, with_out):
        rows = chunk_rows(j, c)
        cols = slice(c * L, (c + 1) * L)
        edge = gpa[pl.ds(pl.multiple_of(j * SB + c * L, L) + (0 if d else L - 1), 1), :]
        g_end = jnp.exp(edge[:, 7:8] if d else edge[:, 6:7])
        out_ops = (qd_s[d][rows, :], qk_s[d][rows, cols]) if with_out else None
        return g_end, u_s[d][rows, :], w_s[d][rows, :], kT_s[d][j, :, cols], out_ops

    def step_compute(ops, S):
        g_end, u, w, kT, out_ops = ops
        Sb = S.astype(BF16)
        v_new = (u - jnp.dot(w, Sb, preferred_element_type=F32)).astype(BF16)
        out = None
        if out_ops is not None:
            out = (jnp.dot(out_ops[0], Sb, preferred_element_type=F32)
                   + jnp.dot(out_ops[1], v_new, preferred_element_type=F32))
        return S * g_end + jnp.dot(kT, v_new, preferred_element_type=F32), out

    def chunk_pair(lo, n, i, c):
        return (lo + i, c), (lo + n - 1 - i, SB // L - 1 - c)

    def scan(lo, n, first, last, carry, with_out):
        def body(i, carry):
            for c in range(SB // L):
                where = chunk_pair(lo, n, i, c)
                res = [step_compute(step_load(d, *where[d], with_out), carry[d]) for d in range(2)]
                if with_out:
                    o_f[chunk_rows(*where[0]), :] = res[0][1]
                    o_b[chunk_rows(*where[1]), :] = res[1][1]
                carry = (res[0][0], res[1][0])
            return carry
        return lax.fori_loop(first, last, body, carry)

    zero = jnp.zeros((HEAD_DIM, HEAD_DIM), F32)
    prepare_loop(0, n_ctx, ctx_out)
    prepare_loop(n_ctx, n_lat, True)
    carry = scan(0, n_ctx, 0, n_ctx, (zero, zero), ctx_out)
    scan(n_ctx, n_lat, 0, n_lat, carry, True)

    def finish(rows, z_ref, y_ref):
        y_ref[...] = ((_rms(o_f[rows, :] + o_b[rows, :]) * norm_ref[...]) * _silu(z_ref[...])).astype(y_ref.dtype)

    finish(slice(tc, None), zl_ref, yl_ref)
    if ctx_out:
        finish(slice(0, tc), zc_ref, yc_ref)


def _gdn(zc, gpc, gptc, zl, gpl, gptl, conv, norm, ctx_out):
    nb, tc, _ = zc.shape
    tl = zl.shape[1]
    assert tc % GD_SUPER == 0 and tl % GD_SUPER == 0, "sequences are processed in whole 256-token superblocks"
    ctx_in = [zc, zc, gpc, gptc] + ([zc, zc] if ctx_out else [])
    ctx_specs = [_seq_spec(tc, C_GD_K), _seq_spec(tc, C_GD_V)] + _gp_specs(tc) + (
        [_seq_spec(tc, C_GD_Q), _seq_spec(tc, C_GD_Z)] if ctx_out else [])
    lat_in = [zl, zl, gpl, gptl, zl, zl]
    lat_specs = [_seq_spec(tl, C_GD_K), _seq_spec(tl, C_GD_V)] + _gp_specs(tl) + [
        _seq_spec(tl, C_GD_Q), _seq_spec(tl, C_GD_Z)]
    conv_spec = lambda off: pl.BlockSpec((3, HEAD_DIM), lambda b, h: (0, off // HEAD_DIM + h))
    out_spec = lambda rows: pl.BlockSpec((None, rows, HEAD_DIM), lambda b, h: (b, 0, h))
    out_shape = lambda rows: jax.ShapeDtypeStruct((nb, rows, BRANCH_W), BF16)
    rows = tc + tl
    seq = lambda dt: pltpu.VMEM((rows, HEAD_DIM), dt)
    two = lambda spec: [spec, spec]
    scratch = ([seq(F32), seq(F32), seq(F32), seq(F32), pltpu.VMEM((rows // ML_CHUNK, N_GATE, ML_CHUNK), F32),
                seq(F32), seq(F32)]
               + two(seq(F32)) + two(seq(BF16)) + two(pltpu.VMEM((rows // GD_SUPER, HEAD_DIM, GD_SUPER), BF16))
               + two(seq(BF16)) + two(pltpu.VMEM((rows, GD_SUPER), BF16)))
    res = pl.pallas_call(
        functools.partial(_gdn_kernel, ctx_out=ctx_out, n_ctx=tc // GD_SUPER, n_lat=tl // GD_SUPER),
        grid=(nb, N_HEADS),
        in_specs=ctx_specs + lat_specs + [conv_spec(0), conv_spec(BRANCH_W), conv_spec(2 * BRANCH_W),
                                          pl.BlockSpec((1, HEAD_DIM), lambda b, h: (0, 0))],
        out_specs=([out_spec(tc)] if ctx_out else []) + [out_spec(tl)],
        out_shape=([out_shape(tc)] if ctx_out else []) + [out_shape(tl)],
        scratch_shapes=scratch,
        compiler_params=_params(("parallel", "parallel")),
        name="gdn",
    )(*ctx_in, *lat_in, conv, conv, conv, norm)
    return (res[0], res[1]) if ctx_out else (None, res[0])


def _rope(x, cos, sin):
    lane = lax.broadcasted_iota(jnp.int32, x.shape, 1)
    quarter = HEAD_DIM // 4
    partner = jnp.where((lane & (2 * quarter - 1)) < quarter,
                        pltpu.roll(x, HEAD_DIM - quarter, 1), pltpu.roll(x, quarter, 1))
    return x * cos + partner * sin


def _attn_kernel(*refs, latent, tc):
    if latent:
        (q_ref, kc_ref, vc_ref, kl_ref, vl_ref, cos_ref, sin_ref, cosq_ref, sinq_ref, qn_ref, kn_ref,
         o_ref, k_scr, v_scr) = refs
    else:
        q_ref, kc_ref, vc_ref, qn_ref, kn_ref, o_ref, k_scr, v_scr = refs

    @pl.when(pl.program_id(2) == 0)
    def _():
        k_scr[0:tc, :] = (_rms(kc_ref[...]) * kn_ref[...]).astype(BF16)
        v_scr[0:tc, :] = vc_ref[...].astype(BF16)
        if latent:
            kl = _rope(_rms(kl_ref[...]) * kn_ref[...], cos_ref[...], sin_ref[...])
            k_scr[tc:, :] = kl.astype(BF16)
            v_scr[tc:, :] = vl_ref[...].astype(BF16)

    scale = HEAD_DIM ** -0.5
    for g in range(AT_Q_HEADS // AT_KV_HEADS):
        cols = slice(g * HEAD_DIM, (g + 1) * HEAD_DIM)
        q = _rms(q_ref[:, cols]) * qn_ref[...]
        if latent:
            q = _rope(q, cosq_ref[...], sinq_ref[...])
        s = lax.dot_general((q * scale).astype(BF16), k_scr[...], (((1,), (1,)), ((), ())),
                            preferred_element_type=F32)
        p = jnp.exp(s - jnp.max(s, axis=-1, keepdims=True))
        l = jnp.sum(p, axis=-1, keepdims=True)
        o_ref[:, cols] = (jnp.dot(p.astype(BF16), v_scr[...], preferred_element_type=F32) / l).astype(o_ref.dtype)


def _attention(zq, zc, zl, cos, sin, qn, kn):
    latent = zl is not None
    nb, tq_total, _ = zq.shape
    tc = zc.shape[1]
    tk = tc + (zl.shape[1] if latent else 0)
    gq = (AT_Q_HEADS // AT_KV_HEADS) * HEAD_DIM
    head_spec = lambda rows, col: pl.BlockSpec((None, rows, HEAD_DIM), lambda b, h, i: (b, 0, col // HEAD_DIM + h))
    vec_spec = pl.BlockSpec((1, HEAD_DIM), lambda b, h, i: (0, 0))
    in_specs = [pl.BlockSpec((None, TQ_ATTN, gq), lambda b, h, i: (b, i, C_AT_Q // gq + h)),
                head_spec(tc, C_AT_K), head_spec(tc, C_AT_V)]
    args = [zq, zc, zc]
    if latent:
        tl = zl.shape[1]
        in_specs += [head_spec(tl, C_AT_K), head_spec(tl, C_AT_V),
                     pl.BlockSpec((tl, HEAD_DIM), lambda b, h, i: (0, 0)),
                     pl.BlockSpec((tl, HEAD_DIM), lambda b, h, i: (0, 0)),
                     pl.BlockSpec((TQ_ATTN, HEAD_DIM), lambda b, h, i: (i, 0)),
                     pl.BlockSpec((TQ_ATTN, HEAD_DIM), lambda b, h, i: (i, 0))]
        args += [zl, zl, cos, sin, cos, sin]
    return pl.pallas_call(
        functools.partial(_attn_kernel, latent=latent, tc=tc),
        grid=(nb, AT_KV_HEADS, tq_total // TQ_ATTN),
        in_specs=in_specs + [vec_spec, vec_spec],
        out_specs=pl.BlockSpec((None, TQ_ATTN, gq), lambda b, h, i: (b, i, h)),
        out_shape=jax.ShapeDtypeStruct((nb, tq_total, BRANCH_W), BF16),
        scratch_shapes=[pltpu.VMEM((tk, HEAD_DIM), BF16), pltpu.VMEM((tk, HEAD_DIM), BF16)],
        compiler_params=_params(("parallel", "parallel", "arbitrary")),
        name="attention",
    )(*args, qn, kn)


def _merge_kernel(h_ref, xcol_ref, modcol_ref, y0_ref, y1_ref, y2_ref, y3_ref,
                  wg0_ref, wg1_ref, wg2_ref, wg3_ref, wb_ref, wo_ref, o_ref, m_scr):
    s = pl.program_id(2)
    n_gate = D_MODEL // TN_MERGE
    ys = (y0_ref, y1_ref, y2_ref, y3_ref)
    wgs = (wg0_ref, wg1_ref, wg2_ref, wg3_ref)

    @pl.when(s < n_gate)
    def _():
        h = h_ref[...]
        acc = jnp.zeros((TM_MERGE, TN_MERGE), F32)
        for n in range(N_BRANCH):
            gate = jnp.dot(h, wgs[n][...], preferred_element_type=F32)
            acc = acc + _sigmoid(gate) * jnp.dot(ys[n][...], wb_ref[n], preferred_element_type=F32)
        m_scr[s] = acc.astype(BF16)

    @pl.when(s >= n_gate)
    def _():
        acc = jnp.zeros((TM_MERGE, TN_OUT), F32)
        for kk in range(n_gate):
            acc = acc + jnp.dot(m_scr[kk], wo_ref[kk * TN_MERGE:(kk + 1) * TN_MERGE, :],
                                preferred_element_type=F32)
        o_ref[...] = xcol_ref[...] + modcol_ref[5:6, :] * acc


def _merge(x, h, mod, ys, w_gate, w_branch, w_out, layer, mod_row):
    groups, rows, _ = x.shape
    n_gate = D_MODEL // TN_MERGE
    n_out = D_MODEL // TN_OUT
    mrow = (lambda b: b) if mod_row is None else (lambda b: mod_row)
    gate_col = lambda s: jnp.minimum(s, n_gate - 1)
    out_col = lambda s: jnp.maximum(s - n_gate, 0)
    y_spec = pl.BlockSpec((None, TM_MERGE, BRANCH_W), lambda b, i, s: (b, i, 0))
    wg_spec = lambda n: pl.BlockSpec((None, D_MODEL, TN_MERGE),
                                     lambda b, i, s: (layer, 0, n * n_gate + gate_col(s)))
    return pl.pallas_call(
        _merge_kernel,
        grid=(groups, rows // TM_MERGE, n_gate + n_out),
        in_specs=[pl.BlockSpec((None, TM_MERGE, D_MODEL), lambda b, i, s: (b, i, 0)),
                  pl.BlockSpec((None, TM_MERGE, TN_OUT), lambda b, i, s: (b, i, out_col(s))),
                  pl.BlockSpec((None, N_MOD, TN_OUT), lambda b, i, s: (mrow(b), 0, out_col(s))),
                  y_spec, y_spec, y_spec, y_spec,
                  wg_spec(0), wg_spec(1), wg_spec(2), wg_spec(3),
                  pl.BlockSpec((None, N_BRANCH, BRANCH_W, TN_MERGE), lambda b, i, s: (layer, 0, 0, gate_col(s))),
                  pl.BlockSpec((None, D_MODEL, TN_OUT), lambda b, i, s: (layer, 0, out_col(s)))],
        out_specs=pl.BlockSpec((None, TM_MERGE, TN_OUT), lambda b, i, s: (b, i, out_col(s))),
        out_shape=jax.ShapeDtypeStruct(x.shape, F32),
        scratch_shapes=[pltpu.VMEM((n_gate, TM_MERGE, TN_MERGE), BF16)],
        compiler_params=_params(("parallel", "parallel", "arbitrary")),
        name="merge",
    )(h, x, mod, *ys, w_gate, w_gate, w_gate, w_gate, w_branch, w_out)


def _rope_tables(seq):
    n = HEAD_DIM // 4
    inv = ROPE_THETA ** (-jnp.arange(n, dtype=F32) / n)
    pos = jnp.arange(seq)
    ar = (pos // GRID_W).astype(F32)[:, None] * inv
    ac = (pos % GRID_W).astype(F32)[:, None] * inv
    cos = jnp.concatenate([jnp.cos(ar), jnp.cos(ar), jnp.cos(ac), jnp.cos(ac)], axis=1)
    sin = jnp.concatenate([-jnp.sin(ar), jnp.sin(ar), -jnp.sin(ac), jnp.sin(ac)], axis=1)
    return cos, sin


def _gate_params(ml_if_bias, gd_a_log, gd_dt_bias):
    zeros = jnp.zeros((2, N_HEADS), F32)
    bias = jnp.concatenate([ml_if_bias.reshape(4, N_HEADS), zeros, gd_dt_bias], axis=0).T
    alog = jnp.concatenate([jnp.zeros((6, N_HEADS), F32), gd_a_log], axis=0).T
    pad = lambda a: jnp.pad(a.reshape(1, N_HEADS * N_GATE), ((0, 0), (0, LANES - N_HEADS * N_GATE)))
    col = lambda a: a.reshape(N_HEADS * N_GATE, 1)
    return pad(bias), pad(alog), col(bias), col(alog)


def _gate_weights(w_in):
    g = jnp.concatenate([w_in[:, R_ML_IF:R_ML_IF + 16], w_in[:, R_GD_BA:R_GD_BA + 16]], axis=1)
    g = g.reshape(D_MODEL, N_GATE, N_HEADS).transpose(0, 2, 1).reshape(D_MODEL, N_HEADS * N_GATE)
    w_gc = jnp.pad(g, ((0, 0), (0, LANES - N_HEADS * N_GATE)))
    w_gr = g.T
    return w_gc.astype(BF16), w_gr.astype(BF16)


def kernel(x, c, ctx, c_ctx, mod_w, mod_b, ffn1_norm, ffn1_w_in, ffn1_w_out, mix_norm, w_in, sg_norm, sg_w, sg_b, ml_if_bias, ml_norm, gd_conv, gd_a_log, gd_dt_bias, gd_norm, at_q_norm, at_k_norm, w_branch, w_out, ffn2_norm, ffn2_w_in, ffn2_w_out):
    nb, seq, _ = x.shape
    tc = ctx.shape[1]
    depth = mod_w.shape[0]
    ctx_row = nb
    cos, sin = _rope_tables(seq)

    cc = jnp.zeros((MOD_ROWS, D_MODEL), F32).at[:nb].set(c).at[ctx_row].set(c_ctx)
    mods = _modulation(cc, mod_w, mod_b).reshape(depth, MOD_ROWS, N_MOD, D_MODEL)

    f1_in, f1_out, f2_in, f2_out = [_to_bf16(w) for w in (ffn1_w_in, ffn1_w_out, ffn2_w_in, ffn2_w_out)]
    w_main = jnp.concatenate([w_in[..., :R_ML_IF], w_in[..., R_GD_KV:R_GD_BA], w_in[..., R_AT_KV:R_GATE]],
                             axis=-1).astype(BF16)
    w_gate = w_in[..., R_GATE:].astype(BF16)
    wb, wo = w_branch.astype(BF16), w_out.astype(BF16)
    flat = lambda a: a.reshape(1, nb * tc, a.shape[-1])
    unflat = lambda a: a.reshape(nb, tc, a.shape[-1])
    xc = ctx
    for l in range(depth):
        last = l == depth - 1
        ctx_out = not last
        mod = mods[l]
        w_gc, w_gr = _gate_weights(w_in[l])
        bias_c, alog_c, bias_r, alog_r = _gate_params(ml_if_bias[l], gd_a_log[l], gd_dt_bias[l])
        sgw = sg_w[l].astype(BF16)
        sg_bias = jnp.repeat(sg_b[l].T, BRANCH_W // SG_GROUPS, axis=1)
        sgn = sg_norm[l].reshape(1, BRANCH_W)
        mln = ml_norm[l].reshape(1, BRANCH_W)
        gdn_g = gd_norm[l].reshape(1, HEAD_DIM)
        qn, kn = at_q_norm[l].reshape(1, HEAD_DIM), at_k_norm[l].reshape(1, HEAD_DIM)

        x = _ffn(x, mod, ffn1_norm[l], f1_in, f1_out, l, 0, None)
        xc = unflat(_ffn(flat(xc), mod, ffn1_norm[l], f1_in, f1_out, l, 0, ctx_row))

        zl, gcl, grl, hl = _inproj(x, mod, mix_norm[l], w_main, l, w_gc, w_gr, N_MAIN, None)
        zc, gcc, grc, hc = _inproj(flat(xc), mod, mix_norm[l], w_main, l, w_gc, w_gr,
                                   N_MAIN if ctx_out else KV_MAIN, ctx_row)
        zc = unflat(zc)
        gcc = unflat(gcc)
        grc = grc.reshape(N_HEADS * N_GATE, nb, tc).transpose(1, 0, 2)
        gpl, gptl = _gateprep(gcl, grl, bias_c, alog_c, bias_r, alog_r)
        gpc, gptc = _gateprep(gcc, grc, bias_c, alog_c, bias_r, alog_r)

        y_sg_l = _spatial_gating(zl, sgn, sgw, sg_bias)
        y_ml_c, y_ml_l = _mlstm(zc, gpc, gptc, zl, gpl, gptl, mln, ctx_out)
        y_gd_c, y_gd_l = _gdn(zc, gpc, gptc, zl, gpl, gptl, gd_conv[l], gdn_g, ctx_out)
        y_at_l = _attention(zl, zc, zl, cos, sin, qn, kn)
        x = _merge(x, hl, mod, [y_sg_l, y_ml_l, y_gd_l, y_at_l], w_gate, wb, wo, l, None)
        if ctx_out:
            y_sg_c = _spatial_gating(zc, sgn, sgw, sg_bias)
            y_at_c = _attention(zc, zc, None, None, None, qn, kn)
            xc = unflat(_merge(flat(xc), hc, mod, [flat(y) for y in (y_sg_c, y_ml_c, y_gd_c, y_at_c)],
                               w_gate, wb, wo, l, ctx_row))

        x = _ffn(x, mod, ffn2_norm[l], f2_in, f2_out, l, 6, None)
        if ctx_out:
            xc = unflat(_ffn(flat(xc), mod, ffn2_norm[l], f2_in, f2_out, l, 6, ctx_row))
    return x
```

```python
import functools
import math

import jax
import jax.numpy as jnp
from jax import lax
from jax.experimental import pallas as pl
from jax.experimental.pallas import tpu as pltpu

F32 = jnp.float32
BF16 = jnp.bfloat16
EPS = 1e-6

D_MODEL = 2048
D_FF = 5632
N_MOD = 9
N_BRANCH = 4
BRANCH_W = 512
HEAD_DIM = 128
N_HEADS = 4
SG_GROUPS = 4
ML_CHUNK = 128
GD_CHUNK = 64
GD_SUPER = 256
GD_PREP_GROUP = 2
AT_Q_HEADS = 4
AT_KV_HEADS = 2
GRID_W = 64
ROPE_THETA = 10000.0
FFN_RESIDUAL = 0.5

LANES = 128
N_GATE = 8
MOD_ROWS = 16

C_ML_K, C_ML_V = 0, 512
C_GD_K, C_GD_V = 1024, 1536
C_AT_K, C_AT_V = 2048, 2304
KV_MAIN = 2560
C_SG_U, C_SG_V = 2560, 3072
C_ML_Q, C_ML_O = 3584, 4096
C_GD_Q, C_GD_Z = 4608, 5120
C_AT_Q = 5632
N_MAIN = 6144
R_ML_IF = 1024
R_GD_KV = 1040
R_GD_BA = 2064
R_AT_KV = 2080
R_REST = 2592
R_GATE = 6176

VMEM_LIMIT = 56 * 1024 * 1024

TM_FFN = 512
TF_FFN = 512
TM_PROJ = 1024
TN_PROJ = 512
TM_MERGE = 1024
TN_MERGE = 256
TN_OUT = 512
TQ_ATTN = 256
TR_SG = 256
TG_GATE = 256
CAST_BLOCK_BYTES = 8 * 1024 * 1024
SPLIT_ROWS = 128
TK_MOD = 128


def _params(sem):
    return pltpu.CompilerParams(dimension_semantics=sem, vmem_limit_bytes=VMEM_LIMIT)


def _mm(a, b):
    return jnp.dot(a.astype(BF16), b.astype(BF16), preferred_element_type=F32)


def _mm_nt(a, b):
    return lax.dot_general(a.astype(BF16), b.astype(BF16), (((1,), (1,)), ((), ())),
                           preferred_element_type=F32)


def _mm_tn(a, b):
    return lax.dot_general(a.astype(BF16), b.astype(BF16), (((0,), (0,)), ((), ())),
                           preferred_element_type=F32)


def _sigmoid(x):
    return 1.0 / (1.0 + jnp.exp(-x))


def _silu(x):
    return x * _sigmoid(x)


def _softplus(x):
    return jnp.maximum(x, 0.0) + jnp.log1p(jnp.exp(-jnp.abs(x)))


def _gelu_tanh(x):
    c = math.sqrt(2.0 / math.pi)
    return x * (0.5 * (1.0 + jnp.tanh(c * (x + 0.044715 * (x * x * x)))))


def _rms(x):
    return x * lax.rsqrt(jnp.mean(x * x, axis=-1, keepdims=True) + EPS)


def _norm_mod(x, gain, shift, scale):
    return (_rms(x) * gain) * (1.0 + scale) + shift


def _cast_kernel(x_ref, o_ref):
    o_ref[...] = x_ref[...].astype(o_ref.dtype)


def _to_bf16(w):
    depth, r, c = w.shape
    rows = 1 << (max(16, min(r, CAST_BLOCK_BYTES // (4 * c))).bit_length() - 1)
    while r % rows:
        rows //= 2
    return pl.pallas_call(
        _cast_kernel,
        grid=(depth, r // rows),
        in_specs=[pl.BlockSpec((None, rows, c), lambda l, i: (l, i, 0))],
        out_specs=pl.BlockSpec((None, rows, c), lambda l, i: (l, i, 0)),
        out_shape=jax.ShapeDtypeStruct(w.shape, BF16),
        compiler_params=_params(("parallel", "parallel")),
        name="cast_bf16",
    )(w)


def _split_w_in_kernel(w_ref, main_ref, gate_ref):
    w = w_ref[...]
    main_ref[...] = jnp.concatenate([w[:, :R_ML_IF], w[:, R_GD_KV:R_GD_BA], w[:, R_AT_KV:R_GATE]],
                                    axis=1).astype(BF16)
    gate_ref[...] = w[:, R_GATE:].astype(BF16)


def _split_w_in(w_in):
    depth, d, n = w_in.shape
    n_gate = n - R_GATE
    return pl.pallas_call(
        _split_w_in_kernel,
        grid=(depth, d // SPLIT_ROWS),
        in_specs=[pl.BlockSpec((None, SPLIT_ROWS, n), lambda l, i: (l, i, 0))],
        out_specs=[pl.BlockSpec((None, SPLIT_ROWS, N_MAIN), lambda l, i: (l, i, 0)),
                   pl.BlockSpec((None, SPLIT_ROWS, n_gate), lambda l, i: (l, i, 0))],
        out_shape=[jax.ShapeDtypeStruct((depth, d, N_MAIN), BF16),
                   jax.ShapeDtypeStruct((depth, d, n_gate), BF16)],
        compiler_params=_params(("parallel", "parallel")),
        name="split_w_in",
    )(w_in)


def _mod_kernel(cc_ref, w_ref, b_ref, o_ref):
    @pl.when(pl.program_id(1) == 0)
    def _():
        o_ref[...] = jnp.broadcast_to(b_ref[...], o_ref.shape)

    o_ref[...] += _mm(_silu(cc_ref[...]), w_ref[...])


def _modulation(cc, mod_w, mod_b):
    depth, d, n = mod_w.shape
    ccs = cc.reshape(MOD_ROWS, d // TK_MOD, TK_MOD).transpose(1, 0, 2)
    return pl.pallas_call(
        _mod_kernel,
        grid=(depth, d // TK_MOD),
        in_specs=[pl.BlockSpec((None, MOD_ROWS, TK_MOD), lambda l, k: (k, 0, 0)),
                  pl.BlockSpec((None, TK_MOD, n), lambda l, k: (l, k, 0)),
                  pl.BlockSpec((None, 1, n), lambda l, k: (l, 0, 0))],
        out_specs=pl.BlockSpec((None, MOD_ROWS, n), lambda l, k: (l, 0, 0)),
        out_shape=jax.ShapeDtypeStruct((depth, MOD_ROWS, n), F32),
        compiler_params=_params(("parallel", "arbitrary")),
        name="modulation",
    )(ccs, mod_w, mod_b.reshape(depth, 1, n))


def _ffn_kernel(x_ref, mod_ref, g_ref, wa_ref, wb_ref, wo_ref, o_ref, h_scr, acc_scr, *, base):
    j = pl.program_id(2)

    @pl.when(j == 0)
    def _():
        h = _norm_mod(x_ref[...], g_ref[...], mod_ref[base:base + 1, :], mod_ref[base + 1:base + 2, :])
        h_scr[...] = h.astype(BF16)
        acc_scr[...] = jnp.zeros_like(acc_scr)

    h = h_scr[...]
    a = jnp.dot(h, wa_ref[...], preferred_element_type=F32)
    b = jnp.dot(h, wb_ref[...], preferred_element_type=F32)
    acc_scr[...] += _mm(_silu(a) * b, wo_ref[...])

    @pl.when(j == pl.num_programs(2) - 1)
    def _():
        o_ref[...] = x_ref[...] + (FFN_RESIDUAL * mod_ref[base + 2:base + 3, :]) * acc_scr[...]


def _ffn(x, mod, gain, w_in, w_out, layer, base, mod_row):
    groups, rows, _ = x.shape
    nj = D_FF // TF_FFN
    mod_idx = (lambda b, i, j: (b, 0, 0)) if mod_row is None else (lambda b, i, j: (mod_row, 0, 0))
    return pl.pallas_call(
        functools.partial(_ffn_kernel, base=base),
        grid=(groups, rows // TM_FFN, nj),
        in_specs=[pl.BlockSpec((None, TM_FFN, D_MODEL), lambda b, i, j: (b, i, 0)),
                  pl.BlockSpec((None, N_MOD, D_MODEL), mod_idx),
                  pl.BlockSpec((1, D_MODEL), lambda b, i, j: (0, 0)),
                  pl.BlockSpec((None, D_MODEL, TF_FFN), lambda b, i, j: (layer, 0, j)),
                  pl.BlockSpec((None, D_MODEL, TF_FFN), lambda b, i, j: (layer, 0, j + nj)),
                  pl.BlockSpec((None, TF_FFN, D_MODEL), lambda b, i, j: (layer, j, 0))],
        out_specs=pl.BlockSpec((None, TM_FFN, D_MODEL), lambda b, i, j: (b, i, 0)),
        out_shape=jax.ShapeDtypeStruct(x.shape, F32),
        scratch_shapes=[pltpu.VMEM((TM_FFN, D_MODEL), BF16), pltpu.VMEM((TM_FFN, D_MODEL), F32)],
        compiler_params=_params(("parallel", "parallel", "arbitrary")),
        name="ffn",
    )(x, mod, gain.reshape(1, D_MODEL), w_in, w_in, w_out)


def _inproj_kernel(x_ref, mod_ref, g_ref, w_ref, wgc_ref, wgr_ref, z_ref, gc_ref, gr_ref, h_ref):
    n = pl.program_id(2)

    @pl.when(n == 0)
    def _():
        h = _norm_mod(x_ref[...], g_ref[...], mod_ref[3:4, :], mod_ref[4:5, :]).astype(BF16)
        h_ref[...] = h
        gc_ref[...] = jnp.dot(h, wgc_ref[...], preferred_element_type=F32)
        gr_ref[...] = lax.dot_general(wgr_ref[...], h, (((1,), (1,)), ((), ())), preferred_element_type=F32)

    z_ref[...] = jnp.dot(h_ref[...], w_ref[...], preferred_element_type=F32)


def _inproj(x, mod, gain, w_main, layer, w_gc, w_gr, n_cols, mod_row):
    groups, rows, _ = x.shape
    mod_idx = (lambda b, i, n: (b, 0, 0)) if mod_row is None else (lambda b, i, n: (mod_row, 0, 0))
    return pl.pallas_call(
        _inproj_kernel,
        grid=(groups, rows // TM_PROJ, n_cols // TN_PROJ),
        in_specs=[pl.BlockSpec((None, TM_PROJ, D_MODEL), lambda b, i, n: (b, i, 0)),
                  pl.BlockSpec((None, N_MOD, D_MODEL), mod_idx),
                  pl.BlockSpec((1, D_MODEL), lambda b, i, n: (0, 0)),
                  pl.BlockSpec((None, D_MODEL, TN_PROJ), lambda b, i, n: (layer, 0, n)),
                  pl.BlockSpec((D_MODEL, LANES), lambda b, i, n: (0, 0)),
                  pl.BlockSpec((N_HEADS * N_GATE, D_MODEL), lambda b, i, n: (0, 0))],
        out_specs=[pl.BlockSpec((None, TM_PROJ, TN_PROJ), lambda b, i, n: (b, i, n)),
                   pl.BlockSpec((None, TM_PROJ, LANES), lambda b, i, n: (b, i, 0)),
                   pl.BlockSpec((None, N_HEADS * N_GATE, TM_PROJ), lambda b, i, n: (b, 0, i)),
                   pl.BlockSpec((None, TM_PROJ, D_MODEL), lambda b, i, n: (b, i, 0))],
        out_shape=[jax.ShapeDtypeStruct((groups, rows, n_cols), F32),
                   jax.ShapeDtypeStruct((groups, rows, LANES), F32),
                   jax.ShapeDtypeStruct((groups, N_HEADS * N_GATE, rows), F32),
                   jax.ShapeDtypeStruct((groups, rows, D_MODEL), BF16)],
        compiler_params=_params(("parallel", "parallel", "arbitrary")),
        name="inproj",
    )(x, mod, gain.reshape(1, D_MODEL), w_main, w_gc, w_gr)


def _split3(e):
    e1 = e.astype(BF16)
    r1 = e - e1.astype(F32)
    e2 = r1.astype(BF16)
    e3 = (r1 - e2.astype(F32)).astype(BF16)
    return e1, e2, e3


def _gate_values(pre, a_log, cls):
    neg_sp = -_softplus(-pre)
    return jnp.where(cls < 2, pre,
                     jnp.where(cls < 4, neg_sp,
                               jnp.where(cls < 6, _sigmoid(pre), -jnp.exp(a_log) * _softplus(pre))))


def _gateprep_kernel(gc_ref, gr_ref, bc_ref, ac_ref, br_ref, ar_ref, oc_ref, or_ref):
    for i in range(TG_GATE // ML_CHUNK):
        tile = slice(i * ML_CHUNK, (i + 1) * ML_CHUNK)
        out_c, or_ref[i] = _gateprep_tile(gc_ref[tile, :], gr_ref[:, tile], bc_ref[...], ac_ref[...],
                                          br_ref[...], ar_ref[...])
        for h in range(N_HEADS):
            oc_ref[tile, h * LANES:(h + 1) * LANES] = (
                out_c if h == 0 else pltpu.roll(out_c, LANES - h * N_GATE, 1))


def _gateprep_tile(gc, gr, bias_c, alog_c, bias_r, alog_r):
    L = ML_CHUNK
    half = GD_CHUNK
    pre = gc + bias_c
    cls = lax.broadcasted_iota(jnp.int32, pre.shape, 1) & (N_GATE - 1)
    t = lax.broadcasted_iota(jnp.int32, pre.shape, 0)
    e = _gate_values(pre, alog_c, cls)
    tri = (lax.broadcasted_iota(jnp.int32, (L, L), 1) <= lax.broadcasted_iota(jnp.int32, (L, L), 0)).astype(BF16)
    e1, e2, e3 = _split3(e)
    p = (jnp.dot(tri, e1, preferred_element_type=F32) + jnp.dot(tri, e2, preferred_element_type=F32)
         + jnp.dot(tri, e3, preferred_element_type=F32))
    tot = p[L - 1:L, :]
    mid = p[half - 1:half, :]
    second = t >= half
    pre64 = p - jnp.where(second, mid, 0.0)
    suf64 = jnp.where(second, tot, mid) - p + e
    cum = jnp.where(cls == 2, p, jnp.where(cls == 3, tot - p + e, jnp.where(cls == 6, pre64, suf64)))
    out_c = jnp.where((cls == 2) | (cls == 3) | (cls == 6) | (cls == 7), cum, e)

    pre_r = gr + bias_r
    cls_r = lax.broadcasted_iota(jnp.int32, pre_r.shape, 0) & (N_GATE - 1)
    t_r = lax.broadcasted_iota(jnp.int32, pre_r.shape, 1)
    e_r = _gate_values(pre_r, alog_r, cls_r)
    tri_r = (lax.broadcasted_iota(jnp.int32, (L, L), 0) <= lax.broadcasted_iota(jnp.int32, (L, L), 1)).astype(BF16)
    f1, f2, f3 = _split3(e_r)
    pr = (jnp.dot(f1, tri_r, preferred_element_type=F32) + jnp.dot(f2, tri_r, preferred_element_type=F32)
          + jnp.dot(f3, tri_r, preferred_element_type=F32))
    tot_r = pr[:, L - 1:L]
    mid_r = pr[:, half - 1:half]
    second_r = t_r >= half
    pre64_r = pr - jnp.where(second_r, mid_r, 0.0)
    suf64_r = jnp.where(second_r, tot_r, mid_r) - pr + e_r
    cum_r = jnp.where(cls_r == 2, pr, jnp.where(cls_r == 3, tot_r - pr + e_r, jnp.where(cls_r == 6, pre64_r, suf64_r)))
    return out_c, jnp.where((cls_r == 2) | (cls_r == 3) | (cls_r == 6) | (cls_r == 7), cum_r, e_r)


def _gateprep(gc, gr, bias_c, alog_c, bias_r, alog_r):
    groups, rows, _ = gc.shape
    nt = rows // ML_CHUNK
    hw = N_HEADS * LANES
    hg = N_HEADS * N_GATE
    tiles = TG_GATE // ML_CHUNK
    return pl.pallas_call(
        _gateprep_kernel,
        grid=(groups, rows // TG_GATE),
        in_specs=[pl.BlockSpec((None, TG_GATE, LANES), lambda b, c: (b, c, 0)),
                  pl.BlockSpec((None, hg, TG_GATE), lambda b, c: (b, 0, c)),
                  pl.BlockSpec((1, LANES), lambda b, c: (0, 0)),
                  pl.BlockSpec((1, LANES), lambda b, c: (0, 0)),
                  pl.BlockSpec((hg, 1), lambda b, c: (0, 0)),
                  pl.BlockSpec((hg, 1), lambda b, c: (0, 0))],
        out_specs=[pl.BlockSpec((None, TG_GATE, hw), lambda b, c: (b, c, 0)),
                   pl.BlockSpec((None, tiles, hg, ML_CHUNK), lambda b, c: (b, c, 0, 0))],
        out_shape=[jax.ShapeDtypeStruct((groups, rows, hw), F32),
                   jax.ShapeDtypeStruct((groups, nt, hg, ML_CHUNK), F32)],
        compiler_params=_params(("parallel", "parallel")),
        name="gateprep",
    )(gc, gr, bias_c, alog_c, bias_r, alog_r)


def _sg_kernel(u_ref, v_ref, gn_ref, w_ref, bias_ref, o_ref):
    gw = BRANCH_W // SG_GROUPS
    for ch in range(TR_SG // ML_CHUNK):
        rows = slice(ch * ML_CHUNK, (ch + 1) * ML_CHUNK)
        v = _rms(_gelu_tanh(v_ref[rows, :])) * gn_ref[...]
        u = _gelu_tanh(u_ref[rows, :])
        for g in range(SG_GROUPS):
            cols = slice(g * gw, (g + 1) * gw)
            s = _mm(w_ref[g], v[:, cols]) + bias_ref[:, cols]
            o_ref[rows, cols] = (u[:, cols] * s).astype(o_ref.dtype)


def _spatial_gating(z, gn, w_s, bias_full):
    groups, rows, _ = z.shape
    ub, vb = C_SG_U // BRANCH_W, C_SG_V // BRANCH_W
    return pl.pallas_call(
        _sg_kernel,
        grid=(groups, rows // TR_SG),
        in_specs=[pl.BlockSpec((None, TR_SG, BRANCH_W), lambda b, i: (b, i, ub)),
                  pl.BlockSpec((None, TR_SG, BRANCH_W), lambda b, i: (b, i, vb)),
                  pl.BlockSpec((1, BRANCH_W), lambda b, i: (0, 0)),
                  pl.BlockSpec((SG_GROUPS, ML_CHUNK, ML_CHUNK), lambda b, i: (0, 0, 0)),
                  pl.BlockSpec((ML_CHUNK, BRANCH_W), lambda b, i: (0, 0))],
        out_specs=pl.BlockSpec((None, TR_SG, BRANCH_W), lambda b, i: (b, i, 0)),
        out_shape=jax.ShapeDtypeStruct((groups, rows, BRANCH_W), BF16),
        compiler_params=_params(("parallel", "parallel")),
        name="spatial_gating",
    )(z, z, gn, w_s, bias_full)


def _mlstm_chunk(q, k, v, gp, gpt, state, rev, with_out):
    L = ML_CHUNK
    j_i, j_b = (1, 3) if rev else (0, 2)
    li_c = gp[:, j_i:j_i + 1]
    b_c = gp[:, j_b:j_b + 1]
    b_end = b_c[0:1, :] if rev else b_c[L - 1:L, :]
    ks = k * (HEAD_DIM ** -0.5)
    a = b_end - b_c + li_c
    m_loc = jnp.max(a, axis=0, keepdims=True)
    wgt = jnp.broadcast_to(jnp.exp(a - m_loc), (L, HEAD_DIM))
    ct_loc = _mm_tn(ks, jnp.concatenate([v * wgt, wgt], axis=1))
    ct, m = state
    m_new = jnp.maximum(b_end + m, m_loc)
    sp = jnp.exp(b_end + m - m_new)
    sl = jnp.exp(m_loc - m_new)
    new_state = (sp * ct + sl * ct_loc, m_new)
    if not with_out:
        return new_state, None
    li_r = gpt[j_i:j_i + 1, :]
    b_r = gpt[j_b:j_b + 1, :]
    t = lax.broadcasted_iota(jnp.int32, (L, L), 0)
    s = lax.broadcasted_iota(jnp.int32, (L, L), 1)
    incl = (s >= t) if rev else (s <= t)
    e = jnp.where(incl, li_r - b_r, -jnp.inf)
    c = jnp.broadcast_to(jnp.maximum(jnp.max(e, axis=1, keepdims=True), m), (L, L))
    p = jnp.exp(e - c) * _mm_nt(q, ks)
    inter = jnp.exp(m - c)
    pv = _mm(p, jnp.concatenate([v, jnp.ones_like(v)], axis=1))
    qc = _mm(q, ct)
    num = pv[:, :HEAD_DIM] + inter * qc[:, :HEAD_DIM]
    den = pv[:, HEAD_DIM:] + inter * qc[:, HEAD_DIM:]
    m_t = jnp.broadcast_to(b_c, (L, L)) + c
    return new_state, num / jnp.maximum(jnp.abs(den), jnp.exp(-m_t))


def _mlstm_kernel(*refs, ctx_out, n_ctx, n_lat):
    if ctx_out:
        (kc_ref, vc_ref, gpc_ref, gptc_ref, qc_ref, oc_ref, kl_ref, vl_ref, gpl_ref, gptl_ref, ql_ref, ol_ref,
         norm_ref, yc_ref, yl_ref, hfc, hbc, hfl, hbl) = refs
    else:
        (kc_ref, vc_ref, gpc_ref, gptc_ref, kl_ref, vl_ref, gpl_ref, gptl_ref, ql_ref, ol_ref,
         norm_ref, yl_ref, hfl, hbl) = refs
        qc_ref = oc_ref = yc_ref = hfc = hbc = None
    L = ML_CHUNK

    def run(k_ref, v_ref, gp_ref, gpt_ref, q_ref, hf, hb, n_chunks, st_f, st_b, with_out):
        def body(j, carry):
            st_f, st_b = carry
            rf = pl.ds(pl.multiple_of(j * L, L), L)
            jb = n_chunks - 1 - j
            rb = pl.ds(pl.multiple_of(jb * L, L), L)
            st_f, h_f = _mlstm_chunk(q_ref[rf, :] if with_out else None, k_ref[rf, :], v_ref[rf, :],
                                     gp_ref[rf, :], gpt_ref[j], st_f, False, with_out)
            st_b, h_b = _mlstm_chunk(q_ref[rb, :] if with_out else None, k_ref[rb, :], v_ref[rb, :],
                                     gp_ref[rb, :], gpt_ref[jb], st_b, True, with_out)
            if with_out:
                hf[rf, :] = h_f
                hb[rb, :] = h_b
            return st_f, st_b
        return lax.fori_loop(0, n_chunks, body, (st_f, st_b))

    zero = (jnp.zeros((HEAD_DIM, 2 * HEAD_DIM), F32), jnp.zeros((1, 1), F32))
    st_f, st_b = run(kc_ref, vc_ref, gpc_ref, gptc_ref, qc_ref, hfc, hbc, n_ctx, zero, zero, ctx_out)
    run(kl_ref, vl_ref, gpl_ref, gptl_ref, ql_ref, hfl, hbl, n_lat, st_f, st_b, True)

    def finish(hf, hb, o_ref, y_ref):
        y_ref[...] = (_sigmoid(o_ref[...]) * (_rms(hf[...] + hb[...]) * norm_ref[...])).astype(y_ref.dtype)

    finish(hfl, hbl, ol_ref, yl_ref)
    if ctx_out:
        finish(hfc, hbc, oc_ref, yc_ref)


def _seq_spec(rows, col):
    return pl.BlockSpec((None, rows, HEAD_DIM), lambda b, h: (b, 0, col // HEAD_DIM + h))


def _gp_specs(rows):
    return [pl.BlockSpec((None, rows, LANES), lambda b, h: (b, 0, h)),
            pl.BlockSpec((None, rows // ML_CHUNK, N_GATE, ML_CHUNK), lambda b, h: (b, 0, h, 0))]


def _mlstm(zc, gpc, gptc, zl, gpl, gptl, norm, ctx_out):
    nb, tc, _ = zc.shape
    tl = zl.shape[1]
    ctx_in = [zc, zc, gpc, gptc] + ([zc, zc] if ctx_out else [])
    ctx_specs = [_seq_spec(tc, C_ML_K), _seq_spec(tc, C_ML_V)] + _gp_specs(tc) + (
        [_seq_spec(tc, C_ML_Q), _seq_spec(tc, C_ML_O)] if ctx_out else [])
    lat_in = [zl, zl, gpl, gptl, zl, zl]
    lat_specs = [_seq_spec(tl, C_ML_K), _seq_spec(tl, C_ML_V)] + _gp_specs(tl) + [
        _seq_spec(tl, C_ML_Q), _seq_spec(tl, C_ML_O)]
    out_spec = lambda rows: pl.BlockSpec((None, rows, HEAD_DIM), lambda b, h: (b, 0, h))
    out_shape = lambda rows: jax.ShapeDtypeStruct((nb, rows, BRANCH_W), BF16)
    scr = lambda rows: [pltpu.VMEM((rows, HEAD_DIM), F32), pltpu.VMEM((rows, HEAD_DIM), F32)]
    res = pl.pallas_call(
        functools.partial(_mlstm_kernel, ctx_out=ctx_out, n_ctx=tc // ML_CHUNK, n_lat=tl // ML_CHUNK),
        grid=(nb, N_HEADS),
        in_specs=ctx_specs + lat_specs + [pl.BlockSpec((1, HEAD_DIM), lambda b, h: (0, h))],
        out_specs=([out_spec(tc)] if ctx_out else []) + [out_spec(tl)],
        out_shape=([out_shape(tc)] if ctx_out else []) + [out_shape(tl)],
        scratch_shapes=(scr(tc) if ctx_out else []) + scr(tl),
        compiler_params=_params(("parallel", "parallel")),
        name="mlstm",
    )(*ctx_in, *lat_in, norm)
    return (res[0], res[1]) if ctx_out else (None, res[0])


def _conv_silu(x, w3):
    rows = x.shape[0]
    t = lax.broadcasted_iota(jnp.int32, x.shape, 0)
    prev = jnp.where(t == 0, 0.0, pltpu.roll(x, 1, 0))
    nxt = jnp.where(t == rows - 1, 0.0, pltpu.roll(x, rows - 1, 0))
    return _silu(w3[0:1, :] * prev + w3[1:2, :] * x + w3[2:3, :] * nxt)


def _l2norm(x):
    return x * lax.rsqrt(jnp.sum(x * x, axis=-1, keepdims=True) + EPS)


def _chunk_totals(gc_r, rev):
    L = GD_CHUNK
    pos = lax.broadcasted_iota(jnp.int32, gc_r.shape, 1)
    tot = jnp.zeros_like(gc_r)
    for c in range(GD_SUPER // L):
        i = c * L if rev else c * L + L - 1
        tot = jnp.where((pos >= c * L) & (pos < (c + 1) * L), gc_r[:, i:i + 1], tot)
    return tot


def _gdn_prepare(blocks):
    t = lax.broadcasted_iota(jnp.int32, (GD_SUPER, GD_SUPER), 0)
    s = lax.broadcasted_iota(jnp.int32, (GD_SUPER, GD_SUPER), 1)
    x = t ^ s
    same = x < GD_CHUNK
    kk0 = [_mm_nt(k, k) for k, _, _, _, _ in blocks]
    qk0 = [None if qs is None else _mm_nt(qs, k) for k, _, qs, _, _ in blocks]
    kT = [k.T for k, _, _, _, _ in blocks]
    prob = []
    for i, (k, v, qs, gp, gpt) in enumerate(blocks):
        for rev in (False, True):
            j_b, j_g = (5, 7) if rev else (4, 6)
            beta = gp[:, j_b:j_b + 1]
            gc_c = gp[:, j_g:j_g + 1]
            gc_r = gpt[j_g:j_g + 1, :]
            incl = same & ((s >= t) if rev else (s <= t))
            strict = same & ((s > t) if rev else (s < t))
            decay = jnp.exp(jnp.where(incl, gc_c - gc_r, -jnp.inf))
            m = jnp.where(strict, (beta * kk0[i]) * decay, 0.0)
            prob.append(dict(i=i, rev=rev, beta=beta, gc_r=gc_r, decay=decay, m=m, eg=jnp.exp(gc_c)))
    tinv = [jnp.where(x == 0, 1.0, 0.0) - jnp.where(x == 1, p["m"], 0.0) for p in prob]
    for lvl in range(1, 6):
        tc = [_mm(tinv[n], jnp.where((x >> lvl) == 1, p["m"], 0.0)) for n, p in enumerate(prob)]
        tinv = [tinv[n] - _mm(tc[n], tinv[n]) for n in range(len(prob))]
    out = [[] for _ in blocks]
    for n, p in enumerate(prob):
        k, v, qs, _, _ = blocks[p["i"]]
        y = _mm(tinv[n], jnp.concatenate([v * p["beta"], (k * p["beta"]) * p["eg"]], axis=1))
        kendT = (kT[p["i"]] * jnp.exp(_chunk_totals(p["gc_r"], p["rev"]) - p["gc_r"])).astype(BF16)
        qd = None if qs is None else (qs * p["eg"]).astype(BF16)
        qk = None if qs is None else (qk0[p["i"]] * p["decay"]).astype(BF16)
        out[p["i"]].append((y[:, :HEAD_DIM], y[:, HEAD_DIM:].astype(BF16), kendT, qd, qk))
    return out


def _gdn_kernel(*refs, ctx_out, n_ctx, n_lat):
    n_in = 16 if ctx_out else 14
    n_out = 2 if ctx_out else 1
    if ctx_out:
        (kc_ref, vc_ref, gpc_ref, gptc_ref, qc_ref, zc_ref, kl_ref, vl_ref, gpl_ref, gptl_ref, ql_ref, zl_ref,
         wq_ref, wk_ref, wv_ref, norm_ref, yc_ref, yl_ref) = refs[:n_in + n_out]
    else:
        (kc_ref, vc_ref, gpc_ref, gptc_ref, kl_ref, vl_ref, gpl_ref, gptl_ref, ql_ref, zl_ref,
         wq_ref, wk_ref, wv_ref, norm_ref, yl_ref) = refs[:n_in + n_out]
        qc_ref = zc_ref = yc_ref = None
    ks, vs, qs, gpa, gpta, o_f, o_b = refs[n_in + n_out:n_in + n_out + 7]
    per_dir = refs[n_in + n_out + 7:]
    u_s, w_s, kT_s, qd_s, qk_s = (per_dir[0:2], per_dir[2:4], per_dir[4:6], per_dir[6:8], per_dir[8:10])
    L = GD_CHUNK
    SB = GD_SUPER
    tc = n_ctx * SB
    tiles = SB // ML_CHUNK

    ks[0:tc, :] = _l2norm(_conv_silu(kc_ref[...], wk_ref[...]))
    vs[0:tc, :] = _conv_silu(vc_ref[...], wv_ref[...])
    ks[tc:, :] = _l2norm(_conv_silu(kl_ref[...], wk_ref[...]))
    vs[tc:, :] = _conv_silu(vl_ref[...], wv_ref[...])
    qs[tc:, :] = _l2norm(_conv_silu(ql_ref[...], wq_ref[...])) * (HEAD_DIM ** -0.5)
    if ctx_out:
        qs[0:tc, :] = _l2norm(_conv_silu(qc_ref[...], wq_ref[...])) * (HEAD_DIM ** -0.5)
    gpa[0:tc, :] = gpc_ref[...]
    gpa[tc:, :] = gpl_ref[...]
    gpta[0:n_ctx * tiles] = gptc_ref[...]
    gpta[n_ctx * tiles:] = gptl_ref[...]

    def prepare(js, with_out):
        rows = [pl.ds(pl.multiple_of(j * SB, SB), SB) for j in js]
        blocks = [(ks[r, :], vs[r, :], qs[r, :] if with_out else None, gpa[r, :],
                   jnp.concatenate([gpta[j * tiles + i] for i in range(tiles)], axis=1))
                  for j, r in zip(js, rows)]
        for j, r, res in zip(js, rows, _gdn_prepare(blocks)):
            for d, (u, w, kT, qd, qk) in enumerate(res):
                u_s[d][r, :] = u
                w_s[d][r, :] = w
                kT_s[d][j] = kT
                if with_out:
                    qd_s[d][r, :] = qd
                    qk_s[d][r, :] = qk

    def prepare_loop(lo, n, with_out):
        group = GD_PREP_GROUP if n % GD_PREP_GROUP == 0 else 1

        def body(i, carry):
            prepare([lo + i * group + g for g in range(group)], with_out)
            return carry
        lax.fori_loop(0, n // group, body, 0)

    def chunk_rows(j, c):
        return pl.ds(pl.multiple_of(j * SB + c * L, L), L)

    def step_load(d, j, c, with_out):
        rows = chunk_rows(j, c)
        cols = slice(c * L, (c + 1) * L)
        edge = gpa[pl.ds(pl.multiple_of(j * SB + c * L, L) + (0 if d else L - 1), 1), :]
        g_end = jnp.exp(edge[:, 7:8] if d else edge[:, 6:7])
        out_ops = (qd_s[d][rows, :], qk_s[d][rows, cols]) if with_out else None
        return g_end, u_s[d][rows, :], w_s[d][rows, :], kT_s[d][j, :, cols], out_ops

    def step_compute(ops, S):
        g_end, u, w, kT, out_ops = ops
        Sb = S.astype(BF16)
        v_new = (u - jnp.dot(w, Sb, preferred_element_type=F32)).astype(BF16)
        out = None
        if out_ops is not None:
            out = (jnp.dot(out_ops[0], Sb, preferred_element_type=F32)
                   + jnp.dot(out_ops[1], v_new, preferred_element_type=F32))
        return S * g_end + jnp.dot(kT, v_new, preferred_element_type=F32), out

    def chunk_pair(lo, n, i, c):
        return (lo + i, c), (lo + n - 1 - i, SB // L - 1 - c)

    def scan(lo, n, first, last, carry, with_out):
        def body(i, carry):
            for c in range(SB // L):
                where = chunk_pair(lo, n, i, c)
                res = [step_compute(step_load(d, *where[d], with_out), carry[d]) for d in range(2)]
                if with_out:
                    o_f[chunk_rows(*where[0]), :] = res[0][1]
                    o_b[chunk_rows(*where[1]), :] = res[1][1]
                carry = (res[0][0], res[1][0])
            return carry
        return lax.fori_loop(first, last, body, carry)

    zero = jnp.zeros((HEAD_DIM, HEAD_DIM), F32)
    prepare_loop(0, n_ctx, ctx_out)
    prepare_loop(n_ctx, n_lat, True)
    carry = scan(0, n_ctx, 0, n_ctx, (zero, zero), ctx_out)
    scan(n_ctx, n_lat, 0, n_lat, carry, True)

    def finish(rows, z_ref, y_ref):
        y_ref[...] = ((_rms(o_f[rows, :] + o_b[rows, :]) * norm_ref[...]) * _silu(z_ref[...])).astype(y_ref.dtype)

    finish(slice(tc, None), zl_ref, yl_ref)
    if ctx_out:
        finish(slice(0, tc), zc_ref, yc_ref)


def _gdn(zc, gpc, gptc, zl, gpl, gptl, conv, norm, ctx_out):
    nb, tc, _ = zc.shape
    tl = zl.shape[1]
    assert tc % GD_SUPER == 0 and tl % GD_SUPER == 0, "sequences are processed in whole 256-token superblocks"
    ctx_in = [zc, zc, gpc, gptc] + ([zc, zc] if ctx_out else [])
    ctx_specs = [_seq_spec(tc, C_GD_K), _seq_spec(tc, C_GD_V)] + _gp_specs(tc) + (
        [_seq_spec(tc, C_GD_Q), _seq_spec(tc, C_GD_Z)] if ctx_out else [])
    lat_in = [zl, zl, gpl, gptl, zl, zl]
    lat_specs = [_seq_spec(tl, C_GD_K), _seq_spec(tl, C_GD_V)] + _gp_specs(tl) + [
        _seq_spec(tl, C_GD_Q), _seq_spec(tl, C_GD_Z)]
    conv_spec = lambda off: pl.BlockSpec((3, HEAD_DIM), lambda b, h: (0, off // HEAD_DIM + h))
    out_spec = lambda rows: pl.BlockSpec((None, rows, HEAD_DIM), lambda b, h: (b, 0, h))
    out_shape = lambda rows: jax.ShapeDtypeStruct((nb, rows, BRANCH_W), BF16)
    rows = tc + tl
    seq = lambda dt: pltpu.VMEM((rows, HEAD_DIM), dt)
    two = lambda spec: [spec, spec]
    scratch = ([seq(F32), seq(F32), seq(F32), seq(F32), pltpu.VMEM((rows // ML_CHUNK, N_GATE, ML_CHUNK), F32),
                seq(F32), seq(F32)]
               + two(seq(F32)) + two(seq(BF16)) + two(pltpu.VMEM((rows // GD_SUPER, HEAD_DIM, GD_SUPER), BF16))
               + two(seq(BF16)) + two(pltpu.VMEM((rows, GD_SUPER), BF16)))
    res = pl.pallas_call(
        functools.partial(_gdn_kernel, ctx_out=ctx_out, n_ctx=tc // GD_SUPER, n_lat=tl // GD_SUPER),
        grid=(nb, N_HEADS),
        in_specs=ctx_specs + lat_specs + [conv_spec(0), conv_spec(BRANCH_W), conv_spec(2 * BRANCH_W),
                                          pl.BlockSpec((1, HEAD_DIM), lambda b, h: (0, 0))],
        out_specs=([out_spec(tc)] if ctx_out else []) + [out_spec(tl)],
        out_shape=([out_shape(tc)] if ctx_out else []) + [out_shape(tl)],
        scratch_shapes=scratch,
        compiler_params=_params(("parallel", "parallel")),
        name="gdn",
    )(*ctx_in, *lat_in, conv, conv, conv, norm)
    return (res[0], res[1]) if ctx_out else (None, res[0])


def _rope(x, cos, sin):
    lane = lax.broadcasted_iota(jnp.int32, x.shape, 1)
    quarter = HEAD_DIM // 4
    partner = jnp.where((lane & (2 * quarter - 1)) < quarter,
                        pltpu.roll(x, HEAD_DIM - quarter, 1), pltpu.roll(x, quarter, 1))
    return x * cos + partner * sin


def _attn_kernel(*refs, latent, tc):
    if latent:
        (q_ref, kc_ref, vc_ref, kl_ref, vl_ref, cos_ref, sin_ref, cosq_ref, sinq_ref, qn_ref, kn_ref,
         o_ref, k_scr, v_scr) = refs
    else:
        q_ref, kc_ref, vc_ref, qn_ref, kn_ref, o_ref, k_scr, v_scr = refs

    @pl.when(pl.program_id(2) == 0)
    def _():
        k_scr[0:tc, :] = (_rms(kc_ref[...]) * kn_ref[...]).astype(BF16)
        v_scr[0:tc, :] = vc_ref[...].astype(BF16)
        if latent:
            kl = _rope(_rms(kl_ref[...]) * kn_ref[...], cos_ref[...], sin_ref[...])
            k_scr[tc:, :] = kl.astype(BF16)
            v_scr[tc:, :] = vl_ref[...].astype(BF16)

    scale = HEAD_DIM ** -0.5
    for g in range(AT_Q_HEADS // AT_KV_HEADS):
        cols = slice(g * HEAD_DIM, (g + 1) * HEAD_DIM)
        q = _rms(q_ref[:, cols]) * qn_ref[...]
        if latent:
            q = _rope(q, cosq_ref[...], sinq_ref[...])
        s = lax.dot_general((q * scale).astype(BF16), k_scr[...], (((1,), (1,)), ((), ())),
                            preferred_element_type=F32)
        p = jnp.exp(s - jnp.max(s, axis=-1, keepdims=True))
        l = jnp.sum(p, axis=-1, keepdims=True)
        o_ref[:, cols] = (jnp.dot(p.astype(BF16), v_scr[...], preferred_element_type=F32) / l).astype(o_ref.dtype)


def _attention(zq, zc, zl, cos, sin, qn, kn):
    latent = zl is not None
    nb, tq_total, _ = zq.shape
    tc = zc.shape[1]
    tk = tc + (zl.shape[1] if latent else 0)
    gq = (AT_Q_HEADS // AT_KV_HEADS) * HEAD_DIM
    head_spec = lambda rows, col: pl.BlockSpec((None, rows, HEAD_DIM), lambda b, h, i: (b, 0, col // HEAD_DIM + h))
    vec_spec = pl.BlockSpec((1, HEAD_DIM), lambda b, h, i: (0, 0))
    in_specs = [pl.BlockSpec((None, TQ_ATTN, gq), lambda b, h, i: (b, i, C_AT_Q // gq + h)),
                head_spec(tc, C_AT_K), head_spec(tc, C_AT_V)]
    args = [zq, zc, zc]
    if latent:
        tl = zl.shape[1]
        in_specs += [head_spec(tl, C_AT_K), head_spec(tl, C_AT_V),
                     pl.BlockSpec((tl, HEAD_DIM), lambda b, h, i: (0, 0)),
                     pl.BlockSpec((tl, HEAD_DIM), lambda b, h, i: (0, 0)),
                     pl.BlockSpec((TQ_ATTN, HEAD_DIM), lambda b, h, i: (i, 0)),
                     pl.BlockSpec((TQ_ATTN, HEAD_DIM), lambda b, h, i: (i, 0))]
        args += [zl, zl, cos, sin, cos, sin]
    return pl.pallas_call(
        functools.partial(_attn_kernel, latent=latent, tc=tc),
        grid=(nb, AT_KV_HEADS, tq_total // TQ_ATTN),
        in_specs=in_specs + [vec_spec, vec_spec],
        out_specs=pl.BlockSpec((None, TQ_ATTN, gq), lambda b, h, i: (b, i, h)),
        out_shape=jax.ShapeDtypeStruct((nb, tq_total, BRANCH_W), BF16),
        scratch_shapes=[pltpu.VMEM((tk, HEAD_DIM), BF16), pltpu.VMEM((tk, HEAD_DIM), BF16)],
        compiler_params=_params(("parallel", "parallel", "arbitrary")),
        name="attention",
    )(*args, qn, kn)


def _merge_kernel(h_ref, xcol_ref, modcol_ref, y0_ref, y1_ref, y2_ref, y3_ref,
                  wg0_ref, wg1_ref, wg2_ref, wg3_ref, wb_ref, wo_ref, o_ref, m_scr):
    s = pl.program_id(2)
    n_gate = D_MODEL // TN_MERGE
    ys = (y0_ref, y1_ref, y2_ref, y3_ref)
    wgs = (wg0_ref, wg1_ref, wg2_ref, wg3_ref)

    @pl.when(s < n_gate)
    def _():
        h = h_ref[...]
        acc = jnp.zeros((TM_MERGE, TN_MERGE), F32)
        for n in range(N_BRANCH):
            gate = jnp.dot(h, wgs[n][...], preferred_element_type=F32)
            acc = acc + _sigmoid(gate) * jnp.dot(ys[n][...], wb_ref[n], preferred_element_type=F32)
        m_scr[s] = acc.astype(BF16)

    @pl.when(s >= n_gate)
    def _():
        acc = jnp.zeros((TM_MERGE, TN_OUT), F32)
        for kk in range(n_gate):
            acc = acc + jnp.dot(m_scr[kk], wo_ref[kk * TN_MERGE:(kk + 1) * TN_MERGE, :],
                                preferred_element_type=F32)
        o_ref[...] = xcol_ref[...] + modcol_ref[5:6, :] * acc


def _merge(x, h, mod, ys, w_gate, w_branch, w_out, layer, mod_row):
    groups, rows, _ = x.shape
    n_gate = D_MODEL // TN_MERGE
    n_out = D_MODEL // TN_OUT
    mrow = (lambda b: b) if mod_row is None else (lambda b: mod_row)
    gate_col = lambda s: jnp.minimum(s, n_gate - 1)
    out_col = lambda s: jnp.maximum(s - n_gate, 0)
    y_spec = pl.BlockSpec((None, TM_MERGE, BRANCH_W), lambda b, i, s: (b, i, 0))
    wg_spec = lambda n: pl.BlockSpec((None, D_MODEL, TN_MERGE),
                                     lambda b, i, s: (layer, 0, n * n_gate + gate_col(s)))
    return pl.pallas_call(
        _merge_kernel,
        grid=(groups, rows // TM_MERGE, n_gate + n_out),
        in_specs=[pl.BlockSpec((None, TM_MERGE, D_MODEL), lambda b, i, s: (b, i, 0)),
                  pl.BlockSpec((None, TM_MERGE, TN_OUT), lambda b, i, s: (b, i, out_col(s))),
                  pl.BlockSpec((None, N_MOD, TN_OUT), lambda b, i, s: (mrow(b), 0, out_col(s))),
                  y_spec, y_spec, y_spec, y_spec,
                  wg_spec(0), wg_spec(1), wg_spec(2), wg_spec(3),
                  pl.BlockSpec((None, N_BRANCH, BRANCH_W, TN_MERGE), lambda b, i, s: (layer, 0, 0, gate_col(s))),
                  pl.BlockSpec((None, D_MODEL, TN_OUT), lambda b, i, s: (layer, 0, out_col(s)))],
        out_specs=pl.BlockSpec((None, TM_MERGE, TN_OUT), lambda b, i, s: (b, i, out_col(s))),
        out_shape=jax.ShapeDtypeStruct(x.shape, F32),
        scratch_shapes=[pltpu.VMEM((n_gate, TM_MERGE, TN_MERGE), BF16)],
        compiler_params=_params(("parallel", "parallel", "arbitrary")),
        name="merge",
    )(h, x, mod, *ys, w_gate, w_gate, w_gate, w_gate, w_branch, w_out)


def _rope_tables(seq):
    n = HEAD_DIM // 4
    inv = ROPE_THETA ** (-jnp.arange(n, dtype=F32) / n)
    pos = jnp.arange(seq)
    ar = (pos // GRID_W).astype(F32)[:, None] * inv
    ac = (pos % GRID_W).astype(F32)[:, None] * inv
    cos = jnp.concatenate([jnp.cos(ar), jnp.cos(ar), jnp.cos(ac), jnp.cos(ac)], axis=1)
    sin = jnp.concatenate([-jnp.sin(ar), jnp.sin(ar), -jnp.sin(ac), jnp.sin(ac)], axis=1)
    return cos, sin


def _gate_params(ml_if_bias, gd_a_log, gd_dt_bias):
    zeros = jnp.zeros((2, N_HEADS), F32)
    bias = jnp.concatenate([ml_if_bias.reshape(4, N_HEADS), zeros, gd_dt_bias], axis=0).T
    alog = jnp.concatenate([jnp.zeros((6, N_HEADS), F32), gd_a_log], axis=0).T
    pad = lambda a: jnp.pad(a.reshape(1, N_HEADS * N_GATE), ((0, 0), (0, LANES - N_HEADS * N_GATE)))
    col = lambda a: a.reshape(N_HEADS * N_GATE, 1)
    return pad(bias), pad(alog), col(bias), col(alog)


def _gate_weights(w_in):
    depth = w_in.shape[0]
    g = jnp.concatenate([w_in[..., R_ML_IF:R_ML_IF + 16], w_in[..., R_GD_BA:R_GD_BA + 16]], axis=-1)
    g = g.reshape(depth, D_MODEL, N_GATE, N_HEADS).transpose(0, 1, 3, 2).reshape(depth, D_MODEL, N_HEADS * N_GATE)
    w_gc = jnp.pad(g, ((0, 0), (0, 0), (0, LANES - N_HEADS * N_GATE)))
    w_gr = g.transpose(0, 2, 1)
    return w_gc.astype(BF16), w_gr.astype(BF16)


def kernel(x, c, ctx, c_ctx, mod_w, mod_b, ffn1_norm, ffn1_w_in, ffn1_w_out, mix_norm, w_in, sg_norm, sg_w, sg_b, ml_if_bias, ml_norm, gd_conv, gd_a_log, gd_dt_bias, gd_norm, at_q_norm, at_k_norm, w_branch, w_out, ffn2_norm, ffn2_w_in, ffn2_w_out):
    nb, seq, _ = x.shape
    tc = ctx.shape[1]
    depth = mod_w.shape[0]
    ctx_row = nb
    cos, sin = _rope_tables(seq)

    cc = jnp.zeros((MOD_ROWS, D_MODEL), F32).at[:nb].set(c).at[ctx_row].set(c_ctx)
    mods = _modulation(cc, mod_w, mod_b).reshape(depth, MOD_ROWS, N_MOD, D_MODEL)

    f1_in, f1_out, f2_in, f2_out = [_to_bf16(w) for w in (ffn1_w_in, ffn1_w_out, ffn2_w_in, ffn2_w_out)]
    w_main, w_gate = _split_w_in(w_in)
    w_gc_all, w_gr_all = _gate_weights(w_in)
    wb, wo = w_branch.astype(BF16), w_out.astype(BF16)
    flat = lambda a: a.reshape(1, nb * tc, a.shape[-1])
    unflat = lambda a: a.reshape(nb, tc, a.shape[-1])
    xc = ctx
    for l in range(depth):
        last = l == depth - 1
        ctx_out = not last
        mod = mods[l]
        w_gc, w_gr = w_gc_all[l], w_gr_all[l]
        bias_c, alog_c, bias_r, alog_r = _gate_params(ml_if_bias[l], gd_a_log[l], gd_dt_bias[l])
        sgw = sg_w[l].astype(BF16)
        sg_bias = jnp.repeat(sg_b[l].T, BRANCH_W // SG_GROUPS, axis=1)
        sgn = sg_norm[l].reshape(1, BRANCH_W)
        mln = ml_norm[l].reshape(1, BRANCH_W)
        gdn_g = gd_norm[l].reshape(1, HEAD_DIM)
        qn, kn = at_q_norm[l].reshape(1, HEAD_DIM), at_k_norm[l].reshape(1, HEAD_DIM)

        x = _ffn(x, mod, ffn1_norm[l], f1_in, f1_out, l, 0, None)
        xc = unflat(_ffn(flat(xc), mod, ffn1_norm[l], f1_in, f1_out, l, 0, ctx_row))

        zl, gcl, grl, hl = _inproj(x, mod, mix_norm[l], w_main, l, w_gc, w_gr, N_MAIN, None)
        zc, gcc, grc, hc = _inproj(flat(xc), mod, mix_norm[l], w_main, l, w_gc, w_gr,
                                   N_MAIN if ctx_out else KV_MAIN, ctx_row)
        zc = unflat(zc)
        gcc = unflat(gcc)
        grc = grc.reshape(N_HEADS * N_GATE, nb, tc).transpose(1, 0, 2)
        gpl, gptl = _gateprep(gcl, grl, bias_c, alog_c, bias_r, alog_r)
        gpc, gptc = _gateprep(gcc, grc, bias_c, alog_c, bias_r, alog_r)

        y_sg_l = _spatial_gating(zl, sgn, sgw, sg_bias)
        y_ml_c, y_ml_l = _mlstm(zc, gpc, gptc, zl, gpl, gptl, mln, ctx_out)
        y_gd_c, y_gd_l = _gdn(zc, gpc, gptc, zl, gpl, gptl, gd_conv[l], gdn_g, ctx_out)
        y_at_l = _attention(zl, zc, zl, cos, sin, qn, kn)
        x = _merge(x, hl, mod, [y_sg_l, y_ml_l, y_gd_l, y_at_l], w_gate, wb, wo, l, None)
        if ctx_out:
            y_sg_c = _spatial_gating(zc, sgn, sgw, sg_bias)
            y_at_c = _attention(zc, zc, None, None, None, qn, kn)
            xc = unflat(_merge(flat(xc), hc, mod, [flat(y) for y in (y_sg_c, y_ml_c, y_gd_c, y_at_c)],
                               w_gate, wb, wo, l, ctx_row))

        x = _ffn(x, mod, ffn2_norm[l], f2_in, f2_out, l, 6, None)
        if ctx_out:
            xc = unflat(_ffn(flat(xc), mod, ffn2_norm[l], f2_in, f2_out, l, 6, ctx_row))
    return x
```

```python
import functools
import math

import jax
import jax.numpy as jnp
from jax import lax
from jax.experimental import pallas as pl
from jax.experimental.pallas import tpu as pltpu

F32 = jnp.float32
BF16 = jnp.bfloat16
EPS = 1e-6

D_MODEL = 2048
D_FF = 5632
N_MOD = 9
N_BRANCH = 4
BRANCH_W = 512
HEAD_DIM = 128
N_HEADS = 4
SG_GROUPS = 4
ML_CHUNK = 128
GD_CHUNK = 128
GD_SUPER = 256
GD_PREP_GROUP = 2
AT_Q_HEADS = 4
AT_KV_HEADS = 2
GRID_W = 64
ROPE_THETA = 10000.0
FFN_RESIDUAL = 0.5

LANES = 128
N_GATE = 8
MOD_ROWS = 16

C_ML_K, C_ML_V = 0, 512
C_GD_K, C_GD_V = 1024, 1536
C_AT_K, C_AT_V = 2048, 2304
KV_MAIN = 2560
C_SG_U, C_SG_V = 2560, 3072
C_ML_Q, C_ML_O = 3584, 4096
C_GD_Q, C_GD_Z = 4608, 5120
C_AT_Q = 5632
N_MAIN = 6144
R_ML_IF = 1024
R_GD_KV = 1040
R_GD_BA = 2064
R_AT_KV = 2080
R_REST = 2592
R_GATE = 6176

VMEM_LIMIT = 56 * 1024 * 1024

TM_FFN = 512
TF_FFN = 512
TM_PROJ = 1024
TN_PROJ = 512
TM_MERGE = 1024
TN_MERGE = 256
TN_OUT = 512
TQ_ATTN = 256
TR_SG = 256
TG_GATE = 256
CAST_BLOCK_BYTES = 8 * 1024 * 1024
SPLIT_ROWS = 128
TK_MOD = 128


def _params(sem):
    return pltpu.CompilerParams(dimension_semantics=sem, vmem_limit_bytes=VMEM_LIMIT)


def _mm(a, b):
    return jnp.dot(a.astype(BF16), b.astype(BF16), preferred_element_type=F32)


def _mm_nt(a, b):
    return lax.dot_general(a.astype(BF16), b.astype(BF16), (((1,), (1,)), ((), ())),
                           preferred_element_type=F32)


def _mm_tn(a, b):
    return lax.dot_general(a.astype(BF16), b.astype(BF16), (((0,), (0,)), ((), ())),
                           preferred_element_type=F32)


def _sigmoid(x):
    return 1.0 / (1.0 + jnp.exp(-x))


def _silu(x):
    return x * _sigmoid(x)


def _softplus(x):
    return jnp.maximum(x, 0.0) + jnp.log1p(jnp.exp(-jnp.abs(x)))


def _gelu_tanh(x):
    c = math.sqrt(2.0 / math.pi)
    return x * (0.5 * (1.0 + jnp.tanh(c * (x + 0.044715 * (x * x * x)))))


def _rms(x):
    return x * lax.rsqrt(jnp.mean(x * x, axis=-1, keepdims=True) + EPS)


def _norm_mod(x, gain, shift, scale):
    return (_rms(x) * gain) * (1.0 + scale) + shift


def _cast_kernel(x_ref, o_ref):
    o_ref[...] = x_ref[...].astype(o_ref.dtype)


def _to_bf16(w):
    depth, r, c = w.shape
    rows = 1 << (max(16, min(r, CAST_BLOCK_BYTES // (4 * c))).bit_length() - 1)
    while r % rows:
        rows //= 2
    return pl.pallas_call(
        _cast_kernel,
        grid=(depth, r // rows),
        in_specs=[pl.BlockSpec((None, rows, c), lambda l, i: (l, i, 0))],
        out_specs=pl.BlockSpec((None, rows, c), lambda l, i: (l, i, 0)),
        out_shape=jax.ShapeDtypeStruct(w.shape, BF16),
        compiler_params=_params(("parallel", "parallel")),
        name="cast_bf16",
    )(w)


def _split_w_in_kernel(w_ref, main_ref, gate_ref):
    w = w_ref[...]
    main_ref[...] = jnp.concatenate([w[:, :R_ML_IF], w[:, R_GD_KV:R_GD_BA], w[:, R_AT_KV:R_GATE]],
                                    axis=1).astype(BF16)
    gate_ref[...] = w[:, R_GATE:].astype(BF16)


def _split_w_in(w_in):
    depth, d, n = w_in.shape
    n_gate = n - R_GATE
    return pl.pallas_call(
        _split_w_in_kernel,
        grid=(depth, d // SPLIT_ROWS),
        in_specs=[pl.BlockSpec((None, SPLIT_ROWS, n), lambda l, i: (l, i, 0))],
        out_specs=[pl.BlockSpec((None, SPLIT_ROWS, N_MAIN), lambda l, i: (l, i, 0)),
                   pl.BlockSpec((None, SPLIT_ROWS, n_gate), lambda l, i: (l, i, 0))],
        out_shape=[jax.ShapeDtypeStruct((depth, d, N_MAIN), BF16),
                   jax.ShapeDtypeStruct((depth, d, n_gate), BF16)],
        compiler_params=_params(("parallel", "parallel")),
        name="split_w_in",
    )(w_in)


def _mod_kernel(cc_ref, w_ref, b_ref, o_ref):
    @pl.when(pl.program_id(1) == 0)
    def _():
        o_ref[...] = jnp.broadcast_to(b_ref[...], o_ref.shape)

    o_ref[...] += _mm(_silu(cc_ref[...]), w_ref[...])


def _modulation(cc, mod_w, mod_b):
    depth, d, n = mod_w.shape
    ccs = cc.reshape(MOD_ROWS, d // TK_MOD, TK_MOD).transpose(1, 0, 2)
    return pl.pallas_call(
        _mod_kernel,
        grid=(depth, d // TK_MOD),
        in_specs=[pl.BlockSpec((None, MOD_ROWS, TK_MOD), lambda l, k: (k, 0, 0)),
                  pl.BlockSpec((None, TK_MOD, n), lambda l, k: (l, k, 0)),
                  pl.BlockSpec((None, 1, n), lambda l, k: (l, 0, 0))],
        out_specs=pl.BlockSpec((None, MOD_ROWS, n), lambda l, k: (l, 0, 0)),
        out_shape=jax.ShapeDtypeStruct((depth, MOD_ROWS, n), F32),
        compiler_params=_params(("parallel", "arbitrary")),
        name="modulation",
    )(ccs, mod_w, mod_b.reshape(depth, 1, n))


def _ffn_kernel(x_ref, mod_ref, g_ref, wa_ref, wb_ref, wo_ref, o_ref, h_scr, acc_scr, *, base):
    j = pl.program_id(2)

    @pl.when(j == 0)
    def _():
        h = _norm_mod(x_ref[...], g_ref[...], mod_ref[base:base + 1, :], mod_ref[base + 1:base + 2, :])
        h_scr[...] = h.astype(BF16)
        acc_scr[...] = jnp.zeros_like(acc_scr)

    h = h_scr[...]
    a = jnp.dot(h, wa_ref[...], preferred_element_type=F32)
    b = jnp.dot(h, wb_ref[...], preferred_element_type=F32)
    acc_scr[...] += _mm(_silu(a) * b, wo_ref[...])

    @pl.when(j == pl.num_programs(2) - 1)
    def _():
        o_ref[...] = x_ref[...] + (FFN_RESIDUAL * mod_ref[base + 2:base + 3, :]) * acc_scr[...]


def _ffn(x, mod, gain, w_in, w_out, layer, base, mod_row):
    groups, rows, _ = x.shape
    nj = D_FF // TF_FFN
    mod_idx = (lambda b, i, j: (b, 0, 0)) if mod_row is None else (lambda b, i, j: (mod_row, 0, 0))
    return pl.pallas_call(
        functools.partial(_ffn_kernel, base=base),
        grid=(groups, rows // TM_FFN, nj),
        in_specs=[pl.BlockSpec((None, TM_FFN, D_MODEL), lambda b, i, j: (b, i, 0)),
                  pl.BlockSpec((None, N_MOD, D_MODEL), mod_idx),
                  pl.BlockSpec((1, D_MODEL), lambda b, i, j: (0, 0)),
                  pl.BlockSpec((None, D_MODEL, TF_FFN), lambda b, i, j: (layer, 0, j)),
                  pl.BlockSpec((None, D_MODEL, TF_FFN), lambda b, i, j: (layer, 0, j + nj)),
                  pl.BlockSpec((None, TF_FFN, D_MODEL), lambda b, i, j: (layer, j, 0))],
        out_specs=pl.BlockSpec((None, TM_FFN, D_MODEL), lambda b, i, j: (b, i, 0)),
        out_shape=jax.ShapeDtypeStruct(x.shape, F32),
        scratch_shapes=[pltpu.VMEM((TM_FFN, D_MODEL), BF16), pltpu.VMEM((TM_FFN, D_MODEL), F32)],
        compiler_params=_params(("parallel", "parallel", "arbitrary")),
        name="ffn",
    )(x, mod, gain.reshape(1, D_MODEL), w_in, w_in, w_out)


def _inproj_kernel(x_ref, mod_ref, g_ref, w_ref, wgc_ref, wgr_ref, z_ref, gc_ref, gr_ref, h_ref):
    n = pl.program_id(2)

    @pl.when(n == 0)
    def _():
        h = _norm_mod(x_ref[...], g_ref[...], mod_ref[3:4, :], mod_ref[4:5, :]).astype(BF16)
        h_ref[...] = h
        gc_ref[...] = jnp.dot(h, wgc_ref[...], preferred_element_type=F32)
        gr_ref[...] = lax.dot_general(wgr_ref[...], h, (((1,), (1,)), ((), ())), preferred_element_type=F32)

    z_ref[...] = jnp.dot(h_ref[...], w_ref[...], preferred_element_type=F32)


def _inproj(x, mod, gain, w_main, layer, w_gc, w_gr, n_cols, mod_row):
    groups, rows, _ = x.shape
    mod_idx = (lambda b, i, n: (b, 0, 0)) if mod_row is None else (lambda b, i, n: (mod_row, 0, 0))
    return pl.pallas_call(
        _inproj_kernel,
        grid=(groups, rows // TM_PROJ, n_cols // TN_PROJ),
        in_specs=[pl.BlockSpec((None, TM_PROJ, D_MODEL), lambda b, i, n: (b, i, 0)),
                  pl.BlockSpec((None, N_MOD, D_MODEL), mod_idx),
                  pl.BlockSpec((1, D_MODEL), lambda b, i, n: (0, 0)),
                  pl.BlockSpec((None, D_MODEL, TN_PROJ), lambda b, i, n: (layer, 0, n)),
                  pl.BlockSpec((D_MODEL, LANES), lambda b, i, n: (0, 0)),
                  pl.BlockSpec((N_HEADS * N_GATE, D_MODEL), lambda b, i, n: (0, 0))],
        out_specs=[pl.BlockSpec((None, TM_PROJ, TN_PROJ), lambda b, i, n: (b, i, n)),
                   pl.BlockSpec((None, TM_PROJ, LANES), lambda b, i, n: (b, i, 0)),
                   pl.BlockSpec((None, N_HEADS * N_GATE, TM_PROJ), lambda b, i, n: (b, 0, i)),
                   pl.BlockSpec((None, TM_PROJ, D_MODEL), lambda b, i, n: (b, i, 0))],
        out_shape=[jax.ShapeDtypeStruct((groups, rows, n_cols), F32),
                   jax.ShapeDtypeStruct((groups, rows, LANES), F32),
                   jax.ShapeDtypeStruct((groups, N_HEADS * N_GATE, rows), F32),
                   jax.ShapeDtypeStruct((groups, rows, D_MODEL), BF16)],
        compiler_params=_params(("parallel", "parallel", "arbitrary")),
        name="inproj",
    )(x, mod, gain.reshape(1, D_MODEL), w_main, w_gc, w_gr)


def _split3(e):
    e1 = e.astype(BF16)
    r1 = e - e1.astype(F32)
    e2 = r1.astype(BF16)
    e3 = (r1 - e2.astype(F32)).astype(BF16)
    return e1, e2, e3


def _gate_values(pre, a_log, cls):
    neg_sp = -_softplus(-pre)
    return jnp.where(cls < 2, pre,
                     jnp.where(cls < 4, neg_sp,
                               jnp.where(cls < 6, _sigmoid(pre), -jnp.exp(a_log) * _softplus(pre))))


def _gateprep_kernel(gc_ref, gr_ref, bc_ref, ac_ref, br_ref, ar_ref, oc_ref, or_ref):
    for i in range(TG_GATE // ML_CHUNK):
        tile = slice(i * ML_CHUNK, (i + 1) * ML_CHUNK)
        out_c, or_ref[i] = _gateprep_tile(gc_ref[tile, :], gr_ref[:, tile], bc_ref[...], ac_ref[...],
                                          br_ref[...], ar_ref[...])
        for h in range(N_HEADS):
            oc_ref[tile, h * LANES:(h + 1) * LANES] = (
                out_c if h == 0 else pltpu.roll(out_c, LANES - h * N_GATE, 1))


def _gateprep_tile(gc, gr, bias_c, alog_c, bias_r, alog_r):
    L = ML_CHUNK
    half = GD_CHUNK
    pre = gc + bias_c
    cls = lax.broadcasted_iota(jnp.int32, pre.shape, 1) & (N_GATE - 1)
    t = lax.broadcasted_iota(jnp.int32, pre.shape, 0)
    e = _gate_values(pre, alog_c, cls)
    tri = (lax.broadcasted_iota(jnp.int32, (L, L), 1) <= lax.broadcasted_iota(jnp.int32, (L, L), 0)).astype(BF16)
    e1, e2, e3 = _split3(e)
    p = (jnp.dot(tri, e1, preferred_element_type=F32) + jnp.dot(tri, e2, preferred_element_type=F32)
         + jnp.dot(tri, e3, preferred_element_type=F32))
    tot = p[L - 1:L, :]
    mid = p[half - 1:half, :]
    second = t >= half
    pre64 = p - jnp.where(second, mid, 0.0)
    suf64 = jnp.where(second, tot, mid) - p + e
    cum = jnp.where(cls == 2, p, jnp.where(cls == 3, tot - p + e, jnp.where(cls == 6, pre64, suf64)))
    out_c = jnp.where((cls == 2) | (cls == 3) | (cls == 6) | (cls == 7), cum, e)

    pre_r = gr + bias_r
    cls_r = lax.broadcasted_iota(jnp.int32, pre_r.shape, 0) & (N_GATE - 1)
    t_r = lax.broadcasted_iota(jnp.int32, pre_r.shape, 1)
    e_r = _gate_values(pre_r, alog_r, cls_r)
    tri_r = (lax.broadcasted_iota(jnp.int32, (L, L), 0) <= lax.broadcasted_iota(jnp.int32, (L, L), 1)).astype(BF16)
    f1, f2, f3 = _split3(e_r)
    pr = (jnp.dot(f1, tri_r, preferred_element_type=F32) + jnp.dot(f2, tri_r, preferred_element_type=F32)
          + jnp.dot(f3, tri_r, preferred_element_type=F32))
    tot_r = pr[:, L - 1:L]
    mid_r = pr[:, half - 1:half]
    second_r = t_r >= half
    pre64_r = pr - jnp.where(second_r, mid_r, 0.0)
    suf64_r = jnp.where(second_r, tot_r, mid_r) - pr + e_r
    cum_r = jnp.where(cls_r == 2, pr, jnp.where(cls_r == 3, tot_r - pr + e_r, jnp.where(cls_r == 6, pre64_r, suf64_r)))
    return out_c, jnp.where((cls_r == 2) | (cls_r == 3) | (cls_r == 6) | (cls_r == 7), cum_r, e_r)


def _gateprep(gc, gr, bias_c, alog_c, bias_r, alog_r):
    groups, rows, _ = gc.shape
    nt = rows // ML_CHUNK
    hw = N_HEADS * LANES
    hg = N_HEADS * N_GATE
    tiles = TG_GATE // ML_CHUNK
    return pl.pallas_call(
        _gateprep_kernel,
        grid=(groups, rows // TG_GATE),
        in_specs=[pl.BlockSpec((None, TG_GATE, LANES), lambda b, c: (b, c, 0)),
                  pl.BlockSpec((None, hg, TG_GATE), lambda b, c: (b, 0, c)),
                  pl.BlockSpec((1, LANES), lambda b, c: (0, 0)),
                  pl.BlockSpec((1, LANES), lambda b, c: (0, 0)),
                  pl.BlockSpec((hg, 1), lambda b, c: (0, 0)),
                  pl.BlockSpec((hg, 1), lambda b, c: (0, 0))],
        out_specs=[pl.BlockSpec((None, TG_GATE, hw), lambda b, c: (b, c, 0)),
                   pl.BlockSpec((None, tiles, hg, ML_CHUNK), lambda b, c: (b, c, 0, 0))],
        out_shape=[jax.ShapeDtypeStruct((groups, rows, hw), F32),
                   jax.ShapeDtypeStruct((groups, nt, hg, ML_CHUNK), F32)],
        compiler_params=_params(("parallel", "parallel")),
        name="gateprep",
    )(gc, gr, bias_c, alog_c, bias_r, alog_r)


def _sg_kernel(u_ref, v_ref, gn_ref, w_ref, bias_ref, o_ref):
    gw = BRANCH_W // SG_GROUPS
    for ch in range(TR_SG // ML_CHUNK):
        rows = slice(ch * ML_CHUNK, (ch + 1) * ML_CHUNK)
        v = _rms(_gelu_tanh(v_ref[rows, :])) * gn_ref[...]
        u = _gelu_tanh(u_ref[rows, :])
        for g in range(SG_GROUPS):
            cols = slice(g * gw, (g + 1) * gw)
            s = _mm(w_ref[g], v[:, cols]) + bias_ref[:, cols]
            o_ref[rows, cols] = (u[:, cols] * s).astype(o_ref.dtype)


def _spatial_gating(z, gn, w_s, bias_full):
    groups, rows, _ = z.shape
    ub, vb = C_SG_U // BRANCH_W, C_SG_V // BRANCH_W
    return pl.pallas_call(
        _sg_kernel,
        grid=(groups, rows // TR_SG),
        in_specs=[pl.BlockSpec((None, TR_SG, BRANCH_W), lambda b, i: (b, i, ub)),
                  pl.BlockSpec((None, TR_SG, BRANCH_W), lambda b, i: (b, i, vb)),
                  pl.BlockSpec((1, BRANCH_W), lambda b, i: (0, 0)),
                  pl.BlockSpec((SG_GROUPS, ML_CHUNK, ML_CHUNK), lambda b, i: (0, 0, 0)),
                  pl.BlockSpec((ML_CHUNK, BRANCH_W), lambda b, i: (0, 0))],
        out_specs=pl.BlockSpec((None, TR_SG, BRANCH_W), lambda b, i: (b, i, 0)),
        out_shape=jax.ShapeDtypeStruct((groups, rows, BRANCH_W), BF16),
        compiler_params=_params(("parallel", "parallel")),
        name="spatial_gating",
    )(z, z, gn, w_s, bias_full)


def _mlstm_chunk(q, k, v, gp, gpt, state, rev, with_out):
    L = ML_CHUNK
    j_i, j_b = (1, 3) if rev else (0, 2)
    li_c = gp[:, j_i:j_i + 1]
    b_c = gp[:, j_b:j_b + 1]
    b_end = b_c[0:1, :] if rev else b_c[L - 1:L, :]
    ks = k * (HEAD_DIM ** -0.5)
    a = b_end - b_c + li_c
    m_loc = jnp.max(a, axis=0, keepdims=True)
    wgt = jnp.broadcast_to(jnp.exp(a - m_loc), (L, HEAD_DIM))
    ct_loc = _mm_tn(ks, jnp.concatenate([v * wgt, wgt], axis=1))
    ct, m = state
    m_new = jnp.maximum(b_end + m, m_loc)
    sp = jnp.exp(b_end + m - m_new)
    sl = jnp.exp(m_loc - m_new)
    new_state = (sp * ct + sl * ct_loc, m_new)
    if not with_out:
        return new_state, None
    li_r = gpt[j_i:j_i + 1, :]
    b_r = gpt[j_b:j_b + 1, :]
    t = lax.broadcasted_iota(jnp.int32, (L, L), 0)
    s = lax.broadcasted_iota(jnp.int32, (L, L), 1)
    incl = (s >= t) if rev else (s <= t)
    e = jnp.where(incl, li_r - b_r, -jnp.inf)
    c = jnp.broadcast_to(jnp.maximum(jnp.max(e, axis=1, keepdims=True), m), (L, L))
    p = jnp.exp(e - c) * _mm_nt(q, ks)
    inter = jnp.exp(m - c)
    pv = _mm(p, jnp.concatenate([v, jnp.ones_like(v)], axis=1))
    qc = _mm(q, ct)
    num = pv[:, :HEAD_DIM] + inter * qc[:, :HEAD_DIM]
    den = pv[:, HEAD_DIM:] + inter * qc[:, HEAD_DIM:]
    m_t = jnp.broadcast_to(b_c, (L, L)) + c
    return new_state, num / jnp.maximum(jnp.abs(den), jnp.exp(-m_t))


def _mlstm_kernel(*refs, ctx_out, n_ctx, n_lat):
    if ctx_out:
        (kc_ref, vc_ref, gpc_ref, gptc_ref, qc_ref, oc_ref, kl_ref, vl_ref, gpl_ref, gptl_ref, ql_ref, ol_ref,
         norm_ref, yc_ref, yl_ref, hfc, hbc, hfl, hbl) = refs
    else:
        (kc_ref, vc_ref, gpc_ref, gptc_ref, kl_ref, vl_ref, gpl_ref, gptl_ref, ql_ref, ol_ref,
         norm_ref, yl_ref, hfl, hbl) = refs
        qc_ref = oc_ref = yc_ref = hfc = hbc = None
    L = ML_CHUNK

    def run(k_ref, v_ref, gp_ref, gpt_ref, q_ref, hf, hb, n_chunks, st_f, st_b, with_out):
        def body(j, carry):
            st_f, st_b = carry
            rf = pl.ds(pl.multiple_of(j * L, L), L)
            jb = n_chunks - 1 - j
            rb = pl.ds(pl.multiple_of(jb * L, L), L)
            st_f, h_f = _mlstm_chunk(q_ref[rf, :] if with_out else None, k_ref[rf, :], v_ref[rf, :],
                                     gp_ref[rf, :], gpt_ref[j], st_f, False, with_out)
            st_b, h_b = _mlstm_chunk(q_ref[rb, :] if with_out else None, k_ref[rb, :], v_ref[rb, :],
                                     gp_ref[rb, :], gpt_ref[jb], st_b, True, with_out)
            if with_out:
                hf[rf, :] = h_f
                hb[rb, :] = h_b
            return st_f, st_b
        return lax.fori_loop(0, n_chunks, body, (st_f, st_b))

    zero = (jnp.zeros((HEAD_DIM, 2 * HEAD_DIM), F32), jnp.zeros((1, 1), F32))
    st_f, st_b = run(kc_ref, vc_ref, gpc_ref, gptc_ref, qc_ref, hfc, hbc, n_ctx, zero, zero, ctx_out)
    run(kl_ref, vl_ref, gpl_ref, gptl_ref, ql_ref, hfl, hbl, n_lat, st_f, st_b, True)

    def finish(hf, hb, o_ref, y_ref):
        y_ref[...] = (_sigmoid(o_ref[...]) * (_rms(hf[...] + hb[...]) * norm_ref[...])).astype(y_ref.dtype)

    finish(hfl, hbl, ol_ref, yl_ref)
    if ctx_out:
        finish(hfc, hbc, oc_ref, yc_ref)


def _seq_spec(rows, col):
    return pl.BlockSpec((None, rows, HEAD_DIM), lambda b, h: (b, 0, col // HEAD_DIM + h))


def _gp_specs(rows):
    return [pl.BlockSpec((None, rows, LANES), lambda b, h: (b, 0, h)),
            pl.BlockSpec((None, rows // ML_CHUNK, N_GATE, ML_CHUNK), lambda b, h: (b, 0, h, 0))]


def _mlstm(zc, gpc, gptc, zl, gpl, gptl, norm, ctx_out):
    nb, tc, _ = zc.shape
    tl = zl.shape[1]
    ctx_in = [zc, zc, gpc, gptc] + ([zc, zc] if ctx_out else [])
    ctx_specs = [_seq_spec(tc, C_ML_K), _seq_spec(tc, C_ML_V)] + _gp_specs(tc) + (
        [_seq_spec(tc, C_ML_Q), _seq_spec(tc, C_ML_O)] if ctx_out else [])
    lat_in = [zl, zl, gpl, gptl, zl, zl]
    lat_specs = [_seq_spec(tl, C_ML_K), _seq_spec(tl, C_ML_V)] + _gp_specs(tl) + [
        _seq_spec(tl, C_ML_Q), _seq_spec(tl, C_ML_O)]
    out_spec = lambda rows: pl.BlockSpec((None, rows, HEAD_DIM), lambda b, h: (b, 0, h))
    out_shape = lambda rows: jax.ShapeDtypeStruct((nb, rows, BRANCH_W), BF16)
    scr = lambda rows: [pltpu.VMEM((rows, HEAD_DIM), F32), pltpu.VMEM((rows, HEAD_DIM), F32)]
    res = pl.pallas_call(
        functools.partial(_mlstm_kernel, ctx_out=ctx_out, n_ctx=tc // ML_CHUNK, n_lat=tl // ML_CHUNK),
        grid=(nb, N_HEADS),
        in_specs=ctx_specs + lat_specs + [pl.BlockSpec((1, HEAD_DIM), lambda b, h: (0, h))],
        out_specs=([out_spec(tc)] if ctx_out else []) + [out_spec(tl)],
        out_shape=([out_shape(tc)] if ctx_out else []) + [out_shape(tl)],
        scratch_shapes=(scr(tc) if ctx_out else []) + scr(tl),
        compiler_params=_params(("parallel", "parallel")),
        name="mlstm",
    )(*ctx_in, *lat_in, norm)
    return (res[0], res[1]) if ctx_out else (None, res[0])


def _conv_silu(x, w3):
    rows = x.shape[0]
    t = lax.broadcasted_iota(jnp.int32, x.shape, 0)
    prev = jnp.where(t == 0, 0.0, pltpu.roll(x, 1, 0))
    nxt = jnp.where(t == rows - 1, 0.0, pltpu.roll(x, rows - 1, 0))
    return _silu(w3[0:1, :] * prev + w3[1:2, :] * x + w3[2:3, :] * nxt)


def _l2norm(x):
    return x * lax.rsqrt(jnp.sum(x * x, axis=-1, keepdims=True) + EPS)


def _chunk_totals(gc_r, rev):
    L = GD_CHUNK
    pos = lax.broadcasted_iota(jnp.int32, gc_r.shape, 1)
    tot = jnp.zeros_like(gc_r)
    for c in range(GD_SUPER // L):
        i = c * L if rev else c * L + L - 1
        tot = jnp.where((pos >= c * L) & (pos < (c + 1) * L), gc_r[:, i:i + 1], tot)
    return tot


def _gdn_prepare(blocks):
    t = lax.broadcasted_iota(jnp.int32, (GD_SUPER, GD_SUPER), 0)
    s = lax.broadcasted_iota(jnp.int32, (GD_SUPER, GD_SUPER), 1)
    x = t ^ s
    same = x < GD_CHUNK
    kk0 = [_mm_nt(k, k) for k, _, _, _, _ in blocks]
    qk0 = [None if qs is None else _mm_nt(qs, k) for k, _, qs, _, _ in blocks]
    kT = [k.T for k, _, _, _, _ in blocks]
    prob = []
    for i, (k, v, qs, gp, gpt) in enumerate(blocks):
        for rev in (False, True):
            j_b, j_g = (5, 7) if rev else (4, 6)
            beta = gp[:, j_b:j_b + 1]
            gc_c = gp[:, j_g:j_g + 1]
            gc_r = gpt[j_g:j_g + 1, :]
            incl = same & ((s >= t) if rev else (s <= t))
            strict = same & ((s > t) if rev else (s < t))
            decay = jnp.exp(jnp.where(incl, gc_c - gc_r, -jnp.inf))
            m = jnp.where(strict, (beta * kk0[i]) * decay, 0.0)
            prob.append(dict(i=i, rev=rev, beta=beta, gc_r=gc_r, decay=decay, m=m, eg=jnp.exp(gc_c)))
    tinv = [jnp.where(x == 0, 1.0, 0.0) - jnp.where(x == 1, p["m"], 0.0) for p in prob]
    for lvl in range(1, GD_CHUNK.bit_length() - 1):
        tc = [_mm(tinv[n], jnp.where((x >> lvl) == 1, p["m"], 0.0)) for n, p in enumerate(prob)]
        tinv = [tinv[n] - _mm(tc[n], tinv[n]) for n in range(len(prob))]
    out = [[] for _ in blocks]
    for n, p in enumerate(prob):
        k, v, qs, _, _ = blocks[p["i"]]
        y = _mm(tinv[n], jnp.concatenate([v * p["beta"], (k * p["beta"]) * p["eg"]], axis=1))
        kendT = (kT[p["i"]] * jnp.exp(_chunk_totals(p["gc_r"], p["rev"]) - p["gc_r"])).astype(BF16)
        qd = None if qs is None else (qs * p["eg"]).astype(BF16)
        qk = None if qs is None else (qk0[p["i"]] * p["decay"]).astype(BF16)
        out[p["i"]].append((y[:, :HEAD_DIM], y[:, HEAD_DIM:].astype(BF16), kendT, qd, qk))
    return out


def _gdn_kernel(*refs, ctx_out, n_ctx, n_lat):
    n_in = 16 if ctx_out else 14
    n_out = 2 if ctx_out else 1
    if ctx_out:
        (kc_ref, vc_ref, gpc_ref, gptc_ref, qc_ref, zc_ref, kl_ref, vl_ref, gpl_ref, gptl_ref, ql_ref, zl_ref,
         wq_ref, wk_ref, wv_ref, norm_ref, yc_ref, yl_ref) = refs[:n_in + n_out]
    else:
        (kc_ref, vc_ref, gpc_ref, gptc_ref, kl_ref, vl_ref, gpl_ref, gptl_ref, ql_ref, zl_ref,
         wq_ref, wk_ref, wv_ref, norm_ref, yl_ref) = refs[:n_in + n_out]
        qc_ref = zc_ref = yc_ref = None
    ks, vs, qs, gpa, gpta, o_f, o_b = refs[n_in + n_out:n_in + n_out + 7]
    per_dir = refs[n_in + n_out + 7:]
    u_s, w_s, kT_s, qd_s, qk_s = (per_dir[0:2], per_dir[2:4], per_dir[4:6], per_dir[6:8], per_dir[8:10])
    L = GD_CHUNK
    SB = GD_SUPER
    tc = n_ctx * SB
    tiles = SB // ML_CHUNK

    ks[0:tc, :] = _l2norm(_conv_silu(kc_ref[...], wk_ref[...]))
    vs[0:tc, :] = _conv_silu(vc_ref[...], wv_ref[...])
    ks[tc:, :] = _l2norm(_conv_silu(kl_ref[...], wk_ref[...]))
    vs[tc:, :] = _conv_silu(vl_ref[...], wv_ref[...])
    qs[tc:, :] = _l2norm(_conv_silu(ql_ref[...], wq_ref[...])) * (HEAD_DIM ** -0.5)
    if ctx_out:
        qs[0:tc, :] = _l2norm(_conv_silu(qc_ref[...], wq_ref[...])) * (HEAD_DIM ** -0.5)
    gpa[0:tc, :] = gpc_ref[...]
    gpa[tc:, :] = gpl_ref[...]
    gpta[0:n_ctx * tiles] = gptc_ref[...]
    gpta[n_ctx * tiles:] = gptl_ref[...]

    def prepare(js, with_out):
        rows = [pl.ds(pl.multiple_of(j * SB, SB), SB) for j in js]
        blocks = [(ks[r, :], vs[r, :], qs[r, :] if with_out else None, gpa[r, :],
                   jnp.concatenate([gpta[j * tiles + i] for i in range(tiles)], axis=1))
                  for j, r in zip(js, rows)]
        for j, r, res in zip(js, rows, _gdn_prepare(blocks)):
            for d, (u, w, kT, qd, qk) in enumerate(res):
                u_s[d][r, :] = u
                w_s[d][r, :] = w
                kT_s[d][j] = kT
                if with_out:
                    qd_s[d][r, :] = qd
                    qk_s[d][r, :] = qk

    def prepare_loop(lo, n, with_out):
        group = GD_PREP_GROUP if n % GD_PREP_GROUP == 0 else 1

        def body(i, carry):
            prepare([lo + i * group + g for g in range(group)], with_out)
            return carry
        lax.fori_loop(0, n // group, body, 0)

    def chunk_rows(j, c):
        return pl.ds(pl.multiple_of(j * SB + c * L, L), L)

    def step_load(d, j, c, with_out):
        rows = chunk_rows(j, c)
        cols = slice(c * L, (c + 1) * L)
        edge = gpa[pl.ds(pl.multiple_of(j * SB + c * L, L) + (0 if d else L - 1), 1), :]
        g_end = jnp.exp(edge[:, 7:8] if d else edge[:, 6:7])
        out_ops = (qd_s[d][rows, :], qk_s[d][rows, cols]) if with_out else None
        return g_end, u_s[d][rows, :], w_s[d][rows, :], kT_s[d][j, :, cols], out_ops

    def step_compute(ops, S):
        g_end, u, w, kT, out_ops = ops
        Sb = S.astype(BF16)
        v_new = (u - jnp.dot(w, Sb, preferred_element_type=F32)).astype(BF16)
        out = None
        if out_ops is not None:
            out = (jnp.dot(out_ops[0], Sb, preferred_element_type=F32)
                   + jnp.dot(out_ops[1], v_new, preferred_element_type=F32))
        return S * g_end + jnp.dot(kT, v_new, preferred_element_type=F32), out

    def chunk_pair(lo, n, i, c):
        return (lo + i, c), (lo + n - 1 - i, SB // L - 1 - c)

    def scan(lo, n, first, last, carry, with_out):
        def body(i, carry):
            for c in range(SB // L):
                where = chunk_pair(lo, n, i, c)
                res = [step_compute(step_load(d, *where[d], with_out), carry[d]) for d in range(2)]
                if with_out:
                    o_f[chunk_rows(*where[0]), :] = res[0][1]
                    o_b[chunk_rows(*where[1]), :] = res[1][1]
                carry = (res[0][0], res[1][0])
            return carry
        return lax.fori_loop(first, last, body, carry)

    zero = jnp.zeros((HEAD_DIM, HEAD_DIM), F32)
    prepare_loop(0, n_ctx, ctx_out)
    prepare_loop(n_ctx, n_lat, True)
    carry = scan(0, n_ctx, 0, n_ctx, (zero, zero), ctx_out)
    scan(n_ctx, n_lat, 0, n_lat, carry, True)

    def finish(rows, z_ref, y_ref):
        y_ref[...] = ((_rms(o_f[rows, :] + o_b[rows, :]) * norm_ref[...]) * _silu(z_ref[...])).astype(y_ref.dtype)

    finish(slice(tc, None), zl_ref, yl_ref)
    if ctx_out:
        finish(slice(0, tc), zc_ref, yc_ref)


def _gdn(zc, gpc, gptc, zl, gpl, gptl, conv, norm, ctx_out):
    nb, tc, _ = zc.shape
    tl = zl.shape[1]
    assert tc % GD_SUPER == 0 and tl % GD_SUPER == 0, "sequences are processed in whole 256-token superblocks"
    ctx_in = [zc, zc, gpc, gptc] + ([zc, zc] if ctx_out else [])
    ctx_specs = [_seq_spec(tc, C_GD_K), _seq_spec(tc, C_GD_V)] + _gp_specs(tc) + (
        [_seq_spec(tc, C_GD_Q), _seq_spec(tc, C_GD_Z)] if ctx_out else [])
    lat_in = [zl, zl, gpl, gptl, zl, zl]
    lat_specs = [_seq_spec(tl, C_GD_K), _seq_spec(tl, C_GD_V)] + _gp_specs(tl) + [
        _seq_spec(tl, C_GD_Q), _seq_spec(tl, C_GD_Z)]
    conv_spec = lambda off: pl.BlockSpec((3, HEAD_DIM), lambda b, h: (0, off // HEAD_DIM + h))
    out_spec = lambda rows: pl.BlockSpec((None, rows, HEAD_DIM), lambda b, h: (b, 0, h))
    out_shape = lambda rows: jax.ShapeDtypeStruct((nb, rows, BRANCH_W), BF16)
    rows = tc + tl
    seq = lambda dt: pltpu.VMEM((rows, HEAD_DIM), dt)
    two = lambda spec: [spec, spec]
    scratch = ([seq(F32), seq(F32), seq(F32), seq(F32), pltpu.VMEM((rows // ML_CHUNK, N_GATE, ML_CHUNK), F32),
                seq(F32), seq(F32)]
               + two(seq(F32)) + two(seq(BF16)) + two(pltpu.VMEM((rows // GD_SUPER, HEAD_DIM, GD_SUPER), BF16))
               + two(seq(BF16)) + two(pltpu.VMEM((rows, GD_SUPER), BF16)))
    res = pl.pallas_call(
        functools.partial(_gdn_kernel, ctx_out=ctx_out, n_ctx=tc // GD_SUPER, n_lat=tl // GD_SUPER),
        grid=(nb, N_HEADS),
        in_specs=ctx_specs + lat_specs + [conv_spec(0), conv_spec(BRANCH_W), conv_spec(2 * BRANCH_W),
                                          pl.BlockSpec((1, HEAD_DIM), lambda b, h: (0, 0))],
        out_specs=([out_spec(tc)] if ctx_out else []) + [out_spec(tl)],
        out_shape=([out_shape(tc)] if ctx_out else []) + [out_shape(tl)],
        scratch_shapes=scratch,
        compiler_params=_params(("parallel", "parallel")),
        name="gdn",
    )(*ctx_in, *lat_in, conv, conv, conv, norm)
    return (res[0], res[1]) if ctx_out else (None, res[0])


def _rope(x, cos, sin):
    lane = lax.broadcasted_iota(jnp.int32, x.shape, 1)
    quarter = HEAD_DIM // 4
    partner = jnp.where((lane & (2 * quarter - 1)) < quarter,
                        pltpu.roll(x, HEAD_DIM - quarter, 1), pltpu.roll(x, quarter, 1))
    return x * cos + partner * sin


def _attn_kernel(*refs, latent, tc):
    if latent:
        (q_ref, kc_ref, vc_ref, kl_ref, vl_ref, cos_ref, sin_ref, cosq_ref, sinq_ref, qn_ref, kn_ref,
         o_ref, k_scr, v_scr) = refs
    else:
        q_ref, kc_ref, vc_ref, qn_ref, kn_ref, o_ref, k_scr, v_scr = refs

    @pl.when(pl.program_id(2) == 0)
    def _():
        k_scr[0:tc, :] = (_rms(kc_ref[...]) * kn_ref[...]).astype(BF16)
        v_scr[0:tc, :] = vc_ref[...].astype(BF16)
        if latent:
            kl = _rope(_rms(kl_ref[...]) * kn_ref[...], cos_ref[...], sin_ref[...])
            k_scr[tc:, :] = kl.astype(BF16)
            v_scr[tc:, :] = vl_ref[...].astype(BF16)

    scale = HEAD_DIM ** -0.5
    for g in range(AT_Q_HEADS // AT_KV_HEADS):
        cols = slice(g * HEAD_DIM, (g + 1) * HEAD_DIM)
        q = _rms(q_ref[:, cols]) * qn_ref[...]
        if latent:
            q = _rope(q, cosq_ref[...], sinq_ref[...])
        s = lax.dot_general((q * scale).astype(BF16), k_scr[...], (((1,), (1,)), ((), ())),
                            preferred_element_type=F32)
        p = jnp.exp(s - jnp.max(s, axis=-1, keepdims=True))
        l = jnp.sum(p, axis=-1, keepdims=True)
        o_ref[:, cols] = (jnp.dot(p.astype(BF16), v_scr[...], preferred_element_type=F32) / l).astype(o_ref.dtype)


def _attention(zq, zc, zl, cos, sin, qn, kn):
    latent = zl is not None
    nb, tq_total, _ = zq.shape
    tc = zc.shape[1]
    tk = tc + (zl.shape[1] if latent else 0)
    gq = (AT_Q_HEADS // AT_KV_HEADS) * HEAD_DIM
    head_spec = lambda rows, col: pl.BlockSpec((None, rows, HEAD_DIM), lambda b, h, i: (b, 0, col // HEAD_DIM + h))
    vec_spec = pl.BlockSpec((1, HEAD_DIM), lambda b, h, i: (0, 0))
    in_specs = [pl.BlockSpec((None, TQ_ATTN, gq), lambda b, h, i: (b, i, C_AT_Q // gq + h)),
                head_spec(tc, C_AT_K), head_spec(tc, C_AT_V)]
    args = [zq, zc, zc]
    if latent:
        tl = zl.shape[1]
        in_specs += [head_spec(tl, C_AT_K), head_spec(tl, C_AT_V),
                     pl.BlockSpec((tl, HEAD_DIM), lambda b, h, i: (0, 0)),
                     pl.BlockSpec((tl, HEAD_DIM), lambda b, h, i: (0, 0)),
                     pl.BlockSpec((TQ_ATTN, HEAD_DIM), lambda b, h, i: (i, 0)),
                     pl.BlockSpec((TQ_ATTN, HEAD_DIM), lambda b, h, i: (i, 0))]
        args += [zl, zl, cos, sin, cos, sin]
    return pl.pallas_call(
        functools.partial(_attn_kernel, latent=latent, tc=tc),
        grid=(nb, AT_KV_HEADS, tq_total // TQ_ATTN),
        in_specs=in_specs + [vec_spec, vec_spec],
        out_specs=pl.BlockSpec((None, TQ_ATTN, gq), lambda b, h, i: (b, i, h)),
        out_shape=jax.ShapeDtypeStruct((nb, tq_total, BRANCH_W), BF16),
        scratch_shapes=[pltpu.VMEM((tk, HEAD_DIM), BF16), pltpu.VMEM((tk, HEAD_DIM), BF16)],
        compiler_params=_params(("parallel", "parallel", "arbitrary")),
        name="attention",
    )(*args, qn, kn)


def _merge_kernel(h_ref, xcol_ref, modcol_ref, y0_ref, y1_ref, y2_ref, y3_ref,
                  wg0_ref, wg1_ref, wg2_ref, wg3_ref, wb_ref, wo_ref, o_ref, m_scr):
    s = pl.program_id(2)
    n_gate = D_MODEL // TN_MERGE
    ys = (y0_ref, y1_ref, y2_ref, y3_ref)
    wgs = (wg0_ref, wg1_ref, wg2_ref, wg3_ref)

    @pl.when(s < n_gate)
    def _():
        h = h_ref[...]
        acc = jnp.zeros((TM_MERGE, TN_MERGE), F32)
        for n in range(N_BRANCH):
            gate = jnp.dot(h, wgs[n][...], preferred_element_type=F32)
            acc = acc + _sigmoid(gate) * jnp.dot(ys[n][...], wb_ref[n], preferred_element_type=F32)
        m_scr[s] = acc.astype(BF16)

    @pl.when(s >= n_gate)
    def _():
        acc = jnp.zeros((TM_MERGE, TN_OUT), F32)
        for kk in range(n_gate):
            acc = acc + jnp.dot(m_scr[kk], wo_ref[kk * TN_MERGE:(kk + 1) * TN_MERGE, :],
                                preferred_element_type=F32)
        o_ref[...] = xcol_ref[...] + modcol_ref[5:6, :] * acc


def _merge(x, h, mod, ys, w_gate, w_branch, w_out, layer, mod_row):
    groups, rows, _ = x.shape
    n_gate = D_MODEL // TN_MERGE
    n_out = D_MODEL // TN_OUT
    mrow = (lambda b: b) if mod_row is None else (lambda b: mod_row)
    gate_col = lambda s: jnp.minimum(s, n_gate - 1)
    out_col = lambda s: jnp.maximum(s - n_gate, 0)
    y_spec = pl.BlockSpec((None, TM_MERGE, BRANCH_W), lambda b, i, s: (b, i, 0))
    wg_spec = lambda n: pl.BlockSpec((None, D_MODEL, TN_MERGE),
                                     lambda b, i, s: (layer, 0, n * n_gate + gate_col(s)))
    return pl.pallas_call(
        _merge_kernel,
        grid=(groups, rows // TM_MERGE, n_gate + n_out),
        in_specs=[pl.BlockSpec((None, TM_MERGE, D_MODEL), lambda b, i, s: (b, i, 0)),
                  pl.BlockSpec((None, TM_MERGE, TN_OUT), lambda b, i, s: (b, i, out_col(s))),
                  pl.BlockSpec((None, N_MOD, TN_OUT), lambda b, i, s: (mrow(b), 0, out_col(s))),
                  y_spec, y_spec, y_spec, y_spec,
                  wg_spec(0), wg_spec(1), wg_spec(2), wg_spec(3),
                  pl.BlockSpec((None, N_BRANCH, BRANCH_W, TN_MERGE), lambda b, i, s: (layer, 0, 0, gate_col(s))),
                  pl.BlockSpec((None, D_MODEL, TN_OUT), lambda b, i, s: (layer, 0, out_col(s)))],
        out_specs=pl.BlockSpec((None, TM_MERGE, TN_OUT), lambda b, i, s: (b, i, out_col(s))),
        out_shape=jax.ShapeDtypeStruct(x.shape, F32),
        scratch_shapes=[pltpu.VMEM((n_gate, TM_MERGE, TN_MERGE), BF16)],
        compiler_params=_params(("parallel", "parallel", "arbitrary")),
        name="merge",
    )(h, x, mod, *ys, w_gate, w_gate, w_gate, w_gate, w_branch, w_out)


def _rope_tables(seq):
    n = HEAD_DIM // 4
    inv = ROPE_THETA ** (-jnp.arange(n, dtype=F32) / n)
    pos = jnp.arange(seq)
    ar = (pos // GRID_W).astype(F32)[:, None] * inv
    ac = (pos % GRID_W).astype(F32)[:, None] * inv
    cos = jnp.concatenate([jnp.cos(ar), jnp.cos(ar), jnp.cos(ac), jnp.cos(ac)], axis=1)
    sin = jnp.concatenate([-jnp.sin(ar), jnp.sin(ar), -jnp.sin(ac), jnp.sin(ac)], axis=1)
    return cos, sin


def _gate_params(ml_if_bias, gd_a_log, gd_dt_bias):
    zeros = jnp.zeros((2, N_HEADS), F32)
    bias = jnp.concatenate([ml_if_bias.reshape(4, N_HEADS), zeros, gd_dt_bias], axis=0).T
    alog = jnp.concatenate([jnp.zeros((6, N_HEADS), F32), gd_a_log], axis=0).T
    pad = lambda a: jnp.pad(a.reshape(1, N_HEADS * N_GATE), ((0, 0), (0, LANES - N_HEADS * N_GATE)))
    col = lambda a: a.reshape(N_HEADS * N_GATE, 1)
    return pad(bias), pad(alog), col(bias), col(alog)


def _gate_weights(w_in):
    depth = w_in.shape[0]
    g = jnp.concatenate([w_in[..., R_ML_IF:R_ML_IF + 16], w_in[..., R_GD_BA:R_GD_BA + 16]], axis=-1)
    g = g.reshape(depth, D_MODEL, N_GATE, N_HEADS).transpose(0, 1, 3, 2).reshape(depth, D_MODEL, N_HEADS * N_GATE)
    w_gc = jnp.pad(g, ((0, 0), (0, 0), (0, LANES - N_HEADS * N_GATE)))
    w_gr = g.transpose(0, 2, 1)
    return w_gc.astype(BF16), w_gr.astype(BF16)


def kernel(x, c, ctx, c_ctx, mod_w, mod_b, ffn1_norm, ffn1_w_in, ffn1_w_out, mix_norm, w_in, sg_norm, sg_w, sg_b, ml_if_bias, ml_norm, gd_conv, gd_a_log, gd_dt_bias, gd_norm, at_q_norm, at_k_norm, w_branch, w_out, ffn2_norm, ffn2_w_in, ffn2_w_out):
    nb, seq, _ = x.shape
    tc = ctx.shape[1]
    depth = mod_w.shape[0]
    ctx_row = nb
    cos, sin = _rope_tables(seq)

    cc = jnp.zeros((MOD_ROWS, D_MODEL), F32).at[:nb].set(c).at[ctx_row].set(c_ctx)
    mods = _modulation(cc, mod_w, mod_b).reshape(depth, MOD_ROWS, N_MOD, D_MODEL)

    f1_in, f1_out, f2_in, f2_out = [_to_bf16(w) for w in (ffn1_w_in, ffn1_w_out, ffn2_w_in, ffn2_w_out)]
    w_main, w_gate = _split_w_in(w_in)
    w_gc_all, w_gr_all = _gate_weights(w_in)
    wb, wo = w_branch.astype(BF16), w_out.astype(BF16)
    flat = lambda a: a.reshape(1, nb * tc, a.shape[-1])
    unflat = lambda a: a.reshape(nb, tc, a.shape[-1])
    xc = ctx
    for l in range(depth):
        last = l == depth - 1
        ctx_out = not last
        mod = mods[l]
        w_gc, w_gr = w_gc_all[l], w_gr_all[l]
        bias_c, alog_c, bias_r, alog_r = _gate_params(ml_if_bias[l], gd_a_log[l], gd_dt_bias[l])
        sgw = sg_w[l].astype(BF16)
        sg_bias = jnp.repeat(sg_b[l].T, BRANCH_W // SG_GROUPS, axis=1)
        sgn = sg_norm[l].reshape(1, BRANCH_W)
        mln = ml_norm[l].reshape(1, BRANCH_W)
        gdn_g = gd_norm[l].reshape(1, HEAD_DIM)
        qn, kn = at_q_norm[l].reshape(1, HEAD_DIM), at_k_norm[l].reshape(1, HEAD_DIM)

        x = _ffn(x, mod, ffn1_norm[l], f1_in, f1_out, l, 0, None)
        xc = unflat(_ffn(flat(xc), mod, ffn1_norm[l], f1_in, f1_out, l, 0, ctx_row))

        zl, gcl, grl, hl = _inproj(x, mod, mix_norm[l], w_main, l, w_gc, w_gr, N_MAIN, None)
        zc, gcc, grc, hc = _inproj(flat(xc), mod, mix_norm[l], w_main, l, w_gc, w_gr,
                                   N_MAIN if ctx_out else KV_MAIN, ctx_row)
        zc = unflat(zc)
        gcc = unflat(gcc)
        grc = grc.reshape(N_HEADS * N_GATE, nb, tc).transpose(1, 0, 2)
        gpl, gptl = _gateprep(gcl, grl, bias_c, alog_c, bias_r, alog_r)
        gpc, gptc = _gateprep(gcc, grc, bias_c, alog_c, bias_r, alog_r)

        y_sg_l = _spatial_gating(zl, sgn, sgw, sg_bias)
        y_ml_c, y_ml_l = _mlstm(zc, gpc, gptc, zl, gpl, gptl, mln, ctx_out)
        y_gd_c, y_gd_l = _gdn(zc, gpc, gptc, zl, gpl, gptl, gd_conv[l], gdn_g, ctx_out)
        y_at_l = _attention(zl, zc, zl, cos, sin, qn, kn)
        x = _merge(x, hl, mod, [y_sg_l, y_ml_l, y_gd_l, y_at_l], w_gate, wb, wo, l, None)
        if ctx_out:
            y_sg_c = _spatial_gating(zc, sgn, sgw, sg_bias)
            y_at_c = _attention(zc, zc, None, None, None, qn, kn)
            xc = unflat(_merge(flat(xc), hc, mod, [flat(y) for y in (y_sg_c, y_ml_c, y_gd_c, y_at_c)],
                               w_gate, wb, wo, l, ctx_row))

        x = _ffn(x, mod, ffn2_norm[l], f2_in, f2_out, l, 6, None)
        if ctx_out:
            xc = unflat(_ffn(flat(xc), mod, ffn2_norm[l], f2_in, f2_out, l, 6, ctx_row))
    return x
```

```python
import functools
import math

import jax
import jax.numpy as jnp
from jax import lax
from jax.experimental import pallas as pl
from jax.experimental.pallas import tpu as pltpu

F32 = jnp.float32
BF16 = jnp.bfloat16
EPS = 1e-6

D_MODEL = 2048
D_FF = 5632
N_MOD = 9
N_BRANCH = 4
BRANCH_W = 512
HEAD_DIM = 128
N_HEADS = 4
SG_GROUPS = 4
ML_CHUNK = 128
GD_CHUNK = 128
GD_SUPER = 256
GD_PREP_GROUP = 2
AT_Q_HEADS = 4
AT_KV_HEADS = 2
GRID_W = 64
ROPE_THETA = 10000.0
FFN_RESIDUAL = 0.5

LANES = 128
N_GATE = 8
MOD_ROWS = 16

C_ML_K, C_ML_V = 0, 512
C_GD_K, C_GD_V = 1024, 1536
C_AT_K, C_AT_V = 2048, 2304
KV_MAIN = 2560
C_SG_U, C_SG_V = 2560, 3072
C_ML_Q, C_ML_O = 3584, 4096
C_GD_Q, C_GD_Z = 4608, 5120
C_AT_Q = 5632
N_MAIN = 6144
R_ML_IF = 1024
R_GD_KV = 1040
R_GD_BA = 2064
R_AT_KV = 2080
R_REST = 2592
R_GATE = 6176

VMEM_LIMIT = 56 * 1024 * 1024

TM_FFN = 512
TF_FFN = 512
TM_PROJ = 1024
TN_PROJ = 512
TM_MERGE = 1024
TN_MERGE = 256
TN_OUT = 512
TQ_ATTN = 256
TR_SG = 256
TG_GATE = 256
CAST_BLOCK_BYTES = 8 * 1024 * 1024
SPLIT_ROWS = 128
TK_MOD = 128


def _params(sem):
    return pltpu.CompilerParams(dimension_semantics=sem, vmem_limit_bytes=VMEM_LIMIT)


def _mm(a, b):
    return jnp.dot(a.astype(BF16), b.astype(BF16), preferred_element_type=F32)


def _mm_nt(a, b):
    return lax.dot_general(a.astype(BF16), b.astype(BF16), (((1,), (1,)), ((), ())),
                           preferred_element_type=F32)


def _mm_tn(a, b):
    return lax.dot_general(a.astype(BF16), b.astype(BF16), (((0,), (0,)), ((), ())),
                           preferred_element_type=F32)


def _sigmoid(x):
    return 1.0 / (1.0 + jnp.exp(-x))


def _silu(x):
    return x * _sigmoid(x)


def _softplus(x):
    return jnp.maximum(x, 0.0) + jnp.log1p(jnp.exp(-jnp.abs(x)))


def _gelu_tanh(x):
    c = math.sqrt(2.0 / math.pi)
    return x * (0.5 * (1.0 + jnp.tanh(c * (x + 0.044715 * (x * x * x)))))


def _rms(x):
    return x * lax.rsqrt(jnp.mean(x * x, axis=-1, keepdims=True) + EPS)


def _norm_mod(x, gain, shift, scale):
    return (_rms(x) * gain) * (1.0 + scale) + shift


def _cast_kernel(x_ref, o_ref):
    o_ref[...] = x_ref[...].astype(o_ref.dtype)


def _to_bf16(w):
    depth, r, c = w.shape
    rows = 1 << (max(16, min(r, CAST_BLOCK_BYTES // (4 * c))).bit_length() - 1)
    while r % rows:
        rows //= 2
    return pl.pallas_call(
        _cast_kernel,
        grid=(depth, r // rows),
        in_specs=[pl.BlockSpec((None, rows, c), lambda l, i: (l, i, 0))],
        out_specs=pl.BlockSpec((None, rows, c), lambda l, i: (l, i, 0)),
        out_shape=jax.ShapeDtypeStruct(w.shape, BF16),
        compiler_params=_params(("parallel", "parallel")),
        name="cast_bf16",
    )(w)


def _split_w_in_kernel(w_ref, main_ref, gate_ref):
    w = w_ref[...]
    main_ref[...] = jnp.concatenate([w[:, :R_ML_IF], w[:, R_GD_KV:R_GD_BA], w[:, R_AT_KV:R_GATE]],
                                    axis=1).astype(BF16)
    gate_ref[...] = w[:, R_GATE:].astype(BF16)


def _split_w_in(w_in):
    depth, d, n = w_in.shape
    n_gate = n - R_GATE
    return pl.pallas_call(
        _split_w_in_kernel,
        grid=(depth, d // SPLIT_ROWS),
        in_specs=[pl.BlockSpec((None, SPLIT_ROWS, n), lambda l, i: (l, i, 0))],
        out_specs=[pl.BlockSpec((None, SPLIT_ROWS, N_MAIN), lambda l, i: (l, i, 0)),
                   pl.BlockSpec((None, SPLIT_ROWS, n_gate), lambda l, i: (l, i, 0))],
        out_shape=[jax.ShapeDtypeStruct((depth, d, N_MAIN), BF16),
                   jax.ShapeDtypeStruct((depth, d, n_gate), BF16)],
        compiler_params=_params(("parallel", "parallel")),
        name="split_w_in",
    )(w_in)


def _mod_kernel(cc_ref, w_ref, b_ref, o_ref):
    @pl.when(pl.program_id(1) == 0)
    def _():
        o_ref[...] = jnp.broadcast_to(b_ref[...], o_ref.shape)

    o_ref[...] += _mm(_silu(cc_ref[...]), w_ref[...])


def _modulation(cc, mod_w, mod_b):
    depth, d, n = mod_w.shape
    ccs = cc.reshape(MOD_ROWS, d // TK_MOD, TK_MOD).transpose(1, 0, 2)
    return pl.pallas_call(
        _mod_kernel,
        grid=(depth, d // TK_MOD),
        in_specs=[pl.BlockSpec((None, MOD_ROWS, TK_MOD), lambda l, k: (k, 0, 0)),
                  pl.BlockSpec((None, TK_MOD, n), lambda l, k: (l, k, 0)),
                  pl.BlockSpec((None, 1, n), lambda l, k: (l, 0, 0))],
        out_specs=pl.BlockSpec((None, MOD_ROWS, n), lambda l, k: (l, 0, 0)),
        out_shape=jax.ShapeDtypeStruct((depth, MOD_ROWS, n), F32),
        compiler_params=_params(("parallel", "arbitrary")),
        name="modulation",
    )(ccs, mod_w, mod_b.reshape(depth, 1, n))


def _ffn_kernel(x_ref, mod_ref, xn_ref, modn_ref, g_ref, wa_ref, wb_ref, wo_ref, o_ref, h_scr, acc_scr, *, base):
    j = pl.program_id(2)
    last = pl.num_programs(2) - 1
    tile = pl.program_id(0) * pl.num_programs(1) + pl.program_id(1)
    slot = tile % 2

    def prepare(x, mod, into):
        h = _norm_mod(x[...], g_ref[...], mod[base:base + 1, :], mod[base + 1:base + 2, :])
        h_scr[into] = h.astype(BF16)

    @pl.when((tile == 0) & (j == 0))
    def _():
        prepare(x_ref, mod_ref, 0)

    def hidden_slice():
        h = h_scr[slot]
        a = jnp.dot(h, wa_ref[...], preferred_element_type=F32)
        b = jnp.dot(h, wb_ref[...], preferred_element_type=F32)
        return _mm(_silu(a) * b, wo_ref[...])

    @pl.when(j == 0)
    def _():
        acc_scr[...] = hidden_slice()

    @pl.when((j > 0) & (j < last))
    def _():
        acc_scr[...] += hidden_slice()

    @pl.when(j == last)
    def _():
        total = acc_scr[...] + hidden_slice()
        prepare(xn_ref, modn_ref, 1 - slot)
        o_ref[...] = x_ref[...] + (FFN_RESIDUAL * mod_ref[base + 2:base + 3, :]) * total


def _ffn(x, mod, gain, w_in, w_out, layer, base, mod_row):
    groups, rows, _ = x.shape
    nj = D_FF // TF_FFN
    ni = rows // TM_FFN
    mrow = (lambda b: b) if mod_row is None else (lambda b: mod_row)

    def next_tile(b, i):
        t = jnp.minimum(b * ni + i + 1, groups * ni - 1)
        return t // ni, t % ni

    return pl.pallas_call(
        functools.partial(_ffn_kernel, base=base),
        grid=(groups, ni, nj),
        in_specs=[pl.BlockSpec((None, TM_FFN, D_MODEL), lambda b, i, j: (b, i, 0)),
                  pl.BlockSpec((None, N_MOD, D_MODEL), lambda b, i, j: (mrow(b), 0, 0)),
                  pl.BlockSpec((None, TM_FFN, D_MODEL), lambda b, i, j: (*next_tile(b, i), 0)),
                  pl.BlockSpec((None, N_MOD, D_MODEL), lambda b, i, j: (mrow(next_tile(b, i)[0]), 0, 0)),
                  pl.BlockSpec((1, D_MODEL), lambda b, i, j: (0, 0)),
                  pl.BlockSpec((None, D_MODEL, TF_FFN), lambda b, i, j: (layer, 0, j)),
                  pl.BlockSpec((None, D_MODEL, TF_FFN), lambda b, i, j: (layer, 0, j + nj)),
                  pl.BlockSpec((None, TF_FFN, D_MODEL), lambda b, i, j: (layer, j, 0))],
        out_specs=pl.BlockSpec((None, TM_FFN, D_MODEL), lambda b, i, j: (b, i, 0)),
        out_shape=jax.ShapeDtypeStruct(x.shape, F32),
        scratch_shapes=[pltpu.VMEM((2, TM_FFN, D_MODEL), BF16), pltpu.VMEM((TM_FFN, D_MODEL), F32)],
        compiler_params=_params(("arbitrary", "arbitrary", "arbitrary")),
        name="ffn",
    )(x, mod, x, mod, gain.reshape(1, D_MODEL), w_in, w_in, w_out)


def _inproj_kernel(x_ref, mod_ref, g_ref, w_ref, wgc_ref, wgr_ref, z_ref, gc_ref, gr_ref, h_ref):
    n = pl.program_id(2)

    @pl.when(n == 0)
    def _():
        h = _norm_mod(x_ref[...], g_ref[...], mod_ref[3:4, :], mod_ref[4:5, :]).astype(BF16)
        h_ref[...] = h
        gc_ref[...] = jnp.dot(h, wgc_ref[...], preferred_element_type=F32)
        gr_ref[...] = lax.dot_general(wgr_ref[...], h, (((1,), (1,)), ((), ())), preferred_element_type=F32)

    z_ref[...] = jnp.dot(h_ref[...], w_ref[...], preferred_element_type=F32)


def _inproj(x, mod, gain, w_main, layer, w_gc, w_gr, n_cols, mod_row):
    groups, rows, _ = x.shape
    mod_idx = (lambda b, i, n: (b, 0, 0)) if mod_row is None else (lambda b, i, n: (mod_row, 0, 0))
    return pl.pallas_call(
        _inproj_kernel,
        grid=(groups, rows // TM_PROJ, n_cols // TN_PROJ),
        in_specs=[pl.BlockSpec((None, TM_PROJ, D_MODEL), lambda b, i, n: (b, i, 0)),
                  pl.BlockSpec((None, N_MOD, D_MODEL), mod_idx),
                  pl.BlockSpec((1, D_MODEL), lambda b, i, n: (0, 0)),
                  pl.BlockSpec((None, D_MODEL, TN_PROJ), lambda b, i, n: (layer, 0, n)),
                  pl.BlockSpec((D_MODEL, LANES), lambda b, i, n: (0, 0)),
                  pl.BlockSpec((N_HEADS * N_GATE, D_MODEL), lambda b, i, n: (0, 0))],
        out_specs=[pl.BlockSpec((None, TM_PROJ, TN_PROJ), lambda b, i, n: (b, i, n)),
                   pl.BlockSpec((None, TM_PROJ, LANES), lambda b, i, n: (b, i, 0)),
                   pl.BlockSpec((None, N_HEADS * N_GATE, TM_PROJ), lambda b, i, n: (b, 0, i)),
                   pl.BlockSpec((None, TM_PROJ, D_MODEL), lambda b, i, n: (b, i, 0))],
        out_shape=[jax.ShapeDtypeStruct((groups, rows, n_cols), F32),
                   jax.ShapeDtypeStruct((groups, rows, LANES), F32),
                   jax.ShapeDtypeStruct((groups, N_HEADS * N_GATE, rows), F32),
                   jax.ShapeDtypeStruct((groups, rows, D_MODEL), BF16)],
        compiler_params=_params(("parallel", "parallel", "arbitrary")),
        name="inproj",
    )(x, mod, gain.reshape(1, D_MODEL), w_main, w_gc, w_gr)


def _split3(e):
    e1 = e.astype(BF16)
    r1 = e - e1.astype(F32)
    e2 = r1.astype(BF16)
    e3 = (r1 - e2.astype(F32)).astype(BF16)
    return e1, e2, e3


def _gate_values(pre, a_log, cls):
    neg_sp = -_softplus(-pre)
    return jnp.where(cls < 2, pre,
                     jnp.where(cls < 4, neg_sp,
                               jnp.where(cls < 6, _sigmoid(pre), -jnp.exp(a_log) * _softplus(pre))))


def _gateprep_kernel(gc_ref, gr_ref, bc_ref, ac_ref, br_ref, ar_ref, oc_ref, or_ref):
    for i in range(TG_GATE // ML_CHUNK):
        tile = slice(i * ML_CHUNK, (i + 1) * ML_CHUNK)
        out_c, or_ref[i] = _gateprep_tile(gc_ref[tile, :], gr_ref[:, tile], bc_ref[...], ac_ref[...],
                                          br_ref[...], ar_ref[...])
        for h in range(N_HEADS):
            oc_ref[tile, h * LANES:(h + 1) * LANES] = (
                out_c if h == 0 else pltpu.roll(out_c, LANES - h * N_GATE, 1))


def _gateprep_tile(gc, gr, bias_c, alog_c, bias_r, alog_r):
    L = ML_CHUNK
    half = GD_CHUNK
    pre = gc + bias_c
    cls = lax.broadcasted_iota(jnp.int32, pre.shape, 1) & (N_GATE - 1)
    t = lax.broadcasted_iota(jnp.int32, pre.shape, 0)
    e = _gate_values(pre, alog_c, cls)
    tri = (lax.broadcasted_iota(jnp.int32, (L, L), 1) <= lax.broadcasted_iota(jnp.int32, (L, L), 0)).astype(BF16)
    e1, e2, e3 = _split3(e)
    p = (jnp.dot(tri, e1, preferred_element_type=F32) + jnp.dot(tri, e2, preferred_element_type=F32)
         + jnp.dot(tri, e3, preferred_element_type=F32))
    tot = p[L - 1:L, :]
    mid = p[half - 1:half, :]
    second = t >= half
    pre64 = p - jnp.where(second, mid, 0.0)
    suf64 = jnp.where(second, tot, mid) - p + e
    cum = jnp.where(cls == 2, p, jnp.where(cls == 3, tot - p + e, jnp.where(cls == 6, pre64, suf64)))
    out_c = jnp.where((cls == 2) | (cls == 3) | (cls == 6) | (cls == 7), cum, e)

    pre_r = gr + bias_r
    cls_r = lax.broadcasted_iota(jnp.int32, pre_r.shape, 0) & (N_GATE - 1)
    t_r = lax.broadcasted_iota(jnp.int32, pre_r.shape, 1)
    e_r = _gate_values(pre_r, alog_r, cls_r)
    tri_r = (lax.broadcasted_iota(jnp.int32, (L, L), 0) <= lax.broadcasted_iota(jnp.int32, (L, L), 1)).astype(BF16)
    f1, f2, f3 = _split3(e_r)
    pr = (jnp.dot(f1, tri_r, preferred_element_type=F32) + jnp.dot(f2, tri_r, preferred_element_type=F32)
          + jnp.dot(f3, tri_r, preferred_element_type=F32))
    tot_r = pr[:, L - 1:L]
    mid_r = pr[:, half - 1:half]
    second_r = t_r >= half
    pre64_r = pr - jnp.where(second_r, mid_r, 0.0)
    suf64_r = jnp.where(second_r, tot_r, mid_r) - pr + e_r
    cum_r = jnp.where(cls_r == 2, pr, jnp.where(cls_r == 3, tot_r - pr + e_r, jnp.where(cls_r == 6, pre64_r, suf64_r)))
    return out_c, jnp.where((cls_r == 2) | (cls_r == 3) | (cls_r == 6) | (cls_r == 7), cum_r, e_r)


def _gateprep(gc, gr, bias_c, alog_c, bias_r, alog_r):
    groups, rows, _ = gc.shape
    nt = rows // ML_CHUNK
    hw = N_HEADS * LANES
    hg = N_HEADS * N_GATE
    tiles = TG_GATE // ML_CHUNK
    return pl.pallas_call(
        _gateprep_kernel,
        grid=(groups, rows // TG_GATE),
        in_specs=[pl.BlockSpec((None, TG_GATE, LANES), lambda b, c: (b, c, 0)),
                  pl.BlockSpec((None, hg, TG_GATE), lambda b, c: (b, 0, c)),
                  pl.BlockSpec((1, LANES), lambda b, c: (0, 0)),
                  pl.BlockSpec((1, LANES), lambda b, c: (0, 0)),
                  pl.BlockSpec((hg, 1), lambda b, c: (0, 0)),
                  pl.BlockSpec((hg, 1), lambda b, c: (0, 0))],
        out_specs=[pl.BlockSpec((None, TG_GATE, hw), lambda b, c: (b, c, 0)),
                   pl.BlockSpec((None, tiles, hg, ML_CHUNK), lambda b, c: (b, c, 0, 0))],
        out_shape=[jax.ShapeDtypeStruct((groups, rows, hw), F32),
                   jax.ShapeDtypeStruct((groups, nt, hg, ML_CHUNK), F32)],
        compiler_params=_params(("parallel", "parallel")),
        name="gateprep",
    )(gc, gr, bias_c, alog_c, bias_r, alog_r)


def _sg_kernel(u_ref, v_ref, gn_ref, w_ref, bias_ref, o_ref):
    gw = BRANCH_W // SG_GROUPS
    for ch in range(TR_SG // ML_CHUNK):
        rows = slice(ch * ML_CHUNK, (ch + 1) * ML_CHUNK)
        v = _rms(_gelu_tanh(v_ref[rows, :])) * gn_ref[...]
        u = _gelu_tanh(u_ref[rows, :])
        for g in range(SG_GROUPS):
            cols = slice(g * gw, (g + 1) * gw)
            s = _mm(w_ref[g], v[:, cols]) + bias_ref[:, cols]
            o_ref[rows, cols] = (u[:, cols] * s).astype(o_ref.dtype)


def _spatial_gating(z, gn, w_s, bias_full):
    groups, rows, _ = z.shape
    ub, vb = C_SG_U // BRANCH_W, C_SG_V // BRANCH_W
    return pl.pallas_call(
        _sg_kernel,
        grid=(groups, rows // TR_SG),
        in_specs=[pl.BlockSpec((None, TR_SG, BRANCH_W), lambda b, i: (b, i, ub)),
                  pl.BlockSpec((None, TR_SG, BRANCH_W), lambda b, i: (b, i, vb)),
                  pl.BlockSpec((1, BRANCH_W), lambda b, i: (0, 0)),
                  pl.BlockSpec((SG_GROUPS, ML_CHUNK, ML_CHUNK), lambda b, i: (0, 0, 0)),
                  pl.BlockSpec((ML_CHUNK, BRANCH_W), lambda b, i: (0, 0))],
        out_specs=pl.BlockSpec((None, TR_SG, BRANCH_W), lambda b, i: (b, i, 0)),
        out_shape=jax.ShapeDtypeStruct((groups, rows, BRANCH_W), BF16),
        compiler_params=_params(("parallel", "parallel")),
        name="spatial_gating",
    )(z, z, gn, w_s, bias_full)


def _mlstm_chunk(q, k, v, gp, gpt, state, rev, with_out):
    L = ML_CHUNK
    j_i, j_b = (1, 3) if rev else (0, 2)
    li_c = gp[:, j_i:j_i + 1]
    b_c = gp[:, j_b:j_b + 1]
    b_end = b_c[0:1, :] if rev else b_c[L - 1:L, :]
    ks = k * (HEAD_DIM ** -0.5)
    a = b_end - b_c + li_c
    m_loc = jnp.max(a, axis=0, keepdims=True)
    wgt = jnp.broadcast_to(jnp.exp(a - m_loc), (L, HEAD_DIM))
    ct_loc = _mm_tn(ks, jnp.concatenate([v * wgt, wgt], axis=1))
    ct, m = state
    m_new = jnp.maximum(b_end + m, m_loc)
    sp = jnp.exp(b_end + m - m_new)
    sl = jnp.exp(m_loc - m_new)
    new_state = (sp * ct + sl * ct_loc, m_new)
    if not with_out:
        return new_state, None
    li_r = gpt[j_i:j_i + 1, :]
    b_r = gpt[j_b:j_b + 1, :]
    t = lax.broadcasted_iota(jnp.int32, (L, L), 0)
    s = lax.broadcasted_iota(jnp.int32, (L, L), 1)
    incl = (s >= t) if rev else (s <= t)
    e = jnp.where(incl, li_r - b_r, -jnp.inf)
    c = jnp.broadcast_to(jnp.maximum(jnp.max(e, axis=1, keepdims=True), m), (L, L))
    p = jnp.exp(e - c) * _mm_nt(q, ks)
    inter = jnp.exp(m - c)
    pv = _mm(p, jnp.concatenate([v, jnp.ones_like(v)], axis=1))
    qc = _mm(q, ct)
    num = pv[:, :HEAD_DIM] + inter * qc[:, :HEAD_DIM]
    den = pv[:, HEAD_DIM:] + inter * qc[:, HEAD_DIM:]
    m_t = jnp.broadcast_to(b_c, (L, L)) + c
    return new_state, num / jnp.maximum(jnp.abs(den), jnp.exp(-m_t))


def _mlstm_kernel(*refs, ctx_out, n_ctx, n_lat):
    if ctx_out:
        (kc_ref, vc_ref, gpc_ref, gptc_ref, qc_ref, oc_ref, kl_ref, vl_ref, gpl_ref, gptl_ref, ql_ref, ol_ref,
         norm_ref, yc_ref, yl_ref, hfc, hbc, hfl, hbl) = refs
    else:
        (kc_ref, vc_ref, gpc_ref, gptc_ref, kl_ref, vl_ref, gpl_ref, gptl_ref, ql_ref, ol_ref,
         norm_ref, yl_ref, hfl, hbl) = refs
        qc_ref = oc_ref = yc_ref = hfc = hbc = None
    L = ML_CHUNK

    def run(k_ref, v_ref, gp_ref, gpt_ref, q_ref, hf, hb, n_chunks, st_f, st_b, with_out):
        def body(j, carry):
            st_f, st_b = carry
            rf = pl.ds(pl.multiple_of(j * L, L), L)
            jb = n_chunks - 1 - j
            rb = pl.ds(pl.multiple_of(jb * L, L), L)
            st_f, h_f = _mlstm_chunk(q_ref[rf, :] if with_out else None, k_ref[rf, :], v_ref[rf, :],
                                     gp_ref[rf, :], gpt_ref[j], st_f, False, with_out)
            st_b, h_b = _mlstm_chunk(q_ref[rb, :] if with_out else None, k_ref[rb, :], v_ref[rb, :],
                                     gp_ref[rb, :], gpt_ref[jb], st_b, True, with_out)
            if with_out:
                hf[rf, :] = h_f
                hb[rb, :] = h_b
            return st_f, st_b
        return lax.fori_loop(0, n_chunks, body, (st_f, st_b))

    zero = (jnp.zeros((HEAD_DIM, 2 * HEAD_DIM), F32), jnp.zeros((1, 1), F32))
    st_f, st_b = run(kc_ref, vc_ref, gpc_ref, gptc_ref, qc_ref, hfc, hbc, n_ctx, zero, zero, ctx_out)
    run(kl_ref, vl_ref, gpl_ref, gptl_ref, ql_ref, hfl, hbl, n_lat, st_f, st_b, True)

    def finish(hf, hb, o_ref, y_ref):
        y_ref[...] = (_sigmoid(o_ref[...]) * (_rms(hf[...] + hb[...]) * norm_ref[...])).astype(y_ref.dtype)

    finish(hfl, hbl, ol_ref, yl_ref)
    if ctx_out:
        finish(hfc, hbc, oc_ref, yc_ref)


def _seq_spec(rows, col):
    return pl.BlockSpec((None, rows, HEAD_DIM), lambda b, h: (b, 0, col // HEAD_DIM + h))


def _gp_specs(rows):
    return [pl.BlockSpec((None, rows, LANES), lambda b, h: (b, 0, h)),
            pl.BlockSpec((None, rows // ML_CHUNK, N_GATE, ML_CHUNK), lambda b, h: (b, 0, h, 0))]


def _mlstm(zc, gpc, gptc, zl, gpl, gptl, norm, ctx_out):
    nb, tc, _ = zc.shape
    tl = zl.shape[1]
    ctx_in = [zc, zc, gpc, gptc] + ([zc, zc] if ctx_out else [])
    ctx_specs = [_seq_spec(tc, C_ML_K), _seq_spec(tc, C_ML_V)] + _gp_specs(tc) + (
        [_seq_spec(tc, C_ML_Q), _seq_spec(tc, C_ML_O)] if ctx_out else [])
    lat_in = [zl, zl, gpl, gptl, zl, zl]
    lat_specs = [_seq_spec(tl, C_ML_K), _seq_spec(tl, C_ML_V)] + _gp_specs(tl) + [
        _seq_spec(tl, C_ML_Q), _seq_spec(tl, C_ML_O)]
    out_spec = lambda rows: pl.BlockSpec((None, rows, HEAD_DIM), lambda b, h: (b, 0, h))
    out_shape = lambda rows: jax.ShapeDtypeStruct((nb, rows, BRANCH_W), BF16)
    scr = lambda rows: [pltpu.VMEM((rows, HEAD_DIM), F32), pltpu.VMEM((rows, HEAD_DIM), F32)]
    res = pl.pallas_call(
        functools.partial(_mlstm_kernel, ctx_out=ctx_out, n_ctx=tc // ML_CHUNK, n_lat=tl // ML_CHUNK),
        grid=(nb, N_HEADS),
        in_specs=ctx_specs + lat_specs + [pl.BlockSpec((1, HEAD_DIM), lambda b, h: (0, h))],
        out_specs=([out_spec(tc)] if ctx_out else []) + [out_spec(tl)],
        out_shape=([out_shape(tc)] if ctx_out else []) + [out_shape(tl)],
        scratch_shapes=(scr(tc) if ctx_out else []) + scr(tl),
        compiler_params=_params(("parallel", "parallel")),
        name="mlstm",
    )(*ctx_in, *lat_in, norm)
    return (res[0], res[1]) if ctx_out else (None, res[0])


def _conv_silu(x, w3):
    rows = x.shape[0]
    t = lax.broadcasted_iota(jnp.int32, x.shape, 0)
    prev = jnp.where(t == 0, 0.0, pltpu.roll(x, 1, 0))
    nxt = jnp.where(t == rows - 1, 0.0, pltpu.roll(x, rows - 1, 0))
    return _silu(w3[0:1, :] * prev + w3[1:2, :] * x + w3[2:3, :] * nxt)


def _l2norm(x):
    return x * lax.rsqrt(jnp.sum(x * x, axis=-1, keepdims=True) + EPS)


def _chunk_totals(gc_r, rev):
    L = GD_CHUNK
    pos = lax.broadcasted_iota(jnp.int32, gc_r.shape, 1)
    tot = jnp.zeros_like(gc_r)
    for c in range(GD_SUPER // L):
        i = c * L if rev else c * L + L - 1
        tot = jnp.where((pos >= c * L) & (pos < (c + 1) * L), gc_r[:, i:i + 1], tot)
    return tot


def _gdn_prepare(blocks):
    t = lax.broadcasted_iota(jnp.int32, (GD_SUPER, GD_SUPER), 0)
    s = lax.broadcasted_iota(jnp.int32, (GD_SUPER, GD_SUPER), 1)
    x = t ^ s
    same = x < GD_CHUNK
    kk0 = [_mm_nt(k, k) for k, _, _, _, _ in blocks]
    qk0 = [None if qs is None else _mm_nt(qs, k) for k, _, qs, _, _ in blocks]
    kT = [k.T for k, _, _, _, _ in blocks]
    prob = []
    for i, (k, v, qs, gp, gpt) in enumerate(blocks):
        for rev in (False, True):
            j_b, j_g = (5, 7) if rev else (4, 6)
            beta = gp[:, j_b:j_b + 1]
            gc_c = gp[:, j_g:j_g + 1]
            gc_r = gpt[j_g:j_g + 1, :]
            incl = same & ((s >= t) if rev else (s <= t))
            strict = same & ((s > t) if rev else (s < t))
            decay = jnp.exp(jnp.where(incl, gc_c - gc_r, -jnp.inf))
            m = jnp.where(strict, (beta * kk0[i]) * decay, 0.0)
            prob.append(dict(i=i, rev=rev, beta=beta, gc_r=gc_r, decay=decay, m=m, eg=jnp.exp(gc_c)))
    tinv = [jnp.where(x == 0, 1.0, 0.0) - jnp.where(x == 1, p["m"], 0.0) for p in prob]
    for lvl in range(1, GD_CHUNK.bit_length() - 1):
        tc = [_mm(tinv[n], jnp.where((x >> lvl) == 1, p["m"], 0.0)) for n, p in enumerate(prob)]
        tinv = [tinv[n] - _mm(tc[n], tinv[n]) for n in range(len(prob))]
    out = [[] for _ in blocks]
    for n, p in enumerate(prob):
        k, v, qs, _, _ = blocks[p["i"]]
        y = _mm(tinv[n], jnp.concatenate([v * p["beta"], (k * p["beta"]) * p["eg"]], axis=1))
        kendT = (kT[p["i"]] * jnp.exp(_chunk_totals(p["gc_r"], p["rev"]) - p["gc_r"])).astype(BF16)
        qd = None if qs is None else (qs * p["eg"]).astype(BF16)
        qk = None if qs is None else (qk0[p["i"]] * p["decay"]).astype(BF16)
        out[p["i"]].append((y[:, :HEAD_DIM], y[:, HEAD_DIM:].astype(BF16), kendT, qd, qk))
    return out


def _gdn_kernel(*refs, ctx_out, n_ctx, n_lat):
    n_in = 16 if ctx_out else 14
    n_out = 2 if ctx_out else 1
    if ctx_out:
        (kc_ref, vc_ref, gpc_ref, gptc_ref, qc_ref, zc_ref, kl_ref, vl_ref, gpl_ref, gptl_ref, ql_ref, zl_ref,
         wq_ref, wk_ref, wv_ref, norm_ref, yc_ref, yl_ref) = refs[:n_in + n_out]
    else:
        (kc_ref, vc_ref, gpc_ref, gptc_ref, kl_ref, vl_ref, gpl_ref, gptl_ref, ql_ref, zl_ref,
         wq_ref, wk_ref, wv_ref, norm_ref, yl_ref) = refs[:n_in + n_out]
        qc_ref = zc_ref = yc_ref = None
    ks, vs, qs, gpa, gpta, o_f, o_b = refs[n_in + n_out:n_in + n_out + 7]
    per_dir = refs[n_in + n_out + 7:]
    u_s, w_s, kT_s, qd_s, qk_s = (per_dir[0:2], per_dir[2:4], per_dir[4:6], per_dir[6:8], per_dir[8:10])
    L = GD_CHUNK
    SB = GD_SUPER
    tc = n_ctx * SB
    tiles = SB // ML_CHUNK

    ks[0:tc, :] = _l2norm(_conv_silu(kc_ref[...], wk_ref[...]))
    vs[0:tc, :] = _conv_silu(vc_ref[...], wv_ref[...])
    ks[tc:, :] = _l2norm(_conv_silu(kl_ref[...], wk_ref[...]))
    vs[tc:, :] = _conv_silu(vl_ref[...], wv_ref[...])
    qs[tc:, :] = _l2norm(_conv_silu(ql_ref[...], wq_ref[...])) * (HEAD_DIM ** -0.5)
    if ctx_out:
        qs[0:tc, :] = _l2norm(_conv_silu(qc_ref[...], wq_ref[...])) * (HEAD_DIM ** -0.5)
    gpa[0:tc, :] = gpc_ref[...]
    gpa[tc:, :] = gpl_ref[...]
    gpta[0:n_ctx * tiles] = gptc_ref[...]
    gpta[n_ctx * tiles:] = gptl_ref[...]

    def prepare(js, with_out):
        rows = [pl.ds(pl.multiple_of(j * SB, SB), SB) for j in js]
        blocks = [(ks[r, :], vs[r, :], qs[r, :] if with_out else None, gpa[r, :],
                   jnp.concatenate([gpta[j * tiles + i] for i in range(tiles)], axis=1))
                  for j, r in zip(js, rows)]
        for j, r, res in zip(js, rows, _gdn_prepare(blocks)):
            for d, (u, w, kT, qd, qk) in enumerate(res):
                u_s[d][r, :] = u
                w_s[d][r, :] = w
                kT_s[d][j] = kT
                if with_out:
                    qd_s[d][r, :] = qd
                    qk_s[d][r, :] = qk

    def prepare_loop(lo, n, with_out):
        group = GD_PREP_GROUP if n % GD_PREP_GROUP == 0 else 1

        def body(i, carry):
            prepare([lo + i * group + g for g in range(group)], with_out)
            return carry
        lax.fori_loop(0, n // group, body, 0)

    def chunk_rows(j, c):
        return pl.ds(pl.multiple_of(j * SB + c * L, L), L)

    def step_load(d, j, c, with_out):
        rows = chunk_rows(j, c)
        cols = slice(c * L, (c + 1) * L)
        edge = gpa[pl.ds(pl.multiple_of(j * SB + c * L, L) + (0 if d else L - 1), 1), :]
        g_end = jnp.exp(edge[:, 7:8] if d else edge[:, 6:7])
        out_ops = (qd_s[d][rows, :], qk_s[d][rows, cols]) if with_out else None
        return g_end, u_s[d][rows, :], w_s[d][rows, :], kT_s[d][j, :, cols], out_ops

    def step_compute(ops, S):
        g_end, u, w, kT, out_ops = ops
        Sb = S.astype(BF16)
        v_new = (u - jnp.dot(w, Sb, preferred_element_type=F32)).astype(BF16)
        out = None
        if out_ops is not None:
            out = (jnp.dot(out_ops[0], Sb, preferred_element_type=F32)
                   + jnp.dot(out_ops[1], v_new, preferred_element_type=F32))
        return S * g_end + jnp.dot(kT, v_new, preferred_element_type=F32), out

    def chunk_pair(lo, n, i, c):
        return (lo + i, c), (lo + n - 1 - i, SB // L - 1 - c)

    def scan(lo, n, first, last, carry, with_out):
        def body(i, carry):
            for c in range(SB // L):
                where = chunk_pair(lo, n, i, c)
                res = [step_compute(step_load(d, *where[d], with_out), carry[d]) for d in range(2)]
                if with_out:
                    o_f[chunk_rows(*where[0]), :] = res[0][1]
                    o_b[chunk_rows(*where[1]), :] = res[1][1]
                carry = (res[0][0], res[1][0])
            return carry
        return lax.fori_loop(first, last, body, carry)

    zero = jnp.zeros((HEAD_DIM, HEAD_DIM), F32)
    prepare_loop(0, n_ctx, ctx_out)
    prepare_loop(n_ctx, n_lat, True)
    carry = scan(0, n_ctx, 0, n_ctx, (zero, zero), ctx_out)
    scan(n_ctx, n_lat, 0, n_lat, carry, True)

    def finish(rows, z_ref, y_ref):
        y_ref[...] = ((_rms(o_f[rows, :] + o_b[rows, :]) * norm_ref[...]) * _silu(z_ref[...])).astype(y_ref.dtype)

    finish(slice(tc, None), zl_ref, yl_ref)
    if ctx_out:
        finish(slice(0, tc), zc_ref, yc_ref)


def _gdn(zc, gpc, gptc, zl, gpl, gptl, conv, norm, ctx_out):
    nb, tc, _ = zc.shape
    tl = zl.shape[1]
    assert tc % GD_SUPER == 0 and tl % GD_SUPER == 0, "sequences are processed in whole 256-token superblocks"
    ctx_in = [zc, zc, gpc, gptc] + ([zc, zc] if ctx_out else [])
    ctx_specs = [_seq_spec(tc, C_GD_K), _seq_spec(tc, C_GD_V)] + _gp_specs(tc) + (
        [_seq_spec(tc, C_GD_Q), _seq_spec(tc, C_GD_Z)] if ctx_out else [])
    lat_in = [zl, zl, gpl, gptl, zl, zl]
    lat_specs = [_seq_spec(tl, C_GD_K), _seq_spec(tl, C_GD_V)] + _gp_specs(tl) + [
        _seq_spec(tl, C_GD_Q), _seq_spec(tl, C_GD_Z)]
    conv_spec = lambda off: pl.BlockSpec((3, HEAD_DIM), lambda b, h: (0, off // HEAD_DIM + h))
    out_spec = lambda rows: pl.BlockSpec((None, rows, HEAD_DIM), lambda b, h: (b, 0, h))
    out_shape = lambda rows: jax.ShapeDtypeStruct((nb, rows, BRANCH_W), BF16)
    rows = tc + tl
    seq = lambda dt: pltpu.VMEM((rows, HEAD_DIM), dt)
    two = lambda spec: [spec, spec]
    scratch = ([seq(F32), seq(F32), seq(F32), seq(F32), pltpu.VMEM((rows // ML_CHUNK, N_GATE, ML_CHUNK), F32),
                seq(F32), seq(F32)]
               + two(seq(F32)) + two(seq(BF16)) + two(pltpu.VMEM((rows // GD_SUPER, HEAD_DIM, GD_SUPER), BF16))
               + two(seq(BF16)) + two(pltpu.VMEM((rows, GD_SUPER), BF16)))
    res = pl.pallas_call(
        functools.partial(_gdn_kernel, ctx_out=ctx_out, n_ctx=tc // GD_SUPER, n_lat=tl // GD_SUPER),
        grid=(nb, N_HEADS),
        in_specs=ctx_specs + lat_specs + [conv_spec(0), conv_spec(BRANCH_W), conv_spec(2 * BRANCH_W),
                                          pl.BlockSpec((1, HEAD_DIM), lambda b, h: (0, 0))],
        out_specs=([out_spec(tc)] if ctx_out else []) + [out_spec(tl)],
        out_shape=([out_shape(tc)] if ctx_out else []) + [out_shape(tl)],
        scratch_shapes=scratch,
        compiler_params=_params(("parallel", "parallel")),
        name="gdn",
    )(*ctx_in, *lat_in, conv, conv, conv, norm)
    return (res[0], res[1]) if ctx_out else (None, res[0])


def _rope(x, cos, sin):
    lane = lax.broadcasted_iota(jnp.int32, x.shape, 1)
    quarter = HEAD_DIM // 4
    partner = jnp.where((lane & (2 * quarter - 1)) < quarter,
                        pltpu.roll(x, HEAD_DIM - quarter, 1), pltpu.roll(x, quarter, 1))
    return x * cos + partner * sin


def _attn_kernel(*refs, latent, tc):
    if latent:
        (q_ref, kc_ref, vc_ref, kl_ref, vl_ref, cos_ref, sin_ref, cosq_ref, sinq_ref, qn_ref, kn_ref,
         o_ref, k_scr, v_scr) = refs
    else:
        q_ref, kc_ref, vc_ref, qn_ref, kn_ref, o_ref, k_scr, v_scr = refs

    @pl.when(pl.program_id(2) == 0)
    def _():
        k_scr[0:tc, :] = (_rms(kc_ref[...]) * kn_ref[...]).astype(BF16)
        v_scr[0:tc, :] = vc_ref[...].astype(BF16)
        if latent:
            kl = _rope(_rms(kl_ref[...]) * kn_ref[...], cos_ref[...], sin_ref[...])
            k_scr[tc:, :] = kl.astype(BF16)
            v_scr[tc:, :] = vl_ref[...].astype(BF16)

    scale = HEAD_DIM ** -0.5
    for g in range(AT_Q_HEADS // AT_KV_HEADS):
        cols = slice(g * HEAD_DIM, (g + 1) * HEAD_DIM)
        q = _rms(q_ref[:, cols]) * qn_ref[...]
        if latent:
            q = _rope(q, cosq_ref[...], sinq_ref[...])
        s = lax.dot_general((q * scale).astype(BF16), k_scr[...], (((1,), (1,)), ((), ())),
                            preferred_element_type=F32)
        p = jnp.exp(s - jnp.max(s, axis=-1, keepdims=True))
        l = jnp.sum(p, axis=-1, keepdims=True)
        o_ref[:, cols] = (jnp.dot(p.astype(BF16), v_scr[...], preferred_element_type=F32) / l).astype(o_ref.dtype)


def _attention(zq, zc, zl, cos, sin, qn, kn):
    latent = zl is not None
    nb, tq_total, _ = zq.shape
    tc = zc.shape[1]
    tk = tc + (zl.shape[1] if latent else 0)
    gq = (AT_Q_HEADS // AT_KV_HEADS) * HEAD_DIM
    head_spec = lambda rows, col: pl.BlockSpec((None, rows, HEAD_DIM), lambda b, h, i: (b, 0, col // HEAD_DIM + h))
    vec_spec = pl.BlockSpec((1, HEAD_DIM), lambda b, h, i: (0, 0))
    in_specs = [pl.BlockSpec((None, TQ_ATTN, gq), lambda b, h, i: (b, i, C_AT_Q // gq + h)),
                head_spec(tc, C_AT_K), head_spec(tc, C_AT_V)]
    args = [zq, zc, zc]
    if latent:
        tl = zl.shape[1]
        in_specs += [head_spec(tl, C_AT_K), head_spec(tl, C_AT_V),
                     pl.BlockSpec((tl, HEAD_DIM), lambda b, h, i: (0, 0)),
                     pl.BlockSpec((tl, HEAD_DIM), lambda b, h, i: (0, 0)),
                     pl.BlockSpec((TQ_ATTN, HEAD_DIM), lambda b, h, i: (i, 0)),
                     pl.BlockSpec((TQ_ATTN, HEAD_DIM), lambda b, h, i: (i, 0))]
        args += [zl, zl, cos, sin, cos, sin]
    return pl.pallas_call(
        functools.partial(_attn_kernel, latent=latent, tc=tc),
        grid=(nb, AT_KV_HEADS, tq_total // TQ_ATTN),
        in_specs=in_specs + [vec_spec, vec_spec],
        out_specs=pl.BlockSpec((None, TQ_ATTN, gq), lambda b, h, i: (b, i, h)),
        out_shape=jax.ShapeDtypeStruct((nb, tq_total, BRANCH_W), BF16),
        scratch_shapes=[pltpu.VMEM((tk, HEAD_DIM), BF16), pltpu.VMEM((tk, HEAD_DIM), BF16)],
        compiler_params=_params(("parallel", "parallel", "arbitrary")),
        name="attention",
    )(*args, qn, kn)


def _merge_kernel(h_ref, xcol_ref, modcol_ref, y0_ref, y1_ref, y2_ref, y3_ref,
                  wg0_ref, wg1_ref, wg2_ref, wg3_ref, wb_ref, wo_ref, o_ref, m_scr):
    s = pl.program_id(2)
    n_gate = D_MODEL // TN_MERGE
    ys = (y0_ref, y1_ref, y2_ref, y3_ref)
    wgs = (wg0_ref, wg1_ref, wg2_ref, wg3_ref)

    @pl.when(s < n_gate)
    def _():
        h = h_ref[...]
        acc = jnp.zeros((TM_MERGE, TN_MERGE), F32)
        for n in range(N_BRANCH):
            gate = jnp.dot(h, wgs[n][...], preferred_element_type=F32)
            acc = acc + _sigmoid(gate) * jnp.dot(ys[n][...], wb_ref[n], preferred_element_type=F32)
        m_scr[s] = acc.astype(BF16)

    @pl.when(s >= n_gate)
    def _():
        acc = jnp.zeros((TM_MERGE, TN_OUT), F32)
        for kk in range(n_gate):
            acc = acc + jnp.dot(m_scr[kk], wo_ref[kk * TN_MERGE:(kk + 1) * TN_MERGE, :],
                                preferred_element_type=F32)
        o_ref[...] = xcol_ref[...] + modcol_ref[5:6, :] * acc


def _merge(x, h, mod, ys, w_gate, w_branch, w_out, layer, mod_row):
    groups, rows, _ = x.shape
    n_gate = D_MODEL // TN_MERGE
    n_out = D_MODEL // TN_OUT
    mrow = (lambda b: b) if mod_row is None else (lambda b: mod_row)
    gate_col = lambda s: jnp.minimum(s, n_gate - 1)
    out_col = lambda s: jnp.maximum(s - n_gate, 0)
    y_spec = pl.BlockSpec((None, TM_MERGE, BRANCH_W), lambda b, i, s: (b, i, 0))
    wg_spec = lambda n: pl.BlockSpec((None, D_MODEL, TN_MERGE),
                                     lambda b, i, s: (layer, 0, n * n_gate + gate_col(s)))
    return pl.pallas_call(
        _merge_kernel,
        grid=(groups, rows // TM_MERGE, n_gate + n_out),
        in_specs=[pl.BlockSpec((None, TM_MERGE, D_MODEL), lambda b, i, s: (b, i, 0)),
                  pl.BlockSpec((None, TM_MERGE, TN_OUT), lambda b, i, s: (b, i, out_col(s))),
                  pl.BlockSpec((None, N_MOD, TN_OUT), lambda b, i, s: (mrow(b), 0, out_col(s))),
                  y_spec, y_spec, y_spec, y_spec,
                  wg_spec(0), wg_spec(1), wg_spec(2), wg_spec(3),
                  pl.BlockSpec((None, N_BRANCH, BRANCH_W, TN_MERGE), lambda b, i, s: (layer, 0, 0, gate_col(s))),
                  pl.BlockSpec((None, D_MODEL, TN_OUT), lambda b, i, s: (layer, 0, out_col(s)))],
        out_specs=pl.BlockSpec((None, TM_MERGE, TN_OUT), lambda b, i, s: (b, i, out_col(s))),
        out_shape=jax.ShapeDtypeStruct(x.shape, F32),
        scratch_shapes=[pltpu.VMEM((n_gate, TM_MERGE, TN_MERGE), BF16)],
        compiler_params=_params(("parallel", "parallel", "arbitrary")),
        name="merge",
    )(h, x, mod, *ys, w_gate, w_gate, w_gate, w_gate, w_branch, w_out)


def _rope_tables(seq):
    n = HEAD_DIM // 4
    inv = ROPE_THETA ** (-jnp.arange(n, dtype=F32) / n)
    pos = jnp.arange(seq)
    ar = (pos // GRID_W).astype(F32)[:, None] * inv
    ac = (pos % GRID_W).astype(F32)[:, None] * inv
    cos = jnp.concatenate([jnp.cos(ar), jnp.cos(ar), jnp.cos(ac), jnp.cos(ac)], axis=1)
    sin = jnp.concatenate([-jnp.sin(ar), jnp.sin(ar), -jnp.sin(ac), jnp.sin(ac)], axis=1)
    return cos, sin


def _gate_params(ml_if_bias, gd_a_log, gd_dt_bias):
    zeros = jnp.zeros((2, N_HEADS), F32)
    bias = jnp.concatenate([ml_if_bias.reshape(4, N_HEADS), zeros, gd_dt_bias], axis=0).T
    alog = jnp.concatenate([jnp.zeros((6, N_HEADS), F32), gd_a_log], axis=0).T
    pad = lambda a: jnp.pad(a.reshape(1, N_HEADS * N_GATE), ((0, 0), (0, LANES - N_HEADS * N_GATE)))
    col = lambda a: a.reshape(N_HEADS * N_GATE, 1)
    return pad(bias), pad(alog), col(bias), col(alog)


def _gate_weights(w_in):
    depth = w_in.shape[0]
    g = jnp.concatenate([w_in[..., R_ML_IF:R_ML_IF + 16], w_in[..., R_GD_BA:R_GD_BA + 16]], axis=-1)
    g = g.reshape(depth, D_MODEL, N_GATE, N_HEADS).transpose(0, 1, 3, 2).reshape(depth, D_MODEL, N_HEADS * N_GATE)
    w_gc = jnp.pad(g, ((0, 0), (0, 0), (0, LANES - N_HEADS * N_GATE)))
    w_gr = g.transpose(0, 2, 1)
    return w_gc.astype(BF16), w_gr.astype(BF16)


def kernel(x, c, ctx, c_ctx, mod_w, mod_b, ffn1_norm, ffn1_w_in, ffn1_w_out, mix_norm, w_in, sg_norm, sg_w, sg_b, ml_if_bias, ml_norm, gd_conv, gd_a_log, gd_dt_bias, gd_norm, at_q_norm, at_k_norm, w_branch, w_out, ffn2_norm, ffn2_w_in, ffn2_w_out):
    nb, seq, _ = x.shape
    tc = ctx.shape[1]
    depth = mod_w.shape[0]
    ctx_row = nb
    cos, sin = _rope_tables(seq)

    cc = jnp.zeros((MOD_ROWS, D_MODEL), F32).at[:nb].set(c).at[ctx_row].set(c_ctx)
    mods = _modulation(cc, mod_w, mod_b).reshape(depth, MOD_ROWS, N_MOD, D_MODEL)

    f1_in, f1_out, f2_in, f2_out = [_to_bf16(w) for w in (ffn1_w_in, ffn1_w_out, ffn2_w_in, ffn2_w_out)]
    w_main, w_gate = _split_w_in(w_in)
    w_gc_all, w_gr_all = _gate_weights(w_in)
    wb, wo = w_branch.astype(BF16), w_out.astype(BF16)
    flat = lambda a: a.reshape(1, nb * tc, a.shape[-1])
    unflat = lambda a: a.reshape(nb, tc, a.shape[-1])
    xc = ctx
    for l in range(depth):
        last = l == depth - 1
        ctx_out = not last
        mod = mods[l]
        w_gc, w_gr = w_gc_all[l], w_gr_all[l]
        bias_c, alog_c, bias_r, alog_r = _gate_params(ml_if_bias[l], gd_a_log[l], gd_dt_bias[l])
        sgw = sg_w[l].astype(BF16)
        sg_bias = jnp.repeat(sg_b[l].T, BRANCH_W // SG_GROUPS, axis=1)
        sgn = sg_norm[l].reshape(1, BRANCH_W)
        mln = ml_norm[l].reshape(1, BRANCH_W)
        gdn_g = gd_norm[l].reshape(1, HEAD_DIM)
        qn, kn = at_q_norm[l].reshape(1, HEAD_DIM), at_k_norm[l].reshape(1, HEAD_DIM)

        x = _ffn(x, mod, ffn1_norm[l], f1_in, f1_out, l, 0, None)
        xc = unflat(_ffn(flat(xc), mod, ffn1_norm[l], f1_in, f1_out, l, 0, ctx_row))

        zl, gcl, grl, hl = _inproj(x, mod, mix_norm[l], w_main, l, w_gc, w_gr, N_MAIN, None)
        zc, gcc, grc, hc = _inproj(flat(xc), mod, mix_norm[l], w_main, l, w_gc, w_gr,
                                   N_MAIN if ctx_out else KV_MAIN, ctx_row)
        zc = unflat(zc)
        gcc = unflat(gcc)
        grc = grc.reshape(N_HEADS * N_GATE, nb, tc).transpose(1, 0, 2)
        gpl, gptl = _gateprep(gcl, grl, bias_c, alog_c, bias_r, alog_r)
        gpc, gptc = _gateprep(gcc, grc, bias_c, alog_c, bias_r, alog_r)

        y_sg_l = _spatial_gating(zl, sgn, sgw, sg_bias)
        y_ml_c, y_ml_l = _mlstm(zc, gpc, gptc, zl, gpl, gptl, mln, ctx_out)
        y_gd_c, y_gd_l = _gdn(zc, gpc, gptc, zl, gpl, gptl, gd_conv[l], gdn_g, ctx_out)
        y_at_l = _attention(zl, zc, zl, cos, sin, qn, kn)
        x = _merge(x, hl, mod, [y_sg_l, y_ml_l, y_gd_l, y_at_l], w_gate, wb, wo, l, None)
        if ctx_out:
            y_sg_c = _spatial_gating(zc, sgn, sgw, sg_bias)
            y_at_c = _attention(zc, zc, None, None, None, qn, kn)
            xc = unflat(_merge(flat(xc), hc, mod, [flat(y) for y in (y_sg_c, y_ml_c, y_gd_c, y_at_c)],
                               w_gate, wb, wo, l, ctx_row))

        x = _ffn(x, mod, ffn2_norm[l], f2_in, f2_out, l, 6, None)
        if ctx_out:
            xc = unflat(_ffn(flat(xc), mod, ffn2_norm[l], f2_in, f2_out, l, 6, ctx_row))
    return x
```

```python
import functools
import math

import jax
import jax.numpy as jnp
from jax import lax
from jax.experimental import pallas as pl
from jax.experimental.pallas import tpu as pltpu

F32 = jnp.float32
BF16 = jnp.bfloat16
EPS = 1e-6

D_MODEL = 2048
D_FF = 5632
N_MOD = 9
N_BRANCH = 4
BRANCH_W = 512
HEAD_DIM = 128
N_HEADS = 4
SG_GROUPS = 4
ML_CHUNK = 128
GD_CHUNK = 128
GD_SUPER = 256
GD_PREP_GROUP = 2
AT_Q_HEADS = 4
AT_KV_HEADS = 2
GRID_W = 64
ROPE_THETA = 10000.0
FFN_RESIDUAL = 0.5

LANES = 128
N_GATE = 8
MOD_ROWS = 16

C_ML_K, C_ML_V = 0, 512
C_GD_K, C_GD_V = 1024, 1536
C_AT_K, C_AT_V = 2048, 2304
KV_MAIN = 2560
C_SG_U, C_SG_V = 2560, 3072
C_ML_Q, C_ML_O = 3584, 4096
C_GD_Q, C_GD_Z = 4608, 5120
C_AT_Q = 5632
N_MAIN = 6144
R_ML_IF = 1024
R_GD_KV = 1040
R_GD_BA = 2064
R_AT_KV = 2080
R_REST = 2592
R_GATE = 6176

VMEM_LIMIT = 56 * 1024 * 1024

TM_FFN = 512
TF_FFN = 512
TM_PROJ = 1024
TN_PROJ = 512
TM_MERGE = 1024
TN_MERGE = 256
TN_OUT = 512
TQ_ATTN = 256
TR_SG = 256
TG_GATE = 256
CAST_BLOCK_BYTES = 8 * 1024 * 1024
SPLIT_ROWS = 128
TK_MOD = 128


def _params(sem):
    return pltpu.CompilerParams(dimension_semantics=sem, vmem_limit_bytes=VMEM_LIMIT)


def _mm(a, b):
    return jnp.dot(a.astype(BF16), b.astype(BF16), preferred_element_type=F32)


def _mm_nt(a, b):
    return lax.dot_general(a.astype(BF16), b.astype(BF16), (((1,), (1,)), ((), ())),
                           preferred_element_type=F32)


def _mm_tn(a, b):
    return lax.dot_general(a.astype(BF16), b.astype(BF16), (((0,), (0,)), ((), ())),
                           preferred_element_type=F32)


def _sigmoid(x):
    return 1.0 / (1.0 + jnp.exp(-x))


def _silu(x):
    return x * _sigmoid(x)


def _softplus(x):
    return jnp.maximum(x, 0.0) + jnp.log1p(jnp.exp(-jnp.abs(x)))


def _gelu_tanh(x):
    c = math.sqrt(2.0 / math.pi)
    return x * (0.5 * (1.0 + jnp.tanh(c * (x + 0.044715 * (x * x * x)))))


def _rms(x):
    return x * lax.rsqrt(jnp.mean(x * x, axis=-1, keepdims=True) + EPS)


def _norm_mod(x, gain, shift, scale):
    return (_rms(x) * gain) * (1.0 + scale) + shift


def _store_col_tiles(value, o_ref):
    tile = o_ref.shape[-1]
    for t in range(o_ref.shape[0]):
        o_ref[t] = value[:, t * tile:(t + 1) * tile].astype(o_ref.dtype)


def _cast_kernel(x_ref, o_ref):
    if len(o_ref.shape) == 3:
        _store_col_tiles(x_ref[...], o_ref)
    else:
        o_ref[...] = x_ref[...].astype(o_ref.dtype)


def _to_bf16(w, col_tile=None):
    depth, r, c = w.shape
    rows = 1 << (max(16, min(r, CAST_BLOCK_BYTES // (4 * c))).bit_length() - 1)
    while r % rows:
        rows //= 2
    if col_tile is None:
        out_spec = pl.BlockSpec((None, rows, c), lambda l, i: (l, i, 0))
        out_shape = jax.ShapeDtypeStruct(w.shape, BF16)
    else:
        out_spec = pl.BlockSpec((None, c // col_tile, rows, col_tile), lambda l, i: (l, 0, i, 0))
        out_shape = jax.ShapeDtypeStruct((depth, c // col_tile, r, col_tile), BF16)
    return pl.pallas_call(
        _cast_kernel,
        grid=(depth, r // rows),
        in_specs=[pl.BlockSpec((None, rows, c), lambda l, i: (l, i, 0))],
        out_specs=out_spec,
        out_shape=out_shape,
        compiler_params=_params(("parallel", "parallel")),
        name="cast_bf16",
    )(w)


def _split_w_in_kernel(w_ref, main_ref, gate_ref):
    w = w_ref[...]
    _store_col_tiles(jnp.concatenate([w[:, :R_ML_IF], w[:, R_GD_KV:R_GD_BA], w[:, R_AT_KV:R_GATE]], axis=1), main_ref)
    _store_col_tiles(w[:, R_GATE:], gate_ref)


def _split_w_in(w_in):
    depth, d, n = w_in.shape
    n_gate = n - R_GATE
    return pl.pallas_call(
        _split_w_in_kernel,
        grid=(depth, d // SPLIT_ROWS),
        in_specs=[pl.BlockSpec((None, SPLIT_ROWS, n), lambda l, i: (l, i, 0))],
        out_specs=[pl.BlockSpec((None, N_MAIN // TN_PROJ, SPLIT_ROWS, TN_PROJ), lambda l, i: (l, 0, i, 0)),
                   pl.BlockSpec((None, n_gate // TN_MERGE, SPLIT_ROWS, TN_MERGE), lambda l, i: (l, 0, i, 0))],
        out_shape=[jax.ShapeDtypeStruct((depth, N_MAIN // TN_PROJ, d, TN_PROJ), BF16),
                   jax.ShapeDtypeStruct((depth, n_gate // TN_MERGE, d, TN_MERGE), BF16)],
        compiler_params=_params(("parallel", "parallel")),
        name="split_w_in",
    )(w_in)


def _mod_kernel(cc_ref, w_ref, b_ref, o_ref):
    @pl.when(pl.program_id(1) == 0)
    def _():
        o_ref[...] = jnp.broadcast_to(b_ref[...], o_ref.shape)

    o_ref[...] += _mm(_silu(cc_ref[...]), w_ref[...])


def _modulation(cc, mod_w, mod_b):
    depth, d, n = mod_w.shape
    ccs = cc.reshape(MOD_ROWS, d // TK_MOD, TK_MOD).transpose(1, 0, 2)
    return pl.pallas_call(
        _mod_kernel,
        grid=(depth, d // TK_MOD),
        in_specs=[pl.BlockSpec((None, MOD_ROWS, TK_MOD), lambda l, k: (k, 0, 0)),
                  pl.BlockSpec((None, TK_MOD, n), lambda l, k: (l, k, 0)),
                  pl.BlockSpec((None, 1, n), lambda l, k: (l, 0, 0))],
        out_specs=pl.BlockSpec((None, MOD_ROWS, n), lambda l, k: (l, 0, 0)),
        out_shape=jax.ShapeDtypeStruct((depth, MOD_ROWS, n), F32),
        compiler_params=_params(("parallel", "arbitrary")),
        name="modulation",
    )(ccs, mod_w, mod_b.reshape(depth, 1, n))


def _ffn_kernel(x_ref, mod_ref, xn_ref, modn_ref, g_ref, wa_ref, wb_ref, wo_ref, o_ref, h_scr, acc_scr, *, base):
    j = pl.program_id(2)
    last = pl.num_programs(2) - 1
    tile = pl.program_id(0) * pl.num_programs(1) + pl.program_id(1)
    slot = tile % 2

    def prepare(x, mod, into):
        h = _norm_mod(x[...], g_ref[...], mod[base:base + 1, :], mod[base + 1:base + 2, :])
        h_scr[into] = h.astype(BF16)

    @pl.when((tile == 0) & (j == 0))
    def _():
        prepare(x_ref, mod_ref, 0)

    def hidden_slice():
        h = h_scr[slot]
        a = jnp.dot(h, wa_ref[...], preferred_element_type=F32)
        b = jnp.dot(h, wb_ref[...], preferred_element_type=F32)
        return _mm(_silu(a) * b, wo_ref[...])

    @pl.when(j == 0)
    def _():
        acc_scr[...] = hidden_slice()

    @pl.when((j > 0) & (j < last))
    def _():
        acc_scr[...] += hidden_slice()

    @pl.when(j == last)
    def _():
        total = acc_scr[...] + hidden_slice()
        prepare(xn_ref, modn_ref, 1 - slot)
        o_ref[...] = x_ref[...] + (FFN_RESIDUAL * mod_ref[base + 2:base + 3, :]) * total


def _ffn(x, mod, gain, w_in, w_out, layer, base, mod_row):
    groups, rows, _ = x.shape
    nj = D_FF // TF_FFN
    ni = rows // TM_FFN
    mrow = (lambda b: b) if mod_row is None else (lambda b: mod_row)

    def next_tile(b, i):
        t = jnp.minimum(b * ni + i + 1, groups * ni - 1)
        return t // ni, t % ni

    return pl.pallas_call(
        functools.partial(_ffn_kernel, base=base),
        grid=(groups, ni, nj),
        in_specs=[pl.BlockSpec((None, TM_FFN, D_MODEL), lambda b, i, j: (b, i, 0)),
                  pl.BlockSpec((None, N_MOD, D_MODEL), lambda b, i, j: (mrow(b), 0, 0)),
                  pl.BlockSpec((None, TM_FFN, D_MODEL), lambda b, i, j: (*next_tile(b, i), 0)),
                  pl.BlockSpec((None, N_MOD, D_MODEL), lambda b, i, j: (mrow(next_tile(b, i)[0]), 0, 0)),
                  pl.BlockSpec((1, D_MODEL), lambda b, i, j: (0, 0)),
                  pl.BlockSpec((None, None, D_MODEL, TF_FFN), lambda b, i, j: (layer, j, 0, 0)),
                  pl.BlockSpec((None, None, D_MODEL, TF_FFN), lambda b, i, j: (layer, j + nj, 0, 0)),
                  pl.BlockSpec((None, TF_FFN, D_MODEL), lambda b, i, j: (layer, j, 0))],
        out_specs=pl.BlockSpec((None, TM_FFN, D_MODEL), lambda b, i, j: (b, i, 0)),
        out_shape=jax.ShapeDtypeStruct(x.shape, F32),
        scratch_shapes=[pltpu.VMEM((2, TM_FFN, D_MODEL), BF16), pltpu.VMEM((TM_FFN, D_MODEL), F32)],
        compiler_params=_params(("arbitrary", "arbitrary", "arbitrary")),
        name="ffn",
    )(x, mod, x, mod, gain.reshape(1, D_MODEL), w_in, w_in, w_out)


def _inproj_kernel(x_ref, mod_ref, g_ref, w_ref, wgc_ref, wgr_ref, z_ref, gc_ref, gr_ref, h_ref):
    n = pl.program_id(2)

    @pl.when(n == 0)
    def _():
        h = _norm_mod(x_ref[...], g_ref[...], mod_ref[3:4, :], mod_ref[4:5, :]).astype(BF16)
        h_ref[...] = h
        gc_ref[...] = jnp.dot(h, wgc_ref[...], preferred_element_type=F32)
        gr_ref[...] = lax.dot_general(wgr_ref[...], h, (((1,), (1,)), ((), ())), preferred_element_type=F32)

    z_ref[...] = jnp.dot(h_ref[...], w_ref[...], preferred_element_type=F32)


def _inproj(x, mod, gain, w_main, layer, w_gc, w_gr, n_cols, mod_row):
    groups, rows, _ = x.shape
    mod_idx = (lambda b, i, n: (b, 0, 0)) if mod_row is None else (lambda b, i, n: (mod_row, 0, 0))
    return pl.pallas_call(
        _inproj_kernel,
        grid=(groups, rows // TM_PROJ, n_cols // TN_PROJ),
        in_specs=[pl.BlockSpec((None, TM_PROJ, D_MODEL), lambda b, i, n: (b, i, 0)),
                  pl.BlockSpec((None, N_MOD, D_MODEL), mod_idx),
                  pl.BlockSpec((1, D_MODEL), lambda b, i, n: (0, 0)),
                  pl.BlockSpec((None, None, D_MODEL, TN_PROJ), lambda b, i, n: (layer, n, 0, 0)),
                  pl.BlockSpec((D_MODEL, LANES), lambda b, i, n: (0, 0)),
                  pl.BlockSpec((N_HEADS * N_GATE, D_MODEL), lambda b, i, n: (0, 0))],
        out_specs=[pl.BlockSpec((None, TM_PROJ, TN_PROJ), lambda b, i, n: (b, i, n)),
                   pl.BlockSpec((None, TM_PROJ, LANES), lambda b, i, n: (b, i, 0)),
                   pl.BlockSpec((None, N_HEADS * N_GATE, TM_PROJ), lambda b, i, n: (b, 0, i)),
                   pl.BlockSpec((None, TM_PROJ, D_MODEL), lambda b, i, n: (b, i, 0))],
        out_shape=[jax.ShapeDtypeStruct((groups, rows, n_cols), F32),
                   jax.ShapeDtypeStruct((groups, rows, LANES), F32),
                   jax.ShapeDtypeStruct((groups, N_HEADS * N_GATE, rows), F32),
                   jax.ShapeDtypeStruct((groups, rows, D_MODEL), BF16)],
        compiler_params=_params(("parallel", "parallel", "arbitrary")),
        name="inproj",
    )(x, mod, gain.reshape(1, D_MODEL), w_main, w_gc, w_gr)


def _split3(e):
    e1 = e.astype(BF16)
    r1 = e - e1.astype(F32)
    e2 = r1.astype(BF16)
    e3 = (r1 - e2.astype(F32)).astype(BF16)
    return e1, e2, e3


def _gate_values(pre, a_log, cls):
    neg_sp = -_softplus(-pre)
    return jnp.where(cls < 2, pre,
                     jnp.where(cls < 4, neg_sp,
                               jnp.where(cls < 6, _sigmoid(pre), -jnp.exp(a_log) * _softplus(pre))))


def _gateprep_kernel(gc_ref, gr_ref, bc_ref, ac_ref, br_ref, ar_ref, oc_ref, or_ref):
    for i in range(TG_GATE // ML_CHUNK):
        tile = slice(i * ML_CHUNK, (i + 1) * ML_CHUNK)
        out_c, or_ref[i] = _gateprep_tile(gc_ref[tile, :], gr_ref[:, tile], bc_ref[...], ac_ref[...],
                                          br_ref[...], ar_ref[...])
        for h in range(N_HEADS):
            oc_ref[tile, h * LANES:(h + 1) * LANES] = (
                out_c if h == 0 else pltpu.roll(out_c, LANES - h * N_GATE, 1))


def _gateprep_tile(gc, gr, bias_c, alog_c, bias_r, alog_r):
    L = ML_CHUNK
    half = GD_CHUNK
    pre = gc + bias_c
    cls = lax.broadcasted_iota(jnp.int32, pre.shape, 1) & (N_GATE - 1)
    t = lax.broadcasted_iota(jnp.int32, pre.shape, 0)
    e = _gate_values(pre, alog_c, cls)
    tri = (lax.broadcasted_iota(jnp.int32, (L, L), 1) <= lax.broadcasted_iota(jnp.int32, (L, L), 0)).astype(BF16)
    e1, e2, e3 = _split3(e)
    p = (jnp.dot(tri, e1, preferred_element_type=F32) + jnp.dot(tri, e2, preferred_element_type=F32)
         + jnp.dot(tri, e3, preferred_element_type=F32))
    tot = p[L - 1:L, :]
    mid = p[half - 1:half, :]
    second = t >= half
    pre64 = p - jnp.where(second, mid, 0.0)
    suf64 = jnp.where(second, tot, mid) - p + e
    cum = jnp.where(cls == 2, p, jnp.where(cls == 3, tot - p + e, jnp.where(cls == 6, pre64, suf64)))
    out_c = jnp.where((cls == 2) | (cls == 3) | (cls == 6) | (cls == 7), cum, e)

    pre_r = gr + bias_r
    cls_r = lax.broadcasted_iota(jnp.int32, pre_r.shape, 0) & (N_GATE - 1)
    t_r = lax.broadcasted_iota(jnp.int32, pre_r.shape, 1)
    e_r = _gate_values(pre_r, alog_r, cls_r)
    tri_r = (lax.broadcasted_iota(jnp.int32, (L, L), 0) <= lax.broadcasted_iota(jnp.int32, (L, L), 1)).astype(BF16)
    f1, f2, f3 = _split3(e_r)
    pr = (jnp.dot(f1, tri_r, preferred_element_type=F32) + jnp.dot(f2, tri_r, preferred_element_type=F32)
          + jnp.dot(f3, tri_r, preferred_element_type=F32))
    tot_r = pr[:, L - 1:L]
    mid_r = pr[:, half - 1:half]
    second_r = t_r >= half
    pre64_r = pr - jnp.where(second_r, mid_r, 0.0)
    suf64_r = jnp.where(second_r, tot_r, mid_r) - pr + e_r
    cum_r = jnp.where(cls_r == 2, pr, jnp.where(cls_r == 3, tot_r - pr + e_r, jnp.where(cls_r == 6, pre64_r, suf64_r)))
    return out_c, jnp.where((cls_r == 2) | (cls_r == 3) | (cls_r == 6) | (cls_r == 7), cum_r, e_r)


def _gateprep(gc, gr, bias_c, alog_c, bias_r, alog_r):
    groups, rows, _ = gc.shape
    nt = rows // ML_CHUNK
    hw = N_HEADS * LANES
    hg = N_HEADS * N_GATE
    tiles = TG_GATE // ML_CHUNK
    return pl.pallas_call(
        _gateprep_kernel,
        grid=(groups, rows // TG_GATE),
        in_specs=[pl.BlockSpec((None, TG_GATE, LANES), lambda b, c: (b, c, 0)),
                  pl.BlockSpec((None, hg, TG_GATE), lambda b, c: (b, 0, c)),
                  pl.BlockSpec((1, LANES), lambda b, c: (0, 0)),
                  pl.BlockSpec((1, LANES), lambda b, c: (0, 0)),
                  pl.BlockSpec((hg, 1), lambda b, c: (0, 0)),
                  pl.BlockSpec((hg, 1), lambda b, c: (0, 0))],
        out_specs=[pl.BlockSpec((None, TG_GATE, hw), lambda b, c: (b, c, 0)),
                   pl.BlockSpec((None, tiles, hg, ML_CHUNK), lambda b, c: (b, c, 0, 0))],
        out_shape=[jax.ShapeDtypeStruct((groups, rows, hw), F32),
                   jax.ShapeDtypeStruct((groups, nt, hg, ML_CHUNK), F32)],
        compiler_params=_params(("parallel", "parallel")),
        name="gateprep",
    )(gc, gr, bias_c, alog_c, bias_r, alog_r)


def _sg_kernel(u_ref, v_ref, gn_ref, w_ref, bias_ref, o_ref):
    gw = BRANCH_W // SG_GROUPS
    for ch in range(TR_SG // ML_CHUNK):
        rows = slice(ch * ML_CHUNK, (ch + 1) * ML_CHUNK)
        v = _rms(_gelu_tanh(v_ref[rows, :])) * gn_ref[...]
        u = _gelu_tanh(u_ref[rows, :])
        for g in range(SG_GROUPS):
            cols = slice(g * gw, (g + 1) * gw)
            s = _mm(w_ref[g], v[:, cols]) + bias_ref[:, cols]
            o_ref[rows, cols] = (u[:, cols] * s).astype(o_ref.dtype)


def _spatial_gating(z, gn, w_s, bias_full):
    groups, rows, _ = z.shape
    ub, vb = C_SG_U // BRANCH_W, C_SG_V // BRANCH_W
    return pl.pallas_call(
        _sg_kernel,
        grid=(groups, rows // TR_SG),
        in_specs=[pl.BlockSpec((None, TR_SG, BRANCH_W), lambda b, i: (b, i, ub)),
                  pl.BlockSpec((None, TR_SG, BRANCH_W), lambda b, i: (b, i, vb)),
                  pl.BlockSpec((1, BRANCH_W), lambda b, i: (0, 0)),
                  pl.BlockSpec((SG_GROUPS, ML_CHUNK, ML_CHUNK), lambda b, i: (0, 0, 0)),
                  pl.BlockSpec((ML_CHUNK, BRANCH_W), lambda b, i: (0, 0))],
        out_specs=pl.BlockSpec((None, TR_SG, BRANCH_W), lambda b, i: (b, i, 0)),
        out_shape=jax.ShapeDtypeStruct((groups, rows, BRANCH_W), BF16),
        compiler_params=_params(("parallel", "parallel")),
        name="spatial_gating",
    )(z, z, gn, w_s, bias_full)


def _mlstm_chunk(q, k, v, gp, gpt, state, rev, with_out):
    L = ML_CHUNK
    j_i, j_b = (1, 3) if rev else (0, 2)
    li_c = gp[:, j_i:j_i + 1]
    b_c = gp[:, j_b:j_b + 1]
    b_end = b_c[0:1, :] if rev else b_c[L - 1:L, :]
    ks = k * (HEAD_DIM ** -0.5)
    a = b_end - b_c + li_c
    m_loc = jnp.max(a, axis=0, keepdims=True)
    wgt = jnp.broadcast_to(jnp.exp(a - m_loc), (L, HEAD_DIM))
    ct_loc = _mm_tn(ks, jnp.concatenate([v * wgt, wgt], axis=1))
    ct, m = state
    m_new = jnp.maximum(b_end + m, m_loc)
    sp = jnp.exp(b_end + m - m_new)
    sl = jnp.exp(m_loc - m_new)
    new_state = (sp * ct + sl * ct_loc, m_new)
    if not with_out:
        return new_state, None
    li_r = gpt[j_i:j_i + 1, :]
    b_r = gpt[j_b:j_b + 1, :]
    t = lax.broadcasted_iota(jnp.int32, (L, L), 0)
    s = lax.broadcasted_iota(jnp.int32, (L, L), 1)
    incl = (s >= t) if rev else (s <= t)
    e = jnp.where(incl, li_r - b_r, -jnp.inf)
    c = jnp.broadcast_to(jnp.maximum(jnp.max(e, axis=1, keepdims=True), m), (L, L))
    p = jnp.exp(e - c) * _mm_nt(q, ks)
    inter = jnp.exp(m - c)
    pv = _mm(p, jnp.concatenate([v, jnp.ones_like(v)], axis=1))
    qc = _mm(q, ct)
    num = pv[:, :HEAD_DIM] + inter * qc[:, :HEAD_DIM]
    den = pv[:, HEAD_DIM:] + inter * qc[:, HEAD_DIM:]
    m_t = jnp.broadcast_to(b_c, (L, L)) + c
    return new_state, num / jnp.maximum(jnp.abs(den), jnp.exp(-m_t))


def _mlstm_kernel(*refs, ctx_out, n_ctx, n_lat):
    if ctx_out:
        (kc_ref, vc_ref, gpc_ref, gptc_ref, qc_ref, oc_ref, kl_ref, vl_ref, gpl_ref, gptl_ref, ql_ref, ol_ref,
         norm_ref, yc_ref, yl_ref, hfc, hbc, hfl, hbl) = refs
    else:
        (kc_ref, vc_ref, gpc_ref, gptc_ref, kl_ref, vl_ref, gpl_ref, gptl_ref, ql_ref, ol_ref,
         norm_ref, yl_ref, hfl, hbl) = refs
        qc_ref = oc_ref = yc_ref = hfc = hbc = None
    L = ML_CHUNK

    def run(k_ref, v_ref, gp_ref, gpt_ref, q_ref, hf, hb, n_chunks, st_f, st_b, with_out):
        def body(j, carry):
            st_f, st_b = carry
            rf = pl.ds(pl.multiple_of(j * L, L), L)
            jb = n_chunks - 1 - j
            rb = pl.ds(pl.multiple_of(jb * L, L), L)
            st_f, h_f = _mlstm_chunk(q_ref[rf, :] if with_out else None, k_ref[rf, :], v_ref[rf, :],
                                     gp_ref[rf, :], gpt_ref[j], st_f, False, with_out)
            st_b, h_b = _mlstm_chunk(q_ref[rb, :] if with_out else None, k_ref[rb, :], v_ref[rb, :],
                                     gp_ref[rb, :], gpt_ref[jb], st_b, True, with_out)
            if with_out:
                hf[rf, :] = h_f
                hb[rb, :] = h_b
            return st_f, st_b
        return lax.fori_loop(0, n_chunks, body, (st_f, st_b))

    zero = (jnp.zeros((HEAD_DIM, 2 * HEAD_DIM), F32), jnp.zeros((1, 1), F32))
    st_f, st_b = run(kc_ref, vc_ref, gpc_ref, gptc_ref, qc_ref, hfc, hbc, n_ctx, zero, zero, ctx_out)
    run(kl_ref, vl_ref, gpl_ref, gptl_ref, ql_ref, hfl, hbl, n_lat, st_f, st_b, True)

    def finish(hf, hb, o_ref, y_ref):
        y_ref[...] = (_sigmoid(o_ref[...]) * (_rms(hf[...] + hb[...]) * norm_ref[...])).astype(y_ref.dtype)

    finish(hfl, hbl, ol_ref, yl_ref)
    if ctx_out:
        finish(hfc, hbc, oc_ref, yc_ref)


def _seq_spec(rows, col):
    return pl.BlockSpec((None, rows, HEAD_DIM), lambda b, h: (b, 0, col // HEAD_DIM + h))


def _gp_specs(rows):
    return [pl.BlockSpec((None, rows, LANES), lambda b, h: (b, 0, h)),
            pl.BlockSpec((None, rows // ML_CHUNK, N_GATE, ML_CHUNK), lambda b, h: (b, 0, h, 0))]


def _mlstm(zc, gpc, gptc, zl, gpl, gptl, norm, ctx_out):
    nb, tc, _ = zc.shape
    tl = zl.shape[1]
    ctx_in = [zc, zc, gpc, gptc] + ([zc, zc] if ctx_out else [])
    ctx_specs = [_seq_spec(tc, C_ML_K), _seq_spec(tc, C_ML_V)] + _gp_specs(tc) + (
        [_seq_spec(tc, C_ML_Q), _seq_spec(tc, C_ML_O)] if ctx_out else [])
    lat_in = [zl, zl, gpl, gptl, zl, zl]
    lat_specs = [_seq_spec(tl, C_ML_K), _seq_spec(tl, C_ML_V)] + _gp_specs(tl) + [
        _seq_spec(tl, C_ML_Q), _seq_spec(tl, C_ML_O)]
    out_spec = lambda rows: pl.BlockSpec((None, rows, HEAD_DIM), lambda b, h: (b, 0, h))
    out_shape = lambda rows: jax.ShapeDtypeStruct((nb, rows, BRANCH_W), BF16)
    scr = lambda rows: [pltpu.VMEM((rows, HEAD_DIM), F32), pltpu.VMEM((rows, HEAD_DIM), F32)]
    res = pl.pallas_call(
        functools.partial(_mlstm_kernel, ctx_out=ctx_out, n_ctx=tc // ML_CHUNK, n_lat=tl // ML_CHUNK),
        grid=(nb, N_HEADS),
        in_specs=ctx_specs + lat_specs + [pl.BlockSpec((1, HEAD_DIM), lambda b, h: (0, h))],
        out_specs=([out_spec(tc)] if ctx_out else []) + [out_spec(tl)],
        out_shape=([out_shape(tc)] if ctx_out else []) + [out_shape(tl)],
        scratch_shapes=(scr(tc) if ctx_out else []) + scr(tl),
        compiler_params=_params(("parallel", "parallel")),
        name="mlstm",
    )(*ctx_in, *lat_in, norm)
    return (res[0], res[1]) if ctx_out else (None, res[0])


def _conv_silu(x, w3):
    rows = x.shape[0]
    t = lax.broadcasted_iota(jnp.int32, x.shape, 0)
    prev = jnp.where(t == 0, 0.0, pltpu.roll(x, 1, 0))
    nxt = jnp.where(t == rows - 1, 0.0, pltpu.roll(x, rows - 1, 0))
    return _silu(w3[0:1, :] * prev + w3[1:2, :] * x + w3[2:3, :] * nxt)


def _l2norm(x):
    return x * lax.rsqrt(jnp.sum(x * x, axis=-1, keepdims=True) + EPS)


def _chunk_totals(gc_r, rev):
    L = GD_CHUNK
    pos = lax.broadcasted_iota(jnp.int32, gc_r.shape, 1)
    tot = jnp.zeros_like(gc_r)
    for c in range(GD_SUPER // L):
        i = c * L if rev else c * L + L - 1
        tot = jnp.where((pos >= c * L) & (pos < (c + 1) * L), gc_r[:, i:i + 1], tot)
    return tot


def _gdn_prepare(blocks):
    t = lax.broadcasted_iota(jnp.int32, (GD_SUPER, GD_SUPER), 0)
    s = lax.broadcasted_iota(jnp.int32, (GD_SUPER, GD_SUPER), 1)
    x = t ^ s
    same = x < GD_CHUNK
    kk0 = [_mm_nt(k, k) for k, _, _, _, _ in blocks]
    qk0 = [None if qs is None else _mm_nt(qs, k) for k, _, qs, _, _ in blocks]
    kT = [k.T for k, _, _, _, _ in blocks]
    prob = []
    for i, (k, v, qs, gp, gpt) in enumerate(blocks):
        for rev in (False, True):
            j_b, j_g = (5, 7) if rev else (4, 6)
            beta = gp[:, j_b:j_b + 1]
            gc_c = gp[:, j_g:j_g + 1]
            gc_r = gpt[j_g:j_g + 1, :]
            incl = same & ((s >= t) if rev else (s <= t))
            strict = same & ((s > t) if rev else (s < t))
            decay = jnp.exp(jnp.where(incl, gc_c - gc_r, -jnp.inf))
            m = jnp.where(strict, (beta * kk0[i]) * decay, 0.0)
            prob.append(dict(i=i, rev=rev, beta=beta, gc_r=gc_r, decay=decay, m=m, eg=jnp.exp(gc_c)))
    tinv = [jnp.where(x == 0, 1.0, 0.0) - jnp.where(x == 1, p["m"], 0.0) for p in prob]
    for lvl in range(1, GD_CHUNK.bit_length() - 1):
        tc = [_mm(tinv[n], jnp.where((x >> lvl) == 1, p["m"], 0.0)) for n, p in enumerate(prob)]
        tinv = [tinv[n] - _mm(tc[n], tinv[n]) for n in range(len(prob))]
    out = [[] for _ in blocks]
    for n, p in enumerate(prob):
        k, v, qs, _, _ = blocks[p["i"]]
        y = _mm(tinv[n], jnp.concatenate([v * p["beta"], (k * p["beta"]) * p["eg"]], axis=1))
        kendT = (kT[p["i"]] * jnp.exp(_chunk_totals(p["gc_r"], p["rev"]) - p["gc_r"])).astype(BF16)
        qd = None if qs is None else (qs * p["eg"]).astype(BF16)
        qk = None if qs is None else (qk0[p["i"]] * p["decay"]).astype(BF16)
        out[p["i"]].append((y[:, :HEAD_DIM], y[:, HEAD_DIM:].astype(BF16), kendT, qd, qk))
    return out


def _gdn_kernel(*refs, ctx_out, n_ctx, n_lat):
    n_in = 16 if ctx_out else 14
    n_out = 2 if ctx_out else 1
    if ctx_out:
        (kc_ref, vc_ref, gpc_ref, gptc_ref, qc_ref, zc_ref, kl_ref, vl_ref, gpl_ref, gptl_ref, ql_ref, zl_ref,
         wq_ref, wk_ref, wv_ref, norm_ref, yc_ref, yl_ref) = refs[:n_in + n_out]
    else:
        (kc_ref, vc_ref, gpc_ref, gptc_ref, kl_ref, vl_ref, gpl_ref, gptl_ref, ql_ref, zl_ref,
         wq_ref, wk_ref, wv_ref, norm_ref, yl_ref) = refs[:n_in + n_out]
        qc_ref = zc_ref = yc_ref = None
    ks, vs, qs, gpa, gpta, o_f, o_b = refs[n_in + n_out:n_in + n_out + 7]
    per_dir = refs[n_in + n_out + 7:]
    u_s, w_s, kT_s, qd_s, qk_s = (per_dir[0:2], per_dir[2:4], per_dir[4:6], per_dir[6:8], per_dir[8:10])
    L = GD_CHUNK
    SB = GD_SUPER
    tc = n_ctx * SB
    tiles = SB // ML_CHUNK

    ks[0:tc, :] = _l2norm(_conv_silu(kc_ref[...], wk_ref[...]))
    vs[0:tc, :] = _conv_silu(vc_ref[...], wv_ref[...])
    ks[tc:, :] = _l2norm(_conv_silu(kl_ref[...], wk_ref[...]))
    vs[tc:, :] = _conv_silu(vl_ref[...], wv_ref[...])
    qs[tc:, :] = _l2norm(_conv_silu(ql_ref[...], wq_ref[...])) * (HEAD_DIM ** -0.5)
    if ctx_out:
        qs[0:tc, :] = _l2norm(_conv_silu(qc_ref[...], wq_ref[...])) * (HEAD_DIM ** -0.5)
    gpa[0:tc, :] = gpc_ref[...]
    gpa[tc:, :] = gpl_ref[...]
    gpta[0:n_ctx * tiles] = gptc_ref[...]
    gpta[n_ctx * tiles:] = gptl_ref[...]

    def prepare(js, with_out):
        rows = [pl.ds(pl.multiple_of(j * SB, SB), SB) for j in js]
        blocks = [(ks[r, :], vs[r, :], qs[r, :] if with_out else None, gpa[r, :],
                   jnp.concatenate([gpta[j * tiles + i] for i in range(tiles)], axis=1))
                  for j, r in zip(js, rows)]
        for j, r, res in zip(js, rows, _gdn_prepare(blocks)):
            for d, (u, w, kT, qd, qk) in enumerate(res):
                u_s[d][r, :] = u
                w_s[d][r, :] = w
                kT_s[d][j] = kT
                if with_out:
                    qd_s[d][r, :] = qd
                    qk_s[d][r, :] = qk

    def prepare_loop(lo, n, with_out):
        group = GD_PREP_GROUP if n % GD_PREP_GROUP == 0 else 1

        def body(i, carry):
            prepare([lo + i * group + g for g in range(group)], with_out)
            return carry
        lax.fori_loop(0, n // group, body, 0)

    def chunk_rows(j, c):
        return pl.ds(pl.multiple_of(j * SB + c * L, L), L)

    def step_load(d, j, c, with_out):
        rows = chunk_rows(j, c)
        cols = slice(c * L, (c + 1) * L)
        edge = gpa[pl.ds(pl.multiple_of(j * SB + c * L, L) + (0 if d else L - 1), 1), :]
        g_end = jnp.exp(edge[:, 7:8] if d else edge[:, 6:7])
        out_ops = (qd_s[d][rows, :], qk_s[d][rows, cols]) if with_out else None
        return g_end, u_s[d][rows, :], w_s[d][rows, :], kT_s[d][j, :, cols], out_ops

    def step_compute(ops, S):
        g_end, u, w, kT, out_ops = ops
        Sb = S.astype(BF16)
        v_new = (u - jnp.dot(w, Sb, preferred_element_type=F32)).astype(BF16)
        out = None
        if out_ops is not None:
            out = (jnp.dot(out_ops[0], Sb, preferred_element_type=F32)
                   + jnp.dot(out_ops[1], v_new, preferred_element_type=F32))
        return S * g_end + jnp.dot(kT, v_new, preferred_element_type=F32), out

    def chunk_pair(lo, n, i, c):
        return (lo + i, c), (lo + n - 1 - i, SB // L - 1 - c)

    def scan(lo, n, first, last, carry, with_out):
        def body(i, carry):
            for c in range(SB // L):
                where = chunk_pair(lo, n, i, c)
                res = [step_compute(step_load(d, *where[d], with_out), carry[d]) for d in range(2)]
                if with_out:
                    o_f[chunk_rows(*where[0]), :] = res[0][1]
                    o_b[chunk_rows(*where[1]), :] = res[1][1]
                carry = (res[0][0], res[1][0])
            return carry
        return lax.fori_loop(first, last, body, carry)

    zero = jnp.zeros((HEAD_DIM, HEAD_DIM), F32)
    prepare_loop(0, n_ctx, ctx_out)
    prepare_loop(n_ctx, n_lat, True)
    carry = scan(0, n_ctx, 0, n_ctx, (zero, zero), ctx_out)
    scan(n_ctx, n_lat, 0, n_lat, carry, True)

    def finish(rows, z_ref, y_ref):
        y_ref[...] = ((_rms(o_f[rows, :] + o_b[rows, :]) * norm_ref[...]) * _silu(z_ref[...])).astype(y_ref.dtype)

    finish(slice(tc, None), zl_ref, yl_ref)
    if ctx_out:
        finish(slice(0, tc), zc_ref, yc_ref)


def _gdn(zc, gpc, gptc, zl, gpl, gptl, conv, norm, ctx_out):
    nb, tc, _ = zc.shape
    tl = zl.shape[1]
    assert tc % GD_SUPER == 0 and tl % GD_SUPER == 0, "sequences are processed in whole 256-token superblocks"
    ctx_in = [zc, zc, gpc, gptc] + ([zc, zc] if ctx_out else [])
    ctx_specs = [_seq_spec(tc, C_GD_K), _seq_spec(tc, C_GD_V)] + _gp_specs(tc) + (
        [_seq_spec(tc, C_GD_Q), _seq_spec(tc, C_GD_Z)] if ctx_out else [])
    lat_in = [zl, zl, gpl, gptl, zl, zl]
    lat_specs = [_seq_spec(tl, C_GD_K), _seq_spec(tl, C_GD_V)] + _gp_specs(tl) + [
        _seq_spec(tl, C_GD_Q), _seq_spec(tl, C_GD_Z)]
    conv_spec = lambda off: pl.BlockSpec((3, HEAD_DIM), lambda b, h: (0, off // HEAD_DIM + h))
    out_spec = lambda rows: pl.BlockSpec((None, rows, HEAD_DIM), lambda b, h: (b, 0, h))
    out_shape = lambda rows: jax.ShapeDtypeStruct((nb, rows, BRANCH_W), BF16)
    rows = tc + tl
    seq = lambda dt: pltpu.VMEM((rows, HEAD_DIM), dt)
    two = lambda spec: [spec, spec]
    scratch = ([seq(F32), seq(F32), seq(F32), seq(F32), pltpu.VMEM((rows // ML_CHUNK, N_GATE, ML_CHUNK), F32),
                seq(F32), seq(F32)]
               + two(seq(F32)) + two(seq(BF16)) + two(pltpu.VMEM((rows // GD_SUPER, HEAD_DIM, GD_SUPER), BF16))
               + two(seq(BF16)) + two(pltpu.VMEM((rows, GD_SUPER), BF16)))
    res = pl.pallas_call(
        functools.partial(_gdn_kernel, ctx_out=ctx_out, n_ctx=tc // GD_SUPER, n_lat=tl // GD_SUPER),
        grid=(nb, N_HEADS),
        in_specs=ctx_specs + lat_specs + [conv_spec(0), conv_spec(BRANCH_W), conv_spec(2 * BRANCH_W),
                                          pl.BlockSpec((1, HEAD_DIM), lambda b, h: (0, 0))],
        out_specs=([out_spec(tc)] if ctx_out else []) + [out_spec(tl)],
        out_shape=([out_shape(tc)] if ctx_out else []) + [out_shape(tl)],
        scratch_shapes=scratch,
        compiler_params=_params(("parallel", "parallel")),
        name="gdn",
    )(*ctx_in, *lat_in, conv, conv, conv, norm)
    return (res[0], res[1]) if ctx_out else (None, res[0])


def _rope(x, cos, sin):
    lane = lax.broadcasted_iota(jnp.int32, x.shape, 1)
    quarter = HEAD_DIM // 4
    partner = jnp.where((lane & (2 * quarter - 1)) < quarter,
                        pltpu.roll(x, HEAD_DIM - quarter, 1), pltpu.roll(x, quarter, 1))
    return x * cos + partner * sin


def _attn_kernel(*refs, latent, tc):
    if latent:
        (q_ref, kc_ref, vc_ref, kl_ref, vl_ref, cos_ref, sin_ref, cosq_ref, sinq_ref, qn_ref, kn_ref,
         o_ref, k_scr, v_scr) = refs
    else:
        q_ref, kc_ref, vc_ref, qn_ref, kn_ref, o_ref, k_scr, v_scr = refs

    @pl.when(pl.program_id(2) == 0)
    def _():
        k_scr[0:tc, :] = (_rms(kc_ref[...]) * kn_ref[...]).astype(BF16)
        v_scr[0:tc, :] = vc_ref[...].astype(BF16)
        if latent:
            kl = _rope(_rms(kl_ref[...]) * kn_ref[...], cos_ref[...], sin_ref[...])
            k_scr[tc:, :] = kl.astype(BF16)
            v_scr[tc:, :] = vl_ref[...].astype(BF16)

    scale = HEAD_DIM ** -0.5
    for g in range(AT_Q_HEADS // AT_KV_HEADS):
        cols = slice(g * HEAD_DIM, (g + 1) * HEAD_DIM)
        q = _rms(q_ref[:, cols]) * qn_ref[...]
        if latent:
            q = _rope(q, cosq_ref[...], sinq_ref[...])
        s = lax.dot_general((q * scale).astype(BF16), k_scr[...], (((1,), (1,)), ((), ())),
                            preferred_element_type=F32)
        p = jnp.exp(s - jnp.max(s, axis=-1, keepdims=True))
        l = jnp.sum(p, axis=-1, keepdims=True)
        o_ref[:, cols] = (jnp.dot(p.astype(BF16), v_scr[...], preferred_element_type=F32) / l).astype(o_ref.dtype)


def _attention(zq, zc, zl, cos, sin, qn, kn):
    latent = zl is not None
    nb, tq_total, _ = zq.shape
    tc = zc.shape[1]
    tk = tc + (zl.shape[1] if latent else 0)
    gq = (AT_Q_HEADS // AT_KV_HEADS) * HEAD_DIM
    head_spec = lambda rows, col: pl.BlockSpec((None, rows, HEAD_DIM), lambda b, h, i: (b, 0, col // HEAD_DIM + h))
    vec_spec = pl.BlockSpec((1, HEAD_DIM), lambda b, h, i: (0, 0))
    in_specs = [pl.BlockSpec((None, TQ_ATTN, gq), lambda b, h, i: (b, i, C_AT_Q // gq + h)),
                head_spec(tc, C_AT_K), head_spec(tc, C_AT_V)]
    args = [zq, zc, zc]
    if latent:
        tl = zl.shape[1]
        in_specs += [head_spec(tl, C_AT_K), head_spec(tl, C_AT_V),
                     pl.BlockSpec((tl, HEAD_DIM), lambda b, h, i: (0, 0)),
                     pl.BlockSpec((tl, HEAD_DIM), lambda b, h, i: (0, 0)),
                     pl.BlockSpec((TQ_ATTN, HEAD_DIM), lambda b, h, i: (i, 0)),
                     pl.BlockSpec((TQ_ATTN, HEAD_DIM), lambda b, h, i: (i, 0))]
        args += [zl, zl, cos, sin, cos, sin]
    return pl.pallas_call(
        functools.partial(_attn_kernel, latent=latent, tc=tc),
        grid=(nb, AT_KV_HEADS, tq_total // TQ_ATTN),
        in_specs=in_specs + [vec_spec, vec_spec],
        out_specs=pl.BlockSpec((None, TQ_ATTN, gq), lambda b, h, i: (b, i, h)),
        out_shape=jax.ShapeDtypeStruct((nb, tq_total, BRANCH_W), BF16),
        scratch_shapes=[pltpu.VMEM((tk, HEAD_DIM), BF16), pltpu.VMEM((tk, HEAD_DIM), BF16)],
        compiler_params=_params(("parallel", "parallel", "arbitrary")),
        name="attention",
    )(*args, qn, kn)


def _merge_kernel(h_ref, xcol_ref, modcol_ref, y0_ref, y1_ref, y2_ref, y3_ref,
                  wg0_ref, wg1_ref, wg2_ref, wg3_ref, wb_ref, wo_ref, o_ref, m_scr):
    s = pl.program_id(2)
    n_gate = D_MODEL // TN_MERGE
    ys = (y0_ref, y1_ref, y2_ref, y3_ref)
    wgs = (wg0_ref, wg1_ref, wg2_ref, wg3_ref)

    @pl.when(s < n_gate)
    def _():
        h = h_ref[...]
        acc = jnp.zeros((TM_MERGE, TN_MERGE), F32)
        for n in range(N_BRANCH):
            gate = jnp.dot(h, wgs[n][...], preferred_element_type=F32)
            acc = acc + _sigmoid(gate) * jnp.dot(ys[n][...], wb_ref[n], preferred_element_type=F32)
        m_scr[s] = acc.astype(BF16)

    @pl.when(s >= n_gate)
    def _():
        acc = jnp.zeros((TM_MERGE, TN_OUT), F32)
        for kk in range(n_gate):
            acc = acc + jnp.dot(m_scr[kk], wo_ref[kk * TN_MERGE:(kk + 1) * TN_MERGE, :],
                                preferred_element_type=F32)
        o_ref[...] = xcol_ref[...] + modcol_ref[5:6, :] * acc


def _merge(x, h, mod, ys, w_gate, w_branch, w_out, layer, mod_row):
    groups, rows, _ = x.shape
    n_gate = D_MODEL // TN_MERGE
    n_out = D_MODEL // TN_OUT
    mrow = (lambda b: b) if mod_row is None else (lambda b: mod_row)
    gate_col = lambda s: jnp.minimum(s, n_gate - 1)
    out_col = lambda s: jnp.maximum(s - n_gate, 0)
    y_spec = pl.BlockSpec((None, TM_MERGE, BRANCH_W), lambda b, i, s: (b, i, 0))
    wg_spec = lambda n: pl.BlockSpec((None, None, D_MODEL, TN_MERGE),
                                     lambda b, i, s: (layer, n * n_gate + gate_col(s), 0, 0))
    return pl.pallas_call(
        _merge_kernel,
        grid=(groups, rows // TM_MERGE, n_gate + n_out),
        in_specs=[pl.BlockSpec((None, TM_MERGE, D_MODEL), lambda b, i, s: (b, i, 0)),
                  pl.BlockSpec((None, TM_MERGE, TN_OUT), lambda b, i, s: (b, i, out_col(s))),
                  pl.BlockSpec((None, N_MOD, TN_OUT), lambda b, i, s: (mrow(b), 0, out_col(s))),
                  y_spec, y_spec, y_spec, y_spec,
                  wg_spec(0), wg_spec(1), wg_spec(2), wg_spec(3),
                  pl.BlockSpec((None, N_BRANCH, BRANCH_W, TN_MERGE), lambda b, i, s: (layer, 0, 0, gate_col(s))),
                  pl.BlockSpec((None, None, D_MODEL, TN_OUT), lambda b, i, s: (layer, out_col(s), 0, 0))],
        out_specs=pl.BlockSpec((None, TM_MERGE, TN_OUT), lambda b, i, s: (b, i, out_col(s))),
        out_shape=jax.ShapeDtypeStruct(x.shape, F32),
        scratch_shapes=[pltpu.VMEM((n_gate, TM_MERGE, TN_MERGE), BF16)],
        compiler_params=_params(("parallel", "parallel", "arbitrary")),
        name="merge",
    )(h, x, mod, *ys, w_gate, w_gate, w_gate, w_gate, w_branch, w_out)


def _rope_tables(seq):
    n = HEAD_DIM // 4
    inv = ROPE_THETA ** (-jnp.arange(n, dtype=F32) / n)
    pos = jnp.arange(seq)
    ar = (pos // GRID_W).astype(F32)[:, None] * inv
    ac = (pos % GRID_W).astype(F32)[:, None] * inv
    cos = jnp.concatenate([jnp.cos(ar), jnp.cos(ar), jnp.cos(ac), jnp.cos(ac)], axis=1)
    sin = jnp.concatenate([-jnp.sin(ar), jnp.sin(ar), -jnp.sin(ac), jnp.sin(ac)], axis=1)
    return cos, sin


def _gate_params(ml_if_bias, gd_a_log, gd_dt_bias):
    zeros = jnp.zeros((2, N_HEADS), F32)
    bias = jnp.concatenate([ml_if_bias.reshape(4, N_HEADS), zeros, gd_dt_bias], axis=0).T
    alog = jnp.concatenate([jnp.zeros((6, N_HEADS), F32), gd_a_log], axis=0).T
    pad = lambda a: jnp.pad(a.reshape(1, N_HEADS * N_GATE), ((0, 0), (0, LANES - N_HEADS * N_GATE)))
    col = lambda a: a.reshape(N_HEADS * N_GATE, 1)
    return pad(bias), pad(alog), col(bias), col(alog)


def _gate_weights(w_in):
    depth = w_in.shape[0]
    g = jnp.concatenate([w_in[..., R_ML_IF:R_ML_IF + 16], w_in[..., R_GD_BA:R_GD_BA + 16]], axis=-1)
    g = g.reshape(depth, D_MODEL, N_GATE, N_HEADS).transpose(0, 1, 3, 2).reshape(depth, D_MODEL, N_HEADS * N_GATE)
    w_gc = jnp.pad(g, ((0, 0), (0, 0), (0, LANES - N_HEADS * N_GATE)))
    w_gr = g.transpose(0, 2, 1)
    return w_gc.astype(BF16), w_gr.astype(BF16)


def kernel(x, c, ctx, c_ctx, mod_w, mod_b, ffn1_norm, ffn1_w_in, ffn1_w_out, mix_norm, w_in, sg_norm, sg_w, sg_b, ml_if_bias, ml_norm, gd_conv, gd_a_log, gd_dt_bias, gd_norm, at_q_norm, at_k_norm, w_branch, w_out, ffn2_norm, ffn2_w_in, ffn2_w_out):
    nb, seq, _ = x.shape
    tc = ctx.shape[1]
    depth = mod_w.shape[0]
    ctx_row = nb
    cos, sin = _rope_tables(seq)

    cc = jnp.zeros((MOD_ROWS, D_MODEL), F32).at[:nb].set(c).at[ctx_row].set(c_ctx)
    mods = _modulation(cc, mod_w, mod_b).reshape(depth, MOD_ROWS, N_MOD, D_MODEL)

    f1_in, f2_in = _to_bf16(ffn1_w_in, TF_FFN), _to_bf16(ffn2_w_in, TF_FFN)
    f1_out, f2_out = _to_bf16(ffn1_w_out), _to_bf16(ffn2_w_out)
    w_main, w_gate = _split_w_in(w_in)
    w_gc_all, w_gr_all = _gate_weights(w_in)
    wb, wo = w_branch.astype(BF16), _to_bf16(w_out, TN_OUT)
    flat = lambda a: a.reshape(1, nb * tc, a.shape[-1])
    unflat = lambda a: a.reshape(nb, tc, a.shape[-1])
    xc = ctx
    for l in range(depth):
        last = l == depth - 1
        ctx_out = not last
        mod = mods[l]
        w_gc, w_gr = w_gc_all[l], w_gr_all[l]
        bias_c, alog_c, bias_r, alog_r = _gate_params(ml_if_bias[l], gd_a_log[l], gd_dt_bias[l])
        sgw = sg_w[l].astype(BF16)
        sg_bias = jnp.repeat(sg_b[l].T, BRANCH_W // SG_GROUPS, axis=1)
        sgn = sg_norm[l].reshape(1, BRANCH_W)
        mln = ml_norm[l].reshape(1, BRANCH_W)
        gdn_g = gd_norm[l].reshape(1, HEAD_DIM)
        qn, kn = at_q_norm[l].reshape(1, HEAD_DIM), at_k_norm[l].reshape(1, HEAD_DIM)

        x = _ffn(x, mod, ffn1_norm[l], f1_in, f1_out, l, 0, None)
        xc = unflat(_ffn(flat(xc), mod, ffn1_norm[l], f1_in, f1_out, l, 0, ctx_row))

        zl, gcl, grl, hl = _inproj(x, mod, mix_norm[l], w_main, l, w_gc, w_gr, N_MAIN, None)
        zc, gcc, grc, hc = _inproj(flat(xc), mod, mix_norm[l], w_main, l, w_gc, w_gr,
                                   N_MAIN if ctx_out else KV_MAIN, ctx_row)
        zc = unflat(zc)
        gcc = unflat(gcc)
        grc = grc.reshape(N_HEADS * N_GATE, nb, tc).transpose(1, 0, 2)
        gpl, gptl = _gateprep(gcl, grl, bias_c, alog_c, bias_r, alog_r)
        gpc, gptc = _gateprep(gcc, grc, bias_c, alog_c, bias_r, alog_r)

        y_sg_l = _spatial_gating(zl, sgn, sgw, sg_bias)
        y_ml_c, y_ml_l = _mlstm(zc, gpc, gptc, zl, gpl, gptl, mln, ctx_out)
        y_gd_c, y_gd_l = _gdn(zc, gpc, gptc, zl, gpl, gptl, gd_conv[l], gdn_g, ctx_out)
        y_at_l = _attention(zl, zc, zl, cos, sin, qn, kn)
        x = _merge(x, hl, mod, [y_sg_l, y_ml_l, y_gd_l, y_at_l], w_gate, wb, wo, l, None)
        if ctx_out:
            y_sg_c = _spatial_gating(zc, sgn, sgw, sg_bias)
            y_at_c = _attention(zc, zc, None, None, None, qn, kn)
            xc = unflat(_merge(flat(xc), hc, mod, [flat(y) for y in (y_sg_c, y_ml_c, y_gd_c, y_at_c)],
                               w_gate, wb, wo, l, ctx_row))

        x = _ffn(x, mod, ffn2_norm[l], f2_in, f2_out, l, 6, None)
        if ctx_out:
            xc = unflat(_ffn(flat(xc), mod, ffn2_norm[l], f2_in, f2_out, l, 6, ctx_row))
    return x
```
